```python
import math
import jax, jax.numpy as jnp
from jax import lax
import numpy as np

D_MODEL = 1024
BATCH = 8
SEQ = 2048
DEPTH = 4
DEC_BATCH = 128
DEC_SEQ = 1
PAST_LEN = 16384
PAGE_SIZE = 128

D_MIX = D_MODEL
GROUP_W = D_MIX // 4
S5_CH = 16
S5_GROUPS = GROUP_W // S5_CH
S5_P = 64
SSD_HEADDIM = 64
SSD_HEADS = GROUP_W // SSD_HEADDIM
SSD_NGROUPS = 2
SSD_N = 64
SSD_CONV = 4
SSD_XBC = GROUP_W + 2 * SSD_NGROUPS * SSD_N
SSD_CHUNK = 64
HG_EXPAND = 64
HG_HEADS = GROUP_W // HG_EXPAND
HG_DV = GROUP_W // HG_HEADS
LB_FLOOR = 1e-30
GLA_HEADS = 4
GLA_DK = GROUP_W // 2 // GLA_HEADS
GLA_DV = GROUP_W // GLA_HEADS
GLA_RANK = 16
GLA_TAU = 16.0
VEC_CHUNK = 32
D_FF = 4 * D_MODEL
EPS = 1e-6

SPLITS = (GROUP_W,
          GROUP_W, SSD_XBC, SSD_HEADS,
          HG_HEADS * HG_EXPAND, HG_HEADS * HG_EXPAND, HG_HEADS * HG_DV, GROUP_W,
          GLA_HEADS * GLA_DK, GLA_HEADS * GLA_DK, GLA_HEADS * GLA_DV, GROUP_W, GLA_RANK)
N_IN = sum(SPLITS)
SPLIT_POINTS = tuple(int(v) for v in np.cumsum(SPLITS)[:-1])

kernel_name = 'hybrid_s5_ssd_hgrn2_gla_decode_step'


def rmsnorm(x, g):
    x32 = x.astype(jnp.float32)
    y = x32 * lax.rsqrt(jnp.mean(x32 * x32, axis=-1, keepdims=True) + EPS)
    return (y * g.astype(jnp.float32)).astype(x.dtype)


def to_chunks(t, c):
    b, l = t.shape[:2]
    return jnp.moveaxis(t.reshape((b, l // c, c) + t.shape[2:]), 1, 0)


def from_chunks(t):
    n, b, c = t.shape[:3]
    return jnp.moveaxis(t, 0, 1).reshape((b, n * c) + t.shape[3:])


def masked_decay(seg, mask):
    return jnp.where(mask, jnp.exp(jnp.where(mask, seg, 0.0)), 0.0)


def ssd_scan(q, k, v, log_a, h0):
    c = math.gcd(q.shape[1], SSD_CHUNK)
    mask = jnp.tril(jnp.ones((c, c), bool))[None, :, :, None]

    def step(h, inp):
        qc, kc, vc, lac = inp
        cum = jnp.cumsum(lac, axis=1)
        seg = cum[:, :, None, :] - cum[:, None, :, :]
        decay = masked_decay(seg, mask)
        scores = jnp.einsum('bihn,bjhn->bijh', qc, kc) * decay
        o = (jnp.einsum('bijh,bjhp->bihp', scores, vc)
             + jnp.einsum('bihn,bhnp->bihp', qc, h) * jnp.exp(cum)[..., None])
        w = jnp.exp(cum[:, -1:, :] - cum)
        h_new = (jnp.exp(cum[:, -1])[:, :, None, None] * h
                 + jnp.einsum('bjhn,bjhp->bhnp', kc * w[..., None], vc))
        return h_new, o

    h_last, o = lax.scan(step, h0, (to_chunks(q, c), to_chunks(k, c), to_chunks(v, c), to_chunks(log_a, c)))
    return from_chunks(o), h_last


def gla_scan(q, k, v, log_a, h0):
    c = math.gcd(q.shape[1], VEC_CHUNK)
    mask = jnp.tril(jnp.ones((c, c), bool))[None, :, :, None, None]

    def step(h, inp):
        qc, kc, vc, lac = inp
        cum = jnp.cumsum(lac, axis=1)
        seg = cum[:, :, None] - cum[:, None]
        decay = masked_decay(seg, mask)
        scores = jnp.einsum('bihk,bjhk,bijhk->bijh', qc, kc, decay)
        o = (jnp.einsum('bijh,bjhv->bihv', scores, vc)
             + jnp.einsum('bihk,bhkv->bihv', qc * jnp.exp(cum), h))
        last = cum[:, -1]
        h_new = (jnp.exp(last)[..., None] * h
                 + jnp.einsum('bjhk,bjhv->bhkv', kc * jnp.exp(last[:, None] - cum), vc))
        return h_new, o

    h_last, o = lax.scan(step, h0, (to_chunks(q, c), to_chunks(k, c), to_chunks(v, c), to_chunks(log_a, c)))
    return from_chunks(o), h_last


def s5_mixer(u, h0_re, h0_im, lam_re, lam_im, log_dt, b_re, b_im, c_re, c_im, d, w_glu, b_glu):
    f32 = jnp.float32
    bsz, l, _ = u.shape
    lam = lax.complex(lam_re.astype(f32), lam_im.astype(f32))
    lam_bar = jnp.exp(lam * jnp.exp(log_dt.astype(f32))[:, None])
    b_bar = ((lam_bar - 1.0) / lam)[..., None] * lax.complex(b_re.astype(f32), b_im.astype(f32))
    c_mat = lax.complex(c_re.astype(f32), c_im.astype(f32))
    ug = u.reshape(bsz, l, S5_GROUPS, S5_CH).astype(jnp.complex64)
    bu = jnp.einsum('gpc,blgc->blgp', b_bar, ug)
    h0 = lax.complex(h0_re.astype(f32), h0_im.astype(f32))
    bu = bu.at[:, 0].add(lam_bar * h0)
    a = jnp.broadcast_to(lam_bar, bu.shape)
    _, h = lax.associative_scan(lambda e1, e2: (e1[0] * e2[0], e2[0] * e1[1] + e2[1]), (a, bu), axis=1)
    y = jnp.einsum('gcp,blgp->blgc', c_mat, h).real.reshape(bsz, l, GROUP_W) + d * u
    z = jax.nn.gelu(y)
    out = z * jax.nn.sigmoid(z @ w_glu + b_glu)
    return out, h[:, -1].real, h[:, -1].imag


def ssd_mixer(z, xbc, dt_raw, conv_buf, h0, conv_w, conv_b, dt_bias, a_log, d_skip, norm_g):
    bsz, l, _ = xbc.shape
    full = jnp.concatenate([conv_buf.astype(jnp.float32), xbc], axis=1)
    conv = sum(full[:, j:j + l] * conv_w[j] for j in range(SSD_CONV)) + conv_b
    new_buf = full[:, l:]
    act = jax.nn.silu(conv)
    x, bm, cm = jnp.split(act, [GROUP_W, GROUP_W + SSD_NGROUPS * SSD_N], axis=-1)
    x = x.reshape(bsz, l, SSD_HEADS, SSD_HEADDIM)
    rep = SSD_HEADS // SSD_NGROUPS
    bm = jnp.repeat(bm.reshape(bsz, l, SSD_NGROUPS, SSD_N), rep, axis=2)
    cm = jnp.repeat(cm.reshape(bsz, l, SSD_NGROUPS, SSD_N), rep, axis=2)
    dt = jax.nn.softplus(dt_raw + dt_bias)
    a = -jnp.exp(a_log.astype(jnp.float32))
    y, h_last = ssd_scan(cm, bm, x * dt[..., None], dt * a, h0.astype(jnp.float32))
    y = y + d_skip[:, None] * x
    y = rmsnorm(y.reshape(bsz, l, GROUP_W) * jax.nn.silu(z), norm_g)
    return y, new_buf, h_last


def hgrn2_mixer(q, f, i, gate, h0, lb, norm_g):
    bsz, l, _ = q.shape
    sh = (bsz, l, HG_HEADS, HG_EXPAND)
    q = jax.nn.silu(q).reshape(sh)
    fz = f.reshape(sh)
    lb = lb.reshape(HG_HEADS, HG_EXPAND)
    log_lb = jnp.log(jnp.maximum(lb, LB_FLOOR))
    log_f = jnp.logaddexp(jax.nn.log_sigmoid(fz), log_lb + jax.nn.log_sigmoid(-fz))
    k = (1.0 - lb) * jax.nn.sigmoid(-fz)
    v = i.reshape(bsz, l, HG_HEADS, HG_DV)
    o, h_last = gla_scan(q, k, v, log_f, h0.astype(jnp.float32))
    o = rmsnorm(o, norm_g.reshape(HG_HEADS, HG_DV)).reshape(bsz, l, GROUP_W) * jax.nn.silu(gate)
    return o, h_last


def gla_mixer(q, k, v, gate, lr, h0, w_gk2, b_gk, norm_g):
    bsz, l, _ = q.shape
    q = q.reshape(bsz, l, GLA_HEADS, GLA_DK) * (GLA_DK ** -0.5)
    k = k.reshape(bsz, l, GLA_HEADS, GLA_DK)
    v = v.reshape(bsz, l, GLA_HEADS, GLA_DV)
    log_a = jax.nn.log_sigmoid((lr @ w_gk2 + b_gk).reshape(bsz, l, GLA_HEADS, GLA_DK)) / GLA_TAU
    o, h_last = gla_scan(q, k, v, log_a, h0.astype(jnp.float32))
    o = rmsnorm(o, norm_g).reshape(bsz, l, GROUP_W) * jax.nn.silu(gate)
    return o, h_last


def layer(x, states, p):
    s5_re0, s5_im0, conv0, ssd0, hg0, gla0 = states
    hn = rmsnorm(x, p['norm_mix_g'])
    proj = (hn @ p['w_in']).astype(jnp.float32)
    (u, ssd_z, ssd_xbc, ssd_dt, hg_q, hg_f, hg_i, hg_gate,
     gla_q, gla_k, gla_v, gla_gate, gla_lr) = jnp.split(proj, SPLIT_POINTS, axis=-1)
    o_s5, s5_re1, s5_im1 = s5_mixer(u, s5_re0, s5_im0, p['s5_lam_re'], p['s5_lam_im'], p['s5_log_dt'],
                                    p['s5_b_re'], p['s5_b_im'], p['s5_c_re'], p['s5_c_im'],
                                    p['s5_d'], p['s5_w_glu'], p['s5_b_glu'])
    o_ssd, conv1, ssd1 = ssd_mixer(ssd_z, ssd_xbc, ssd_dt, conv0, ssd0, p['ssd_conv_w'], p['ssd_conv_b'],
                                   p['ssd_dt_bias'], p['ssd_a_log'], p['ssd_d'], p['ssd_norm_g'])
    o_hg, hg1 = hgrn2_mixer(hg_q, hg_f, hg_i, hg_gate, hg0, p['hg_lb'], p['hg_norm_g'])
    o_gla, gla1 = gla_mixer(gla_q, gla_k, gla_v, gla_gate, gla_lr, gla0, p['gla_w_gk2'], p['gla_b_gk'], p['gla_norm_g'])
    mix = jnp.concatenate([o_s5, o_ssd, o_hg, o_gla], axis=-1).astype(x.dtype)
    x = x + mix @ p['w_out']
    hn = rmsnorm(x, p['norm_mlp_g'])
    x = x + jnp.square(jax.nn.relu(hn @ p['w_up'])) @ p['w_down']
    return x, (s5_re1, s5_im1, conv1, ssd1, hg1, gla1)


def setup_inputs(seed: int = 0) -> dict:
    key = jax.random.key(seed)
    ks = iter(jax.random.split(key, 48))
    f32 = jnp.float32

    def nrm(shape, scale):
        return jax.random.normal(next(ks), shape, f32) * scale

    def gain(shape):
        return 1.0 + nrm(shape, 0.02)

    lam_im_base = jnp.pi * jnp.arange(S5_P, dtype=f32)
    dt_ssd = jnp.exp(jax.random.uniform(next(ks), (DEPTH, SSD_HEADS), f32, math.log(1e-3), math.log(1e-1)))
    return {
        'x_prompt': nrm((BATCH, SEQ, D_MODEL), 1.0),
        'x_sample': nrm((DEC_BATCH, DEC_SEQ, D_MODEL), 1.0),
        'state_s5_re': nrm((DEPTH, DEC_BATCH, S5_GROUPS, S5_P), 0.5),
        'state_s5_im': nrm((DEPTH, DEC_BATCH, S5_GROUPS, S5_P), 0.5),
        'state_ssd_conv': nrm((DEPTH, DEC_BATCH, SSD_CONV - 1, SSD_XBC), 1.0),
        'state_ssd': nrm((DEPTH, DEC_BATCH, SSD_HEADS, SSD_N, SSD_HEADDIM), 0.1),
        'state_hgrn': nrm((DEPTH, DEC_BATCH, HG_HEADS, HG_EXPAND, HG_DV), 0.3),
        'state_gla': nrm((DEPTH, DEC_BATCH, GLA_HEADS, GLA_DK, GLA_DV), 0.3),
        'norm_mix_g': gain((DEPTH, D_MODEL)),
        'w_in': nrm((DEPTH, D_MODEL, N_IN), D_MODEL ** -0.5),
        's5_lam_re': -0.5 + nrm((DEPTH, S5_GROUPS, S5_P), 0.01),
        's5_lam_im': lam_im_base + nrm((DEPTH, S5_GROUPS, S5_P), 0.01),
        's5_log_dt': jax.random.uniform(next(ks), (DEPTH, S5_GROUPS), f32, math.log(1e-3), math.log(1e-1)),
        's5_b_re': nrm((DEPTH, S5_GROUPS, S5_P, S5_CH), (2 * S5_CH) ** -0.5),
        's5_b_im': nrm((DEPTH, S5_GROUPS, S5_P, S5_CH), (2 * S5_CH) ** -0.5),
        's5_c_re': nrm((DEPTH, S5_GROUPS, S5_CH, S5_P), (2 * S5_P) ** -0.5),
        's5_c_im': nrm((DEPTH, S5_GROUPS, S5_CH, S5_P), (2 * S5_P) ** -0.5),
        's5_d': nrm((DEPTH, GROUP_W), 1.0),
        's5_w_glu': nrm((DEPTH, GROUP_W, GROUP_W), GROUP_W ** -0.5),
        's5_b_glu': nrm((DEPTH, GROUP_W), 0.01),
        'ssd_conv_w': nrm((DEPTH, SSD_CONV, SSD_XBC), SSD_CONV ** -0.5),
        'ssd_conv_b': nrm((DEPTH, SSD_XBC), 0.01),
        'ssd_dt_bias': dt_ssd + jnp.log(-jnp.expm1(-dt_ssd)),
        'ssd_a_log': jnp.log(jax.random.uniform(next(ks), (DEPTH, SSD_HEADS), f32, 1.0, 16.0)),
        'ssd_d': gain((DEPTH, SSD_HEADS)),
        'ssd_norm_g': gain((DEPTH, GROUP_W)),
        'hg_lb_logits': nrm((DEPTH, HG_HEADS * HG_EXPAND), 0.1),
        'hg_norm_g': gain((DEPTH, HG_HEADS * HG_DV)),
        'gla_w_gk2': nrm((DEPTH, GLA_RANK, GLA_HEADS * GLA_DK), GLA_RANK ** -0.5),
        'gla_b_gk': nrm((DEPTH, GLA_HEADS * GLA_DK), 0.01),
        'gla_norm_g': gain((DEPTH, GLA_DV)),
        'w_out': nrm((DEPTH, D_MIX, D_MODEL), D_MIX ** -0.5),
        'norm_mlp_g': gain((DEPTH, D_MODEL)),
        'w_up': nrm((DEPTH, D_MODEL, D_FF), D_MODEL ** -0.5),
        'w_down': nrm((DEPTH, D_FF, D_MODEL), D_FF ** -0.5),
        'norm_final_g': gain((D_MODEL,)),
    }


def reference(x_prompt, x_sample, state_s5_re, state_s5_im, state_ssd_conv, state_ssd, state_hgrn, state_gla,
              norm_mix_g, w_in, s5_lam_re, s5_lam_im, s5_log_dt, s5_b_re, s5_b_im, s5_c_re, s5_c_im,
              s5_d, s5_w_glu, s5_b_glu, ssd_conv_w, ssd_conv_b, ssd_dt_bias, ssd_a_log, ssd_d, ssd_norm_g,
              hg_lb_logits, hg_norm_g, gla_w_gk2, gla_b_gk, gla_norm_g, w_out, norm_mlp_g, w_up, w_down,
              norm_final_g):
    f32 = jnp.float32
    sm = jax.nn.softmax(hg_lb_logits.astype(f32), axis=0)
    lb_all = jnp.cumsum(sm, axis=0) - sm[0:1]

    sample_in = (state_s5_re, state_s5_im, state_ssd_conv, state_ssd, state_hgrn, state_gla)
    prompt_zero = tuple(jnp.zeros((BATCH,) + s.shape[2:], f32) for s in sample_in)

    hp, hs = x_prompt, x_sample
    new_p = [[] for _ in sample_in]
    new_s = [[] for _ in sample_in]
    for l in range(DEPTH):
        p = {
            'norm_mix_g': norm_mix_g[l], 'w_in': w_in[l],
            's5_lam_re': s5_lam_re[l], 's5_lam_im': s5_lam_im[l], 's5_log_dt': s5_log_dt[l],
            's5_b_re': s5_b_re[l], 's5_b_im': s5_b_im[l], 's5_c_re': s5_c_re[l], 's5_c_im': s5_c_im[l],
            's5_d': s5_d[l], 's5_w_glu': s5_w_glu[l], 's5_b_glu': s5_b_glu[l],
            'ssd_conv_w': ssd_conv_w[l], 'ssd_conv_b': ssd_conv_b[l], 'ssd_dt_bias': ssd_dt_bias[l],
            'ssd_a_log': ssd_a_log[l], 'ssd_d': ssd_d[l], 'ssd_norm_g': ssd_norm_g[l],
            'hg_lb': lb_all[l], 'hg_norm_g': hg_norm_g[l],
            'gla_w_gk2': gla_w_gk2[l], 'gla_b_gk': gla_b_gk[l], 'gla_norm_g': gla_norm_g[l],
            'w_out': w_out[l], 'norm_mlp_g': norm_mlp_g[l], 'w_up': w_up[l], 'w_down': w_down[l],
        }
        hp, st_p = layer(hp, prompt_zero, p)
        hs, st_s = layer(hs, tuple(s[l] for s in sample_in), p)
        for j in range(len(sample_in)):
            new_p[j].append(st_p[j].astype(sample_in[j].dtype))
            new_s[j].append(st_s[j].astype(sample_in[j].dtype))

    y_prompt = rmsnorm(hp, norm_final_g)
    y_sample = rmsnorm(hs, norm_final_g)
    p_s5_re, p_s5_im, p_conv, p_ssd, p_hg, p_gla = [jnp.stack(v, axis=0) for v in new_p]
    s_s5_re, s_s5_im, s_conv, s_ssd, s_hg, s_gla = [jnp.stack(v, axis=0) for v in new_s]
    return (y_prompt, y_sample, p_s5_re, p_s5_im, p_conv, p_ssd, p_hg, p_gla,
            s_s5_re, s_s5_im, s_conv, s_ssd, s_hg, s_gla)
```

```python
import functools
import math

import numpy as np
import jax
import jax.numpy as jnp
from jax import lax
from jax.experimental import pallas as pl
from jax.experimental.pallas import tpu as pltpu

F32 = jnp.float32
BF16 = jnp.bfloat16
HIGHEST = lax.Precision.HIGHEST

D_MODEL = 1024
GROUP_W = 256
S5_GROUPS, S5_CH, S5_P = 16, 16, 64
S5_STATE = S5_GROUPS * S5_P
SSD_HEADS, SSD_N, SSD_P, SSD_NGROUPS, SSD_CONV, SSD_XBC = 4, 64, 64, 2, 4, 512
HG_HEADS, HG_K, HG_V = 4, 64, 64
GLA_HEADS, GLA_K, GLA_V, GLA_RANK, GLA_TAU = 4, 32, 64, 16, 16.0
N_HEADS = 4
D_FF = 4096
EPS = 1e-6
LB_FLOOR = 1e-30

O_U, O_Z, O_XBC = 0, 256, 512
O_HQ, O_HF, O_HI, O_HGATE = 1024, 1280, 1536, 1792
O_GQ, O_GK, O_GV, O_GGATE = 2048, 2176, 2304, 2560
O_MISC = 2816
N_PROJ = 2944
LR_LO = SSD_HEADS

SUB = 16
NEG = -1e30
VMEM_LIMIT = 56 * 1024 * 1024


def _sigmoid(x):
    return 1.0 / (1.0 + jnp.exp(-x))


def _silu(x):
    return x * _sigmoid(x)


def _softplus(x):
    return jnp.maximum(x, 0.0) + jnp.log1p(jnp.exp(-jnp.abs(x)))


def _log_sigmoid(x):
    return -_softplus(-x)


def _logaddexp(a, b):
    return jnp.maximum(a, b) + jnp.log1p(jnp.exp(-jnp.abs(a - b)))


def _gelu_tanh(x):
    c = math.sqrt(2.0 / math.pi)
    return x * (0.5 * (1.0 + jnp.tanh(c * (x + 0.044715 * (x * x * x)))))


def _rms_scale(x, axis):
    return lax.rsqrt(jnp.mean(x * x, axis=axis, keepdims=True) + EPS)


def _dot(a, b):
    return jnp.dot(a, b, preferred_element_type=F32)


def _dot_nt(a, b):
    return lax.dot_general(a, b, (((1,), (1,)), ((), ())), preferred_element_type=F32)


def _dot_tn(a, b):
    return lax.dot_general(a, b, (((0,), (0,)), ((), ())), preferred_element_type=F32)


def _dot_exact(a, b):
    return jnp.dot(a, b, preferred_element_type=F32, precision=HIGHEST)


def _idiv(x, n):
    shift = int(math.log2(n))
    assert 1 << shift == n
    return x >> shift


def _store_tiles(ref, row0, val):
    rows = val.shape[0]
    for j in range(val.shape[1] // 128):
        ref[j, row0:row0 + rows, :] = val[:, j * 128:(j + 1) * 128]


def _load_tiles(ref, row0, rows, n_tiles):
    return jnp.concatenate([ref[j, row0:row0 + rows, :] for j in range(n_tiles)], axis=1)


def _load_seq(ref, start, t_steps, nb, n_tiles):
    tiles = [ref[j, pl.ds(start, t_steps, stride=nb), :] for j in range(n_tiles)]
    return tiles[0] if n_tiles == 1 else jnp.concatenate(tiles, axis=1)


def _gated_scan_tile(kd_lanes, dk, nb, t_steps, pad, qf_ref, kp_ref, cp_ref, vp_ref, ob_ref, st_ref,
                     j_units):
    rows = nb * t_steps
    sub_rows = SUB * nb
    n_sub = rows // sub_rows

    step_in_sub = _idiv(lax.broadcasted_iota(jnp.int32, (sub_rows, 128), 0), nb)

    def sub_body(sb, carry):
        r0 = pl.multiple_of(sb * sub_rows, sub_rows)
        for ql, outs in j_units:
            q = qf_ref[ql, pl.ds(r0, sub_rows), :]
            c = cp_ref[ql, pl.ds(pad + r0, sub_rows), :]
            accs = [jnp.zeros((sub_rows, 128), F32) for _ in outs]
            for d in range(SUB):
                s = pl.multiple_of(pad + r0 - nb * d, nb)
                k_d = kp_ref[ql, pl.ds(s, sub_rows), :]
                c_d = cp_ref[ql, pl.ds(s, sub_rows), :]
                arg = jnp.where(step_in_sub >= d, c - c_d, NEG)
                p = (q * k_d * jnp.exp(arg)).astype(BF16)
                for n, (vl, j_ref) in enumerate(outs):
                    v_d = vp_ref[vl, pl.ds(s, sub_rows), :]
                    accs[n] = accs[n] + _dot(p, j_ref[...]) * v_d
            for n, (vl, _) in enumerate(outs):
                ob_ref[vl, pl.ds(r0, sub_rows), :] = accs[n]
        return carry

    lax.fori_loop(0, n_sub, sub_body, 0)

    t_idx = lax.broadcasted_iota(jnp.int32, (t_steps, kd_lanes), 0)
    key_head = _idiv(lax.broadcasted_iota(jnp.int32, (t_steps, kd_lanes), 1), dk)
    val_head = _idiv(lax.broadcasted_iota(jnp.int32, (t_steps, 256), 1), 64)
    ri = lax.broadcasted_iota(jnp.int32, (t_steps, t_steps), 0)
    ci = lax.broadcasted_iota(jnp.int32, (t_steps, t_steps), 1)
    st_diag = (_idiv(lax.broadcasted_iota(jnp.int32, (256, kd_lanes), 0), 64)
               == _idiv(lax.broadcasted_iota(jnp.int32, (256, kd_lanes), 1), dk))
    levels = []
    c_sz = SUB
    while c_sz < t_steps:
        levels.append(c_sz)
        c_sz *= 2

    def seq_body(b, carry):
        qb = _load_seq(qf_ref, b, t_steps, nb, kd_lanes // 128)
        kb = _load_seq(kp_ref, pad + b, t_steps, nb, kd_lanes // 128)
        cb = _load_seq(cp_ref, pad + b, t_steps, nb, kd_lanes // 128)
        vb = _load_seq(vp_ref, pad + b, t_steps, nb, 2)
        c_last = cb[t_steps - 1:t_steps, :]
        st = st_ref[b]

        q_lv, k_lv = [], []
        for c_sz in levels:
            pieces = []
            for m in range(t_steps // (2 * c_sz)):
                a_row = m * 2 * c_sz + c_sz - 1
                pieces.append(jnp.broadcast_to(cb[a_row:a_row + 1, :], (2 * c_sz, kd_lanes)))
            anchor = pieces[0] if len(pieces) == 1 else jnp.concatenate(pieces, axis=0)
            upper = (t_idx & c_sz) != 0
            q_lv.append(qb * jnp.exp(jnp.where(upper, cb - anchor, NEG)))
            k_lv.append((kb * jnp.exp(jnp.where(upper, NEG, anchor - cb))).astype(BF16))

        o = _dot_nt((qb * jnp.exp(cb)).astype(BF16), st.astype(BF16))
        for h in range(N_HEADS):
            hm = key_head == h
            s = None
            for c_sz, q_l, k_l in zip(levels, q_lv, k_lv):
                s_l = _dot_nt(jnp.where(hm, q_l, 0.0).astype(BF16), k_l)
                if 2 * c_sz < t_steps:
                    shift = int(math.log2(2 * c_sz))
                    s_l = jnp.where((ri >> shift) == (ci >> shift), s_l, 0.0)
                s = s_l if s is None else s + s_l
            if s is not None:
                o = o + _dot(s.astype(BF16), jnp.where(val_head == h, vb, 0.0).astype(BF16))
        for j in range(2):
            sl = pl.ds(b, t_steps, stride=nb)
            ob_ref[j, sl, :] = ob_ref[j, sl, :] + o[:, j * 128:(j + 1) * 128]

        upd = _dot_tn(vb.astype(BF16), (kb * jnp.exp(c_last - cb)).astype(BF16))
        st_ref[b] = jnp.exp(c_last) * st + jnp.where(st_diag, upd, 0.0)
        return carry

    lax.fori_loop(0, nb, seq_body, 0)


def _cumsum_time(nb, t_steps, pad, src_ref, cp_ref, kd_lanes):
    def body(t, c):
        r = pl.multiple_of(t * nb, nb)
        c = c + src_ref[pl.ds(r, nb), :kd_lanes]
        for j in range(kd_lanes // 128):
            cp_ref[j, pl.ds(pad + r, nb), :] = c[:, j * 128:(j + 1) * 128]
        return c

    lax.fori_loop(0, t_steps, body, jnp.zeros((nb, kd_lanes), F32), unroll=4)


def _head_rms_scale(o, jn_ref):
    return lax.rsqrt(_dot_exact(o * o, jn_ref[...]) + EPS)


def _mixer_prompt_kernel(nb, t_steps, pad,
                         x_ref, gmix_ref, win_ref, bblk_ref, lam_ref, cblk_ref, wglu_ref,
                         v256_ref, v128_ref, v512_ref, ehd_ref, w2p_ref, ja_ref, jb0_ref, jb1_ref,
                         jn_ref,
                         mix_ref, s5_ref, conv_ref, st_ssd_ref, st_hg_ref, st_gla_ref,
                         proj_ref, bu_ref, xe_ref, qf_ref, kp_ref, cp_ref, vp_ref, ob_ref, la_ref):
    rows = nb * t_steps
    conv_rows = (SSD_CONV - 1) * nb

    @pl.when(pl.program_id(0) == 0)
    def _():
        s5_ref[...] = jnp.zeros_like(s5_ref)
        st_ssd_ref[...] = jnp.zeros_like(st_ssd_ref)
        st_hg_ref[...] = jnp.zeros_like(st_hg_ref)
        st_gla_ref[...] = jnp.zeros_like(st_gla_ref)
        xe_ref[0:conv_rows, :] = jnp.zeros((conv_rows, SSD_XBC), F32)
        for ref in (kp_ref, cp_ref, vp_ref):
            ref[:, 0:pad, :] = jnp.zeros((2, pad, 128), F32)

    x = x_ref[...]
    hn = x * _rms_scale(x, -1) * gmix_ref[...]
    proj_ref[...] = _dot(hn.astype(BF16), win_ref[...])

    u = proj_ref[:, O_U:O_U + GROUP_W]
    bu_ref[...] = _dot(u.astype(BF16), bblk_ref[...])
    lam_r = jnp.broadcast_to(lam_ref[0:1, :], (nb, S5_STATE))
    lam_i = jnp.broadcast_to(lam_ref[1:2, :], (nb, S5_STATE))

    def s5_step(t, h):
        h_r, h_i = h
        r = pl.multiple_of(t * nb, nb)
        n_r = lam_r * h_r - lam_i * h_i + bu_ref[pl.ds(r, nb), 0:S5_STATE]
        n_i = lam_r * h_i + lam_i * h_r + bu_ref[pl.ds(r, nb), S5_STATE:2 * S5_STATE]
        bu_ref[pl.ds(r, nb), 0:S5_STATE] = n_r
        bu_ref[pl.ds(r, nb), S5_STATE:2 * S5_STATE] = n_i
        return n_r, n_i

    h_r, h_i = lax.fori_loop(0, t_steps, s5_step, (s5_ref[0], s5_ref[1]), unroll=2)
    s5_ref[0] = h_r
    s5_ref[1] = h_i
    y = _dot(bu_ref[...].astype(BF16), cblk_ref[...]) + v256_ref[0:1, :] * u
    z = _gelu_tanh(y)
    o_s5 = z * _sigmoid(_dot(z.astype(BF16), wglu_ref[...]) + v256_ref[1:2, :])
    mix_ref[:, 0:GROUP_W] = o_s5.astype(mix_ref.dtype)

    xe_ref[conv_rows:conv_rows + rows, :] = proj_ref[:, O_XBC:O_XBC + SSD_XBC]
    conv = v512_ref[SSD_CONV:SSD_CONV + 1, :]
    for j in range(SSD_CONV):
        conv = conv + xe_ref[j * nb:j * nb + rows, :] * v512_ref[j:j + 1, :]
    tail = xe_ref[rows:rows + conv_rows, :]
    xe_ref[0:conv_rows, :] = tail
    conv_ref[...] = tail
    act = _silu(conv)
    xs = act[:, 0:GROUP_W]
    b_g = act[:, GROUP_W:GROUP_W + 128]
    c_g = act[:, GROUP_W + 128:GROUP_W + 256]
    misc = proj_ref[:, O_MISC:O_MISC + 128]
    dt = _softplus(misc + v128_ref[0:1, :])
    dt_hd = _dot_exact(dt, ehd_ref[...])
    la_ref[...] = dt_hd * v256_ref[7:8, :]
    _store_tiles(qf_ref, 0, jnp.concatenate(
        [c_g[:, 0:64], c_g[:, 0:64], c_g[:, 64:128], c_g[:, 64:128]], axis=1))
    _store_tiles(kp_ref, pad, jnp.concatenate(
        [b_g[:, 0:64], b_g[:, 0:64], b_g[:, 64:128], b_g[:, 64:128]], axis=1))
    _store_tiles(vp_ref, pad, xs * dt_hd)
    _cumsum_time(nb, t_steps, pad, la_ref, cp_ref, 256)
    units_256 = ((0, ((0, ja_ref),)), (1, ((1, ja_ref),)))
    _gated_scan_tile(256, 64, nb, t_steps, pad, qf_ref, kp_ref, cp_ref, vp_ref, ob_ref, st_ssd_ref,
                     units_256)
    y = _load_tiles(ob_ref, 0, rows, 2) + v256_ref[2:3, :] * xs
    y = y * _silu(proj_ref[:, O_Z:O_Z + GROUP_W])
    y = y * _rms_scale(y, -1) * v256_ref[3:4, :]
    mix_ref[:, GROUP_W:2 * GROUP_W] = y.astype(mix_ref.dtype)

    fz = proj_ref[:, O_HF:O_HF + GROUP_W]
    la_ref[...] = _logaddexp(_log_sigmoid(fz), v256_ref[4:5, :] + _log_sigmoid(-fz))
    _store_tiles(qf_ref, 0, _silu(proj_ref[:, O_HQ:O_HQ + GROUP_W]))
    _store_tiles(kp_ref, pad, v256_ref[5:6, :] * _sigmoid(-fz))
    _store_tiles(vp_ref, pad, proj_ref[:, O_HI:O_HI + GROUP_W])
    _cumsum_time(nb, t_steps, pad, la_ref, cp_ref, 256)
    _gated_scan_tile(256, 64, nb, t_steps, pad, qf_ref, kp_ref, cp_ref, vp_ref, ob_ref, st_hg_ref,
                     units_256)
    o = _load_tiles(ob_ref, 0, rows, 2)
    o = o * _head_rms_scale(o, jn_ref) * v256_ref[6:7, :]
    o = o * _silu(proj_ref[:, O_HGATE:O_HGATE + GROUP_W])
    mix_ref[:, 2 * GROUP_W:3 * GROUP_W] = o.astype(mix_ref.dtype)

    gk = _dot(misc.astype(BF16), w2p_ref[...]) + v128_ref[2:3, :]
    la_ref[:, 0:128] = _log_sigmoid(gk) * (1.0 / GLA_TAU)
    _store_tiles(qf_ref, 0, proj_ref[:, O_GQ:O_GQ + 128] * (GLA_K ** -0.5))
    _store_tiles(kp_ref, pad, proj_ref[:, O_GK:O_GK + 128])
    _store_tiles(vp_ref, pad, proj_ref[:, O_GV:O_GV + GROUP_W])
    _cumsum_time(nb, t_steps, pad, la_ref, cp_ref, 128)
    units_128 = ((0, ((0, jb0_ref), (1, jb1_ref))),)
    _gated_scan_tile(128, 32, nb, t_steps, pad, qf_ref, kp_ref, cp_ref, vp_ref, ob_ref, st_gla_ref,
                     units_128)
    o = _load_tiles(ob_ref, 0, rows, 2)
    o = o * _head_rms_scale(o, jn_ref) * v256_ref[8:9, :]
    o = o * _silu(proj_ref[:, O_GGATE:O_GGATE + GROUP_W])
    mix_ref[:, 3 * GROUP_W:4 * GROUP_W] = o.astype(mix_ref.dtype)


def _const_spec(shape):
    nd = len(shape)
    return pl.BlockSpec(shape, lambda i, _nd=nd: (0,) * _nd, pipeline_mode=pl.Buffered(1))


def _mixer_prompt(x_tm, nb, t_steps, p):
    rows_total = x_tm.shape[0]
    rows = nb * t_steps
    n_tiles = rows_total // rows
    pad = SUB * nb
    consts = [p['gmix'], p['win'], p['bblk'], p['lam'], p['cblk'], p['wglu'], p['v256'], p['v128'],
              p['v512'], p['ehd'], p['w2p'], p['ja'], p['jb0'], p['jb1'], p['jn']]
    in_specs = [pl.BlockSpec((rows, D_MODEL), lambda i: (i, 0))] + [_const_spec(c.shape) for c in consts]
    conv_rows = (SSD_CONV - 1) * nb
    out_shape = (
        jax.ShapeDtypeStruct((rows_total, D_MODEL), BF16),
        jax.ShapeDtypeStruct((2, nb, S5_STATE), F32),
        jax.ShapeDtypeStruct((conv_rows, SSD_XBC), F32),
        jax.ShapeDtypeStruct((nb, 256, 256), F32),
        jax.ShapeDtypeStruct((nb, 256, 256), F32),
        jax.ShapeDtypeStruct((nb, 256, 128), F32),
    )
    out_specs = (
        pl.BlockSpec((rows, D_MODEL), lambda i: (i, 0)),
        pl.BlockSpec((2, nb, S5_STATE), lambda i: (0, 0, 0)),
        pl.BlockSpec((conv_rows, SSD_XBC), lambda i: (0, 0)),
        pl.BlockSpec((nb, 256, 256), lambda i: (0, 0, 0)),
        pl.BlockSpec((nb, 256, 256), lambda i: (0, 0, 0)),
        pl.BlockSpec((nb, 256, 128), lambda i: (0, 0, 0)),
    )
    scratch = [
        pltpu.VMEM((rows, N_PROJ), F32),
        pltpu.VMEM((rows, 2 * S5_STATE), F32),
        pltpu.VMEM((conv_rows + rows, SSD_XBC), F32),
        pltpu.VMEM((2, rows, 128), F32),
        pltpu.VMEM((2, pad + rows, 128), F32),
        pltpu.VMEM((2, pad + rows, 128), F32),
        pltpu.VMEM((2, pad + rows, 128), F32),
        pltpu.VMEM((2, rows, 128), F32),
        pltpu.VMEM((rows, 256), F32),
    ]
    return pl.pallas_call(
        functools.partial(_mixer_prompt_kernel, nb, t_steps, pad),
        grid=(n_tiles,),
        in_specs=in_specs,
        out_specs=out_specs,
        out_shape=out_shape,
        scratch_shapes=scratch,
        compiler_params=pltpu.CompilerParams(dimension_semantics=("arbitrary",),
                                             vmem_limit_bytes=VMEM_LIMIT),
        name="mixer_prompt",
    )(x_tm, *consts)


C_GMIX = 0
C_LAMR = C_GMIX + D_MODEL
C_LAMI = C_LAMR + S5_STATE
C_S5D = C_LAMI + S5_STATE
C_BGLU = C_S5D + GROUP_W
C_CONVW = C_BGLU + GROUP_W
C_CONVB = C_CONVW + SSD_CONV * SSD_XBC
C_DTB = C_CONVB + SSD_XBC
C_A = C_DTB + 8
C_DSKIP = C_A + 8
C_SSDG = C_DSKIP + GROUP_W
C_LOGLB = C_SSDG + GROUP_W
C_OML = C_LOGLB + GROUP_W
C_HGG = C_OML + GROUP_W
C_BGK = C_HGG + GROUP_W
C_GLAG = C_BGK + 128
C_ROWS = C_GLAG + GROUP_W


def _mixer_sample_kernel(xT_ref, col_ref, winT_ref, bblkT_ref, cblkT_ref, wgluT_ref, w2pT_ref,
                         s5_in, conv_in, ssd_in, hg_in, gla_in,
                         mixT_ref, s5_out, conv_out, ssd_out, hg_out, gla_out,
                         proj_ref, act_ref, dt_ref, es_ref, qh_ref, kh_ref, eh_ref,
                         qg_ref, kg_ref, eg_ref, ys_ref):
    h = pl.program_id(0)

    def col(off, n):
        return col_ref[off:off + n, :]

    def col_h(off):
        return col_ref[pl.ds(pl.multiple_of(off + h * 64, 64), 64), :]

    @pl.when(h == 0)
    def _():
        x = xT_ref[...]
        hn = x * _rms_scale(x, 0) * col(C_GMIX, D_MODEL)
        proj_ref[...] = _dot(winT_ref[...], hn.astype(BF16))

        u = proj_ref[O_U:O_U + GROUP_W, :]
        bu = _dot(bblkT_ref[...], u.astype(BF16))
        lam_r, lam_i = col(C_LAMR, S5_STATE), col(C_LAMI, S5_STATE)
        h0_r, h0_i = s5_in[0], s5_in[1]
        h_r = lam_r * h0_r - lam_i * h0_i + bu[0:S5_STATE]
        h_i = lam_r * h0_i + lam_i * h0_r + bu[S5_STATE:2 * S5_STATE]
        s5_out[0] = h_r
        s5_out[1] = h_i
        y = (_dot(cblkT_ref[...], jnp.concatenate([h_r, h_i], axis=0).astype(BF16))
             + col(C_S5D, GROUP_W) * u)
        z = _gelu_tanh(y)
        mixT_ref[0:GROUP_W, :] = z * _sigmoid(_dot(wgluT_ref[...], z.astype(BF16)) + col(C_BGLU, GROUP_W))

        xbc = proj_ref[O_XBC:O_XBC + SSD_XBC, :]
        conv = col(C_CONVB, SSD_XBC) + col(C_CONVW + 3 * SSD_XBC, SSD_XBC) * xbc
        for j in range(SSD_CONV - 1):
            conv = conv + col(C_CONVW + j * SSD_XBC, SSD_XBC) * conv_in[j * SSD_XBC:(j + 1) * SSD_XBC, :]
        conv_out[0:2 * SSD_XBC, :] = conv_in[SSD_XBC:3 * SSD_XBC, :]
        conv_out[2 * SSD_XBC:3 * SSD_XBC, :] = xbc
        act_ref[...] = _silu(conv)
        misc = proj_ref[O_MISC:O_MISC + 128, :]
        dt = _softplus(misc[0:8] + col(C_DTB, 8))
        dt_ref[...] = dt
        es_ref[...] = jnp.exp(dt * col(C_A, 8))

        fz = proj_ref[O_HF:O_HF + GROUP_W, :]
        eh_ref[...] = jnp.exp(_logaddexp(_log_sigmoid(fz), col(C_LOGLB, GROUP_W) + _log_sigmoid(-fz)))
        kh_ref[...] = col(C_OML, GROUP_W) * _sigmoid(-fz)
        qh_ref[...] = _silu(proj_ref[O_HQ:O_HQ + GROUP_W, :])

        gk = _dot(w2pT_ref[...], misc.astype(BF16)) + col(C_BGK, 128)
        eg_ref[...] = jnp.exp(_log_sigmoid(gk) * (1.0 / GLA_TAU))
        kg_ref[...] = proj_ref[O_GK:O_GK + 128, :]
        qg_ref[...] = proj_ref[O_GQ:O_GQ + 128, :] * (GLA_K ** -0.5)

    def state_step(st_in, st_out, e_row, k_row, q_row, v, n_keys):
        def body(i, acc):
            r = pl.multiple_of(i * 64, 64)
            s_new = e_row(i) * st_in[pl.ds(r, 64), :] + k_row(i) * v
            st_out[pl.ds(r, 64), :] = s_new
            return acc + q_row(i) * s_new
        return lax.fori_loop(0, n_keys, body, jnp.zeros_like(v), unroll=4)

    def rows_h(ref, base):
        return ref[pl.ds(pl.multiple_of(base + h * 64, 64), 64), :]

    g = _idiv(h, SSD_HEADS // SSD_NGROUPS)
    xh = rows_h(act_ref, 0)
    e_h = es_ref[pl.ds(h, 1), :]
    o = state_step(ssd_in, ssd_out,
                   lambda i: e_h,
                   lambda i: act_ref[pl.ds(GROUP_W + g * SSD_N + i, 1), :],
                   lambda i: act_ref[pl.ds(GROUP_W + 128 + g * SSD_N + i, 1), :],
                   xh * dt_ref[pl.ds(h, 1), :], SSD_N)
    ys_ref[pl.ds(pl.multiple_of(h * 64, 64), 64), :] = o + col_h(C_DSKIP) * xh

    o = state_step(hg_in, hg_out,
                   lambda i: eh_ref[pl.ds(h * HG_K + i, 1), :],
                   lambda i: kh_ref[pl.ds(h * HG_K + i, 1), :],
                   lambda i: qh_ref[pl.ds(h * HG_K + i, 1), :],
                   rows_h(proj_ref, O_HI), HG_K)
    o = o * _rms_scale(o, 0) * col_h(C_HGG) * _silu(rows_h(proj_ref, O_HGATE))
    mixT_ref[pl.ds(pl.multiple_of(2 * GROUP_W + h * 64, 64), 64), :] = o

    o = state_step(gla_in, gla_out,
                   lambda i: eg_ref[pl.ds(h * GLA_K + i, 1), :],
                   lambda i: kg_ref[pl.ds(h * GLA_K + i, 1), :],
                   lambda i: qg_ref[pl.ds(h * GLA_K + i, 1), :],
                   rows_h(proj_ref, O_GV), GLA_K)
    o = o * _rms_scale(o, 0) * col_h(C_GLAG) * _silu(rows_h(proj_ref, O_GGATE))
    mixT_ref[pl.ds(pl.multiple_of(3 * GROUP_W + h * 64, 64), 64), :] = o

    @pl.when(h == N_HEADS - 1)
    def _():
        y = ys_ref[...] * _silu(proj_ref[O_Z:O_Z + GROUP_W, :])
        mixT_ref[GROUP_W:2 * GROUP_W, :] = y * _rms_scale(y, 0) * col(C_SSDG, GROUP_W)


def _mixer_sample(xT, p, s5T, convT, ssdT, hgT, glaT):
    nbatch = xT.shape[1]
    consts = [xT, p['col'], p['winT'], p['bblkT'], p['cblkT'], p['wgluT'], p['w2pT'], s5T, convT]
    head_blk = lambda a: pl.BlockSpec((a.shape[0] // N_HEADS, nbatch), lambda i: (i, 0))
    res_blk = lambda shape: pl.BlockSpec(shape, lambda i, _nd=len(shape): (0,) * _nd)
    out_shape = (
        jax.ShapeDtypeStruct((D_MODEL, nbatch), F32),
        jax.ShapeDtypeStruct(s5T.shape, F32),
        jax.ShapeDtypeStruct(convT.shape, F32),
        jax.ShapeDtypeStruct(ssdT.shape, F32),
        jax.ShapeDtypeStruct(hgT.shape, F32),
        jax.ShapeDtypeStruct(glaT.shape, F32),
    )
    scratch = [
        pltpu.VMEM((N_PROJ, nbatch), F32),
        pltpu.VMEM((SSD_XBC, nbatch), F32),
        pltpu.VMEM((8, nbatch), F32),
        pltpu.VMEM((8, nbatch), F32),
        pltpu.VMEM((GROUP_W, nbatch), F32),
        pltpu.VMEM((GROUP_W, nbatch), F32),
        pltpu.VMEM((GROUP_W, nbatch), F32),
        pltpu.VMEM((128, nbatch), F32),
        pltpu.VMEM((128, nbatch), F32),
        pltpu.VMEM((128, nbatch), F32),
        pltpu.VMEM((GROUP_W, nbatch), F32),
    ]
    return pl.pallas_call(
        _mixer_sample_kernel,
        grid=(N_HEADS,),
        in_specs=[_const_spec(c.shape) for c in consts] + [head_blk(ssdT), head_blk(hgT), head_blk(glaT)],
        out_specs=(res_blk((D_MODEL, nbatch)), res_blk(s5T.shape), res_blk(convT.shape),
                   head_blk(ssdT), head_blk(hgT), head_blk(glaT)),
        out_shape=out_shape,
        scratch_shapes=scratch,
        compiler_params=pltpu.CompilerParams(dimension_semantics=("arbitrary",),
                                             vmem_limit_bytes=VMEM_LIMIT),
        name="mixer_sample",
    )(*consts, ssdT, hgT, glaT)


FF_TILE = 512


def _mlp_kernel(final, mix_ref, x_ref, wout_ref, g_ref, wup_ref, wdn_ref, gfin_ref, o_ref):
    x1 = x_ref[...] + _dot(mix_ref[...], wout_ref[...])
    hn = (x1 * _rms_scale(x1, -1) * g_ref[...]).astype(BF16)
    acc = x1
    for f in range(D_FF // FF_TILE):
        up = jnp.maximum(_dot(hn, wup_ref[:, f * FF_TILE:(f + 1) * FF_TILE]), 0.0)
        acc = acc + _dot((up * up).astype(BF16), wdn_ref[f * FF_TILE:(f + 1) * FF_TILE, :])
    if final:
        acc = acc * _rms_scale(acc, -1) * gfin_ref[...]
    o_ref[...] = acc


def _mlp(mix, x, wout, g, wup, wdn, gfin, final, tm):
    rows_total = x.shape[0]
    tm = min(tm, rows_total)
    consts = [wout, g, wup, wdn, gfin]
    return pl.pallas_call(
        functools.partial(_mlp_kernel, final),
        grid=(rows_total // tm,),
        in_specs=[pl.BlockSpec((tm, D_MODEL), lambda i: (i, 0)),
                  pl.BlockSpec((tm, D_MODEL), lambda i: (i, 0))] + [_const_spec(c.shape) for c in consts],
        out_specs=pl.BlockSpec((tm, D_MODEL), lambda i: (i, 0)),
        out_shape=jax.ShapeDtypeStruct((rows_total, D_MODEL), F32),
        compiler_params=pltpu.CompilerParams(dimension_semantics=("parallel",),
                                             vmem_limit_bytes=VMEM_LIMIT),
        name="out_mlp",
    )(mix, x, *consts)


def _block_ones(n_rows, row_blk, n_cols, col_blk, scale=1.0):
    r = np.arange(n_rows)[:, None] // row_blk
    c = np.arange(n_cols)[None, :] // col_blk
    return (r == c).astype(np.float32) * scale


def _prepare(norm_mix_g, w_in, s5_lam_re, s5_lam_im, s5_log_dt, s5_b_re, s5_b_im, s5_c_re, s5_c_im,
             s5_d, s5_w_glu, s5_b_glu, ssd_conv_w, ssd_conv_b, ssd_dt_bias, ssd_a_log, ssd_d, ssd_norm_g,
             hg_lb_logits, hg_norm_g, gla_w_gk2, gla_b_gk, gla_norm_g, nbatch_sample):
    depth = w_in.shape[0]
    w = w_in.astype(F32)
    misc = jnp.concatenate([w[:, :, 1024:1028], w[:, :, 2820:2836],
                            jnp.zeros((depth, D_MODEL, 128 - SSD_HEADS - GLA_RANK), F32)], axis=-1)
    win = jnp.concatenate([w[:, :, :1024], w[:, :, 1028:2820], misc], axis=-1).astype(BF16)

    lam_re, lam_im = s5_lam_re.astype(F32), s5_lam_im.astype(F32)
    dt = jnp.exp(s5_log_dt.astype(F32))[..., None]
    ea = jnp.exp(lam_re * dt)
    lb_r, lb_i = ea * jnp.cos(lam_im * dt), ea * jnp.sin(lam_im * dt)
    den = lam_re * lam_re + lam_im * lam_im
    cr = ((lb_r - 1.0) * lam_re + lb_i * lam_im) / den
    ci = (lb_i * lam_re - (lb_r - 1.0) * lam_im) / den
    bb_r = cr[..., None] * s5_b_re - ci[..., None] * s5_b_im
    bb_i = cr[..., None] * s5_b_im + ci[..., None] * s5_b_re
    eye_g = jnp.eye(S5_GROUPS, dtype=F32)
    blk_b = lambda t: jnp.einsum('lgpc,gh->lgchp', t, eye_g).reshape(depth, GROUP_W, S5_STATE)
    blk_c = lambda t: jnp.einsum('lgcp,gh->lgphc', t, eye_g).reshape(depth, S5_STATE, GROUP_W)
    bblk = jnp.concatenate([blk_b(bb_r), blk_b(bb_i)], axis=-1)
    cblk = jnp.concatenate([blk_c(s5_c_re.astype(F32)), -blk_c(s5_c_im.astype(F32))], axis=1)
    lam = jnp.stack([lb_r.reshape(depth, S5_STATE), lb_i.reshape(depth, S5_STATE)], axis=1)

    sm = jax.nn.softmax(hg_lb_logits.astype(F32), axis=0)
    lb = jnp.cumsum(sm, axis=0) - sm[0:1]
    log_lb = jnp.log(jnp.maximum(lb, LB_FLOOR))
    a = -jnp.exp(ssd_a_log.astype(F32))
    rep64 = lambda t: jnp.repeat(t, 64, axis=-1)
    zeros256 = jnp.zeros((depth, GROUP_W), F32)
    v256 = jnp.stack([s5_d, s5_b_glu, rep64(ssd_d), ssd_norm_g, log_lb, 1.0 - lb, hg_norm_g, rep64(a),
                      jnp.tile(gla_norm_g, (1, GLA_HEADS))] + [zeros256] * 7, axis=1).astype(F32)
    pad128 = lambda t: jnp.pad(t, ((0, 0), (0, 128 - t.shape[-1])))
    zeros128 = jnp.zeros((depth, 128), F32)
    v128 = jnp.stack([pad128(ssd_dt_bias), pad128(a), gla_b_gk] + [zeros128] * 5, axis=1).astype(F32)
    v512 = jnp.concatenate([ssd_conv_w, ssd_conv_b[:, None, :], jnp.zeros((depth, 3, SSD_XBC), F32)],
                           axis=1).astype(F32)
    w2p = jnp.pad(gla_w_gk2, ((0, 0), (LR_LO, 128 - LR_LO - GLA_RANK), (0, 0))).astype(BF16)

    shared = dict(
        ehd=jnp.asarray(_block_ones(128, 1, 256, 64) * (np.arange(128)[:, None] < SSD_HEADS)),
        ja=jnp.asarray(_block_ones(128, 64, 128, 64), BF16),
        jb0=jnp.asarray(_block_ones(128, 32, 256, 64)[:, :128], BF16),
        jb1=jnp.asarray(_block_ones(128, 32, 256, 64)[:, 128:], BF16),
        jn=jnp.asarray(_block_ones(256, 64, 256, 64, 1.0 / 64)),
    )

    bcast = lambda t: jnp.broadcast_to(t[..., None], t.shape + (nbatch_sample,))
    col = jnp.concatenate([
        norm_mix_g, lam[:, 0], lam[:, 1], s5_d, s5_b_glu, ssd_conv_w.reshape(depth, -1), ssd_conv_b,
        jnp.pad(ssd_dt_bias, ((0, 0), (0, 8 - SSD_HEADS))), jnp.pad(a, ((0, 0), (0, 8 - SSD_HEADS))),
        rep64(ssd_d), ssd_norm_g, log_lb, 1.0 - lb, hg_norm_g, gla_b_gk, jnp.tile(gla_norm_g, (1, GLA_HEADS)),
    ], axis=-1).astype(F32)
    layers = []
    for l in range(depth):
        layers.append(dict(
            gmix=norm_mix_g[l][None, :].astype(F32), win=win[l], bblk=bblk[l].astype(BF16), lam=lam[l],
            cblk=cblk[l].astype(BF16), wglu=s5_w_glu[l].astype(BF16), v256=v256[l], v128=v128[l], v512=v512[l],
            w2p=w2p[l], col=bcast(col[l]),
            winT=win[l].T, bblkT=bblk[l].T.astype(BF16), cblkT=cblk[l].T.astype(BF16),
            wgluT=s5_w_glu[l].T.astype(BF16), w2pT=w2p[l].T, **shared))
    return layers


def _diag_state(st, dk):
    nb = st.shape[0]
    s = st.reshape(nb, N_HEADS, 64, N_HEADS, dk)
    idx = jnp.arange(N_HEADS)
    s = s[:, idx, :, idx, :]
    return s.transpose(1, 0, 3, 2)


def kernel(x_prompt, x_sample, state_s5_re, state_s5_im, state_ssd_conv, state_ssd, state_hgrn, state_gla,
           norm_mix_g, w_in, s5_lam_re, s5_lam_im, s5_log_dt, s5_b_re, s5_b_im, s5_c_re, s5_c_im,
           s5_d, s5_w_glu, s5_b_glu, ssd_conv_w, ssd_conv_b, ssd_dt_bias, ssd_a_log, ssd_d, ssd_norm_g,
           hg_lb_logits, hg_norm_g, gla_w_gk2, gla_b_gk, gla_norm_g, w_out, norm_mlp_g, w_up, w_down,
           norm_final_g):
    nb, seq, _ = x_prompt.shape
    ns = x_sample.shape[0]
    depth = w_in.shape[0]
    t_steps = min(64, seq)

    layers = _prepare(norm_mix_g, w_in, s5_lam_re, s5_lam_im, s5_log_dt, s5_b_re, s5_b_im, s5_c_re, s5_c_im,
                      s5_d, s5_w_glu, s5_b_glu, ssd_conv_w, ssd_conv_b, ssd_dt_bias, ssd_a_log, ssd_d,
                      ssd_norm_g, hg_lb_logits, hg_norm_g, gla_w_gk2, gla_b_gk, gla_norm_g, ns)
    wout16, wup16, wdn16 = w_out.astype(BF16), w_up.astype(BF16), w_down.astype(BF16)
    gfin = norm_final_g[None, :].astype(F32)

    xp = x_prompt.astype(F32).transpose(1, 0, 2).reshape(seq * nb, D_MODEL)
    xs = x_sample.astype(F32).reshape(ns, D_MODEL)

    outs_p = [[] for _ in range(6)]
    outs_s = [[] for _ in range(6)]
    for l in range(depth):
        p = layers[l]
        final = l == depth - 1
        gmlp = norm_mlp_g[l][None, :].astype(F32)

        mix, s5, conv, st_ssd, st_hg, st_gla = _mixer_prompt(xp, nb, t_steps, p)
        xp = _mlp(mix, xp, wout16[l], gmlp, wup16[l], wdn16[l], gfin, final, 512)
        outs_p[0].append(s5[0].reshape(nb, S5_GROUPS, S5_P))
        outs_p[1].append(s5[1].reshape(nb, S5_GROUPS, S5_P))
        outs_p[2].append(conv.reshape(SSD_CONV - 1, nb, SSD_XBC).transpose(1, 0, 2))
        outs_p[3].append(_diag_state(st_ssd, SSD_N))
        outs_p[4].append(_diag_state(st_hg, HG_K))
        outs_p[5].append(_diag_state(st_gla, GLA_K))

        s5T = jnp.stack([state_s5_re[l].reshape(ns, S5_STATE).T, state_s5_im[l].reshape(ns, S5_STATE).T])
        convT = state_ssd_conv[l].transpose(1, 2, 0).reshape((SSD_CONV - 1) * SSD_XBC, ns)
        ssdT = state_ssd[l].transpose(1, 2, 3, 0).reshape(SSD_HEADS * SSD_N * SSD_P, ns)
        hgT = state_hgrn[l].transpose(1, 2, 3, 0).reshape(HG_HEADS * HG_K * HG_V, ns)
        glaT = state_gla[l].transpose(1, 2, 3, 0).reshape(GLA_HEADS * GLA_K * GLA_V, ns)
        mixT, s5n, convn, ssdn, hgn, glan = _mixer_sample(xs.T, p, s5T.astype(F32), convT.astype(F32),
                                                          ssdT.astype(F32), hgT.astype(F32), glaT.astype(F32))
        xs = _mlp(mixT.T.astype(BF16), xs, wout16[l], gmlp, wup16[l], wdn16[l], gfin, final, 512)
        outs_s[0].append(s5n[0].T.reshape(ns, S5_GROUPS, S5_P))
        outs_s[1].append(s5n[1].T.reshape(ns, S5_GROUPS, S5_P))
        outs_s[2].append(convn.reshape(SSD_CONV - 1, SSD_XBC, ns).transpose(2, 0, 1))
        outs_s[3].append(ssdn.reshape(SSD_HEADS, SSD_N, SSD_P, ns).transpose(3, 0, 1, 2))
        outs_s[4].append(hgn.reshape(HG_HEADS, HG_K, HG_V, ns).transpose(3, 0, 1, 2))
        outs_s[5].append(glan.reshape(GLA_HEADS, GLA_K, GLA_V, ns).transpose(3, 0, 1, 2))

    y_prompt = xp.reshape(seq, nb, D_MODEL).transpose(1, 0, 2)
    y_sample = xs.reshape(ns, 1, D_MODEL)
    return (y_prompt, y_sample) + tuple(jnp.stack(v, axis=0) for v in outs_p) \
        + tuple(jnp.stack(v, axis=0) for v in outs_s)
```

```python
import functools
import math

import numpy as np
import jax
import jax.numpy as jnp
from jax import lax
from jax.experimental import pallas as pl
from jax.experimental.pallas import tpu as pltpu

F32 = jnp.float32
BF16 = jnp.bfloat16
HIGHEST = lax.Precision.HIGHEST

D_MODEL = 1024
GROUP_W = 256
S5_GROUPS, S5_CH, S5_P = 16, 16, 64
S5_STATE = S5_GROUPS * S5_P
SSD_HEADS, SSD_N, SSD_P, SSD_NGROUPS, SSD_CONV, SSD_XBC = 4, 64, 64, 2, 4, 512
HG_HEADS, HG_K, HG_V = 4, 64, 64
GLA_HEADS, GLA_K, GLA_V, GLA_RANK, GLA_TAU = 4, 32, 64, 16, 16.0
N_HEADS = 4
D_FF = 4096
EPS = 1e-6
LB_FLOOR = 1e-30

O_U, O_Z, O_XBC = 0, 256, 512
O_HQ, O_HF, O_HI, O_HGATE = 1024, 1280, 1536, 1792
O_GQ, O_GK, O_GV, O_GGATE = 2048, 2176, 2304, 2560
O_MISC = 2816
N_PROJ = 2944
LR_LO = SSD_HEADS

SUB = 16
NEG = -1e30
LOG2E = 1.4426950408889634
VMEM_LIMIT = 56 * 1024 * 1024


def _sigmoid(x):
    return 1.0 / (1.0 + jnp.exp(-x))


def _silu(x):
    return x * _sigmoid(x)


def _softplus(x):
    return jnp.maximum(x, 0.0) + jnp.log1p(jnp.exp(-jnp.abs(x)))


def _log_sigmoid(x):
    return -_softplus(-x)


def _logaddexp(a, b):
    return jnp.maximum(a, b) + jnp.log1p(jnp.exp(-jnp.abs(a - b)))


def _gelu_tanh(x):
    c = math.sqrt(2.0 / math.pi)
    return x * (0.5 * (1.0 + jnp.tanh(c * (x + 0.044715 * (x * x * x)))))


def _rms_scale(x, axis):
    return lax.rsqrt(jnp.mean(x * x, axis=axis, keepdims=True) + EPS)


def _dot(a, b):
    return jnp.dot(a, b, preferred_element_type=F32)


def _dot_nt(a, b):
    return lax.dot_general(a, b, (((1,), (1,)), ((), ())), preferred_element_type=F32)


def _dot_tn(a, b):
    return lax.dot_general(a, b, (((0,), (0,)), ((), ())), preferred_element_type=F32)


def _dot_split(a, b16, terms):
    acc, rest = None, a
    for _ in range(terms):
        piece = rest.astype(BF16)
        part = _dot(piece, b16)
        acc = part if acc is None else acc + part
        rest = rest - piece.astype(F32)
    return acc


def _idiv(x, n):
    shift = int(math.log2(n))
    assert 1 << shift == n
    return x >> shift


def _store_tiles(ref, row0, val):
    rows = val.shape[0]
    for j in range(val.shape[1] // 128):
        ref[j, row0:row0 + rows, :] = val[:, j * 128:(j + 1) * 128]


def _load_tiles(ref, row0, rows, n_tiles):
    return jnp.concatenate([ref[j, row0:row0 + rows, :] for j in range(n_tiles)], axis=1)


def _load_seq(ref, start, t_steps, nb, n_tiles):
    tiles = [ref[j, pl.ds(start, t_steps, stride=nb), :] for j in range(n_tiles)]
    return tiles[0] if n_tiles == 1 else jnp.concatenate(tiles, axis=1)


def _load_rows(ref, start, rows, n_tiles):
    tiles = [ref[j, pl.ds(start, rows), :] for j in range(n_tiles)]
    return tiles[0] if n_tiles == 1 else jnp.concatenate(tiles, axis=1)


def _gated_scan_tile(kd_lanes, dk, nb, t_steps, qf_ref, kp_ref, cp_ref, vp_ref, ob_ref, st_ref, j_ref):
    rows = nb * t_steps
    sub_rows = SUB * nb
    n_sub = rows // sub_rows
    n_kt = kd_lanes // 128

    def sub_body(sb, carry):
        r0 = pl.multiple_of(sb * sub_rows, sub_rows)
        accs = [[None] * SUB for _ in range(2)]
        for j in range(SUB):
            reps = SUB - j
            rj = pl.multiple_of(r0 + nb * j, nb)
            q = _load_rows(qf_ref, rj, reps * nb, n_kt)
            c = _load_rows(cp_ref, rj, reps * nb, n_kt)
            k_j = jnp.tile(_load_rows(kp_ref, rj, nb, n_kt), (reps, 1))
            c_j = jnp.tile(_load_rows(cp_ref, rj, nb, n_kt), (reps, 1))
            w = _dot((q * k_j * jnp.exp2(c - c_j)).astype(BF16), j_ref[...])
            for n in range(2):
                v_j = vp_ref[n, pl.ds(rj, nb), :]
                for i in range(j, SUB):
                    term = w[(i - j) * nb:(i - j + 1) * nb, n * 128:(n + 1) * 128] * v_j
                    accs[n][i] = term if accs[n][i] is None else accs[n][i] + term
        for n in range(2):
            ob_ref[n, pl.ds(r0, sub_rows), :] = jnp.concatenate(accs[n], axis=0)
        return carry

    lax.fori_loop(0, n_sub, sub_body, 0, unroll=2)

    t_idx = lax.broadcasted_iota(jnp.int32, (t_steps, kd_lanes), 0)
    own_keys = (_idiv(lax.broadcasted_iota(jnp.int32, (N_HEADS * t_steps, kd_lanes), 0), t_steps)
                == _idiv(lax.broadcasted_iota(jnp.int32, (N_HEADS * t_steps, kd_lanes), 1), dk))
    val_head = _idiv(lax.broadcasted_iota(jnp.int32, (t_steps, 256), 1), 64)
    ri = lax.broadcasted_iota(jnp.int32, (N_HEADS * t_steps, t_steps), 0) & (t_steps - 1)
    ci = lax.broadcasted_iota(jnp.int32, (N_HEADS * t_steps, t_steps), 1)
    st_diag = (_idiv(lax.broadcasted_iota(jnp.int32, (256, kd_lanes), 0), 64)
               == _idiv(lax.broadcasted_iota(jnp.int32, (256, kd_lanes), 1), dk))
    levels = []
    c_sz = SUB
    while c_sz < t_steps:
        levels.append(c_sz)
        c_sz *= 2

    def seq_body(b, carry):
        qb = _load_seq(qf_ref, b, t_steps, nb, n_kt)
        kb = _load_seq(kp_ref, b, t_steps, nb, n_kt)
        cb = _load_seq(cp_ref, b, t_steps, nb, n_kt)
        vb = _load_seq(vp_ref, b, t_steps, nb, 2)
        c_last = cb[t_steps - 1:t_steps, :]
        st = st_ref[b]
        vb16 = vb.astype(BF16)

        o = _dot_nt((qb * jnp.exp2(cb)).astype(BF16), st.astype(BF16))
        s4 = None
        for c_sz in levels:
            pieces = []
            for m in range(t_steps // (2 * c_sz)):
                a_row = m * 2 * c_sz + c_sz - 1
                pieces.append(jnp.broadcast_to(cb[a_row:a_row + 1, :], (2 * c_sz, kd_lanes)))
            anchor = pieces[0] if len(pieces) == 1 else jnp.concatenate(pieces, axis=0)
            upper = (t_idx & c_sz) != 0
            q_l = qb * jnp.exp2(jnp.where(upper, cb - anchor, NEG))
            k_l = (kb * jnp.exp2(jnp.where(upper, NEG, anchor - cb))).astype(BF16)
            q4 = jnp.where(own_keys, jnp.tile(q_l, (N_HEADS, 1)), 0.0).astype(BF16)
            s_l = _dot_nt(q4, k_l)
            if 2 * c_sz < t_steps:
                shift = int(math.log2(2 * c_sz))
                s_l = jnp.where((ri >> shift) == (ci >> shift), s_l, 0.0)
            s4 = s_l if s4 is None else s4 + s_l
        if s4 is not None:
            r4 = _dot(s4.astype(BF16), vb16)
            for h in range(N_HEADS):
                o = o + jnp.where(val_head == h, r4[h * t_steps:(h + 1) * t_steps, :], 0.0)
        for j in range(2):
            sl = pl.ds(b, t_steps, stride=nb)
            ob_ref[j, sl, :] = ob_ref[j, sl, :] + o[:, j * 128:(j + 1) * 128]

        upd = _dot_tn(vb16, (kb * jnp.exp2(c_last - cb)).astype(BF16))
        st_ref[b] = jnp.exp2(c_last) * st + jnp.where(st_diag, upd, 0.0)
        return carry

    lax.fori_loop(0, nb, seq_body, 0, unroll=4)


def _cumsum_time(nb, t_steps, src_ref, cp_ref, kd_lanes):
    def body(t, c):
        r = pl.multiple_of(t * nb, nb)
        c = c + src_ref[pl.ds(r, nb), :kd_lanes] * LOG2E
        for j in range(kd_lanes // 128):
            cp_ref[j, pl.ds(r, nb), :] = c[:, j * 128:(j + 1) * 128]
        return c

    lax.fori_loop(0, t_steps, body, jnp.zeros((nb, kd_lanes), F32), unroll=4)


def _head_rms_scale(o, jn_ref):
    return lax.rsqrt(_dot_split(o * o, jn_ref[...], 2) + EPS)


def _mixer_prompt_kernel(nb, t_steps,
                         x_ref, gmix_ref, win_ref, bblk_ref, lam_ref, cblk_ref, wglu_ref,
                         v256_ref, v128_ref, v512_ref, ehd_ref, w2p_ref, j256_ref, j128_ref, jn_ref,
                         mix_ref, s5_ref, conv_ref, st_ssd_ref, st_hg_ref, st_gla_ref,
                         proj_ref, bu_ref, xe_ref, qf_ref, kp_ref, cp_ref, vp_ref, ob_ref, la_ref):
    rows = nb * t_steps
    conv_rows = (SSD_CONV - 1) * nb

    @pl.when(pl.program_id(0) == 0)
    def _():
        s5_ref[...] = jnp.zeros_like(s5_ref)
        st_ssd_ref[...] = jnp.zeros_like(st_ssd_ref)
        st_hg_ref[...] = jnp.zeros_like(st_hg_ref)
        st_gla_ref[...] = jnp.zeros_like(st_gla_ref)
        xe_ref[0:conv_rows, :] = jnp.zeros((conv_rows, SSD_XBC), F32)

    x = x_ref[...]
    hn = x * _rms_scale(x, -1) * gmix_ref[...]
    proj_ref[...] = _dot(hn.astype(BF16), win_ref[...])

    u = proj_ref[:, O_U:O_U + GROUP_W]
    bu_ref[...] = _dot(u.astype(BF16), bblk_ref[...])
    lam_r = jnp.broadcast_to(lam_ref[0:1, :], (nb, S5_STATE))
    lam_i = jnp.broadcast_to(lam_ref[1:2, :], (nb, S5_STATE))

    def s5_step(t, h):
        h_r, h_i = h
        r = pl.multiple_of(t * nb, nb)
        n_r = lam_r * h_r - lam_i * h_i + bu_ref[pl.ds(r, nb), 0:S5_STATE]
        n_i = lam_r * h_i + lam_i * h_r + bu_ref[pl.ds(r, nb), S5_STATE:2 * S5_STATE]
        bu_ref[pl.ds(r, nb), 0:S5_STATE] = n_r
        bu_ref[pl.ds(r, nb), S5_STATE:2 * S5_STATE] = n_i
        return n_r, n_i

    h_r, h_i = lax.fori_loop(0, t_steps, s5_step, (s5_ref[0], s5_ref[1]), unroll=2)
    s5_ref[0] = h_r
    s5_ref[1] = h_i
    y = _dot(bu_ref[...].astype(BF16), cblk_ref[...]) + v256_ref[0:1, :] * u
    z = _gelu_tanh(y)
    o_s5 = z * _sigmoid(_dot(z.astype(BF16), wglu_ref[...]) + v256_ref[1:2, :])
    mix_ref[:, 0:GROUP_W] = o_s5.astype(mix_ref.dtype)

    xe_ref[conv_rows:conv_rows + rows, :] = proj_ref[:, O_XBC:O_XBC + SSD_XBC]
    conv = v512_ref[SSD_CONV:SSD_CONV + 1, :]
    for j in range(SSD_CONV):
        conv = conv + xe_ref[j * nb:j * nb + rows, :] * v512_ref[j:j + 1, :]
    tail = xe_ref[rows:rows + conv_rows, :]
    xe_ref[0:conv_rows, :] = tail
    conv_ref[...] = tail
    act = _silu(conv)
    xs = act[:, 0:GROUP_W]
    b_g = act[:, GROUP_W:GROUP_W + 128]
    c_g = act[:, GROUP_W + 128:GROUP_W + 256]
    misc = proj_ref[:, O_MISC:O_MISC + 128]
    dt = _softplus(misc + v128_ref[0:1, :])
    dt_hd = _dot_split(dt, ehd_ref[...], 3)
    la_ref[...] = dt_hd * v256_ref[7:8, :]
    _store_tiles(qf_ref, 0, jnp.concatenate(
        [c_g[:, 0:64], c_g[:, 0:64], c_g[:, 64:128], c_g[:, 64:128]], axis=1))
    _store_tiles(kp_ref, 0, jnp.concatenate(
        [b_g[:, 0:64], b_g[:, 0:64], b_g[:, 64:128], b_g[:, 64:128]], axis=1))
    _store_tiles(vp_ref, 0, xs * dt_hd)
    _cumsum_time(nb, t_steps, la_ref, cp_ref, 256)
    _gated_scan_tile(256, 64, nb, t_steps, qf_ref, kp_ref, cp_ref, vp_ref, ob_ref, st_ssd_ref, j256_ref)
    y = _load_tiles(ob_ref, 0, rows, 2) + v256_ref[2:3, :] * xs
    y = y * _silu(proj_ref[:, O_Z:O_Z + GROUP_W])
    y = y * _rms_scale(y, -1) * v256_ref[3:4, :]
    mix_ref[:, GROUP_W:2 * GROUP_W] = y.astype(mix_ref.dtype)

    fz = proj_ref[:, O_HF:O_HF + GROUP_W]
    la_ref[...] = _logaddexp(_log_sigmoid(fz), v256_ref[4:5, :] + _log_sigmoid(-fz))
    _store_tiles(qf_ref, 0, _silu(proj_ref[:, O_HQ:O_HQ + GROUP_W]))
    _store_tiles(kp_ref, 0, v256_ref[5:6, :] * _sigmoid(-fz))
    _store_tiles(vp_ref, 0, proj_ref[:, O_HI:O_HI + GROUP_W])
    _cumsum_time(nb, t_steps, la_ref, cp_ref, 256)
    _gated_scan_tile(256, 64, nb, t_steps, qf_ref, kp_ref, cp_ref, vp_ref, ob_ref, st_hg_ref, j256_ref)
    o = _load_tiles(ob_ref, 0, rows, 2)
    o = o * _head_rms_scale(o, jn_ref) * v256_ref[6:7, :]
    o = o * _silu(proj_ref[:, O_HGATE:O_HGATE + GROUP_W])
    mix_ref[:, 2 * GROUP_W:3 * GROUP_W] = o.astype(mix_ref.dtype)

    gk = _dot(misc.astype(BF16), w2p_ref[...]) + v128_ref[2:3, :]
    la_ref[:, 0:128] = _log_sigmoid(gk) * (1.0 / GLA_TAU)
    _store_tiles(qf_ref, 0, proj_ref[:, O_GQ:O_GQ + 128] * (GLA_K ** -0.5))
    _store_tiles(kp_ref, 0, proj_ref[:, O_GK:O_GK + 128])
    _store_tiles(vp_ref, 0, proj_ref[:, O_GV:O_GV + GROUP_W])
    _cumsum_time(nb, t_steps, la_ref, cp_ref, 128)
    _gated_scan_tile(128, 32, nb, t_steps, qf_ref, kp_ref, cp_ref, vp_ref, ob_ref, st_gla_ref, j128_ref)
    o = _load_tiles(ob_ref, 0, rows, 2)
    o = o * _head_rms_scale(o, jn_ref) * v256_ref[8:9, :]
    o = o * _silu(proj_ref[:, O_GGATE:O_GGATE + GROUP_W])
    mix_ref[:, 3 * GROUP_W:4 * GROUP_W] = o.astype(mix_ref.dtype)


def _const_spec(shape):
    nd = len(shape)
    return pl.BlockSpec(shape, lambda i, _nd=nd: (0,) * _nd, pipeline_mode=pl.Buffered(1))


def _mixer_prompt(x_tm, nb, t_steps, p):
    rows_total = x_tm.shape[0]
    rows = nb * t_steps
    n_tiles = rows_total // rows
    consts = [p['gmix'], p['win'], p['bblk'], p['lam'], p['cblk'], p['wglu'], p['v256'], p['v128'],
              p['v512'], p['ehd'], p['w2p'], p['j256'], p['j128'], p['jn']]
    in_specs = [pl.BlockSpec((rows, D_MODEL), lambda i: (i, 0))] + [_const_spec(c.shape) for c in consts]
    conv_rows = (SSD_CONV - 1) * nb
    out_shape = (
        jax.ShapeDtypeStruct((rows_total, D_MODEL), BF16),
        jax.ShapeDtypeStruct((2, nb, S5_STATE), F32),
        jax.ShapeDtypeStruct((conv_rows, SSD_XBC), F32),
        jax.ShapeDtypeStruct((nb, 256, 256), F32),
        jax.ShapeDtypeStruct((nb, 256, 256), F32),
        jax.ShapeDtypeStruct((nb, 256, 128), F32),
    )
    out_specs = (
        pl.BlockSpec((rows, D_MODEL), lambda i: (i, 0)),
        pl.BlockSpec((2, nb, S5_STATE), lambda i: (0, 0, 0)),
        pl.BlockSpec((conv_rows, SSD_XBC), lambda i: (0, 0)),
        pl.BlockSpec((nb, 256, 256), lambda i: (0, 0, 0)),
        pl.BlockSpec((nb, 256, 256), lambda i: (0, 0, 0)),
        pl.BlockSpec((nb, 256, 128), lambda i: (0, 0, 0)),
    )
    scratch = [
        pltpu.VMEM((rows, N_PROJ), F32),
        pltpu.VMEM((rows, 2 * S5_STATE), F32),
        pltpu.VMEM((conv_rows + rows, SSD_XBC), F32),
        pltpu.VMEM((2, rows, 128), F32),
        pltpu.VMEM((2, rows, 128), F32),
        pltpu.VMEM((2, rows, 128), F32),
        pltpu.VMEM((2, rows, 128), F32),
        pltpu.VMEM((2, rows, 128), F32),
        pltpu.VMEM((rows, 256), F32),
    ]
    return pl.pallas_call(
        functools.partial(_mixer_prompt_kernel, nb, t_steps),
        grid=(n_tiles,),
        in_specs=in_specs,
        out_specs=out_specs,
        out_shape=out_shape,
        scratch_shapes=scratch,
        compiler_params=pltpu.CompilerParams(dimension_semantics=("arbitrary",),
                                             vmem_limit_bytes=VMEM_LIMIT),
        name="mixer_prompt",
    )(x_tm, *consts)


C_GMIX = 0
C_LAMR = C_GMIX + D_MODEL
C_LAMI = C_LAMR + S5_STATE
C_S5D = C_LAMI + S5_STATE
C_BGLU = C_S5D + GROUP_W
C_CONVW = C_BGLU + GROUP_W
C_CONVB = C_CONVW + SSD_CONV * SSD_XBC
C_DTB = C_CONVB + SSD_XBC
C_A = C_DTB + 8
C_DSKIP = C_A + 8
C_SSDG = C_DSKIP + GROUP_W
C_LOGLB = C_SSDG + GROUP_W
C_OML = C_LOGLB + GROUP_W
C_HGG = C_OML + GROUP_W
C_BGK = C_HGG + GROUP_W
C_GLAG = C_BGK + 128
C_ROWS = C_GLAG + GROUP_W


def _mixer_sample_kernel(xT_ref, col_ref, winT_ref, bblkT_ref, cblkT_ref, wgluT_ref, w2pT_ref,
                         s5_in, conv_in, ssd_in, hg_in, gla_in,
                         mixT_ref, s5_out, conv_out, ssd_out, hg_out, gla_out,
                         proj_ref, act_ref, dt_ref, es_ref, qh_ref, kh_ref, eh_ref,
                         qg_ref, kg_ref, eg_ref, ys_ref):
    h = pl.program_id(0)

    def col(off, n):
        return col_ref[off:off + n, :]

    def col_h(off):
        return col_ref[pl.ds(pl.multiple_of(off + h * 64, 8), 64), :]

    @pl.when(h == 0)
    def _():
        x = xT_ref[...]
        hn = x * _rms_scale(x, 0) * col(C_GMIX, D_MODEL)
        proj_ref[...] = _dot(winT_ref[...], hn.astype(BF16))

        u = proj_ref[O_U:O_U + GROUP_W, :]
        bu = _dot(bblkT_ref[...], u.astype(BF16))
        lam_r, lam_i = col(C_LAMR, S5_STATE), col(C_LAMI, S5_STATE)
        h0_r, h0_i = s5_in[0], s5_in[1]
        h_r = lam_r * h0_r - lam_i * h0_i + bu[0:S5_STATE]
        h_i = lam_r * h0_i + lam_i * h0_r + bu[S5_STATE:2 * S5_STATE]
        s5_out[0] = h_r
        s5_out[1] = h_i
        y = (_dot(cblkT_ref[...], jnp.concatenate([h_r, h_i], axis=0).astype(BF16))
             + col(C_S5D, GROUP_W) * u)
        z = _gelu_tanh(y)
        mixT_ref[0:GROUP_W, :] = z * _sigmoid(_dot(wgluT_ref[...], z.astype(BF16)) + col(C_BGLU, GROUP_W))

        xbc = proj_ref[O_XBC:O_XBC + SSD_XBC, :]
        conv = col(C_CONVB, SSD_XBC) + col(C_CONVW + 3 * SSD_XBC, SSD_XBC) * xbc
        for j in range(SSD_CONV - 1):
            conv = conv + col(C_CONVW + j * SSD_XBC, SSD_XBC) * conv_in[j * SSD_XBC:(j + 1) * SSD_XBC, :]
        conv_out[0:2 * SSD_XBC, :] = conv_in[SSD_XBC:3 * SSD_XBC, :]
        conv_out[2 * SSD_XBC:3 * SSD_XBC, :] = xbc
        act_ref[...] = _silu(conv)
        misc = proj_ref[O_MISC:O_MISC + 128, :]
        dt = _softplus(misc[0:8] + col(C_DTB, 8))
        dt_ref[...] = dt
        es_ref[...] = jnp.exp(dt * col(C_A, 8))

        fz = proj_ref[O_HF:O_HF + GROUP_W, :]
        eh_ref[...] = jnp.exp(_logaddexp(_log_sigmoid(fz), col(C_LOGLB, GROUP_W) + _log_sigmoid(-fz)))
        kh_ref[...] = col(C_OML, GROUP_W) * _sigmoid(-fz)
        qh_ref[...] = _silu(proj_ref[O_HQ:O_HQ + GROUP_W, :])

        gk = _dot(w2pT_ref[...], misc.astype(BF16)) + col(C_BGK, 128)
        eg_ref[...] = jnp.exp(_log_sigmoid(gk) * (1.0 / GLA_TAU))
        kg_ref[...] = proj_ref[O_GK:O_GK + 128, :]
        qg_ref[...] = proj_ref[O_GQ:O_GQ + 128, :] * (GLA_K ** -0.5)

    def state_step(st_in, st_out, e_row, k_row, q_row, v, n_keys):
        def body(i, acc):
            r = pl.multiple_of(i * 64, 64)
            s_new = e_row(i) * st_in[pl.ds(r, 64), :] + k_row(i) * v
            st_out[pl.ds(r, 64), :] = s_new
            return acc + q_row(i) * s_new
        return lax.fori_loop(0, n_keys, body, jnp.zeros_like(v), unroll=4)

    def rows_h(ref, base):
        return ref[pl.ds(pl.multiple_of(base + h * 64, 64), 64), :]

    g = _idiv(h, SSD_HEADS // SSD_NGROUPS)
    xh = rows_h(act_ref, 0)
    e_h = es_ref[pl.ds(h, 1), :]
    o = state_step(ssd_in, ssd_out,
                   lambda i: e_h,
                   lambda i: act_ref[pl.ds(GROUP_W + g * SSD_N + i, 1), :],
                   lambda i: act_ref[pl.ds(GROUP_W + 128 + g * SSD_N + i, 1), :],
                   xh * dt_ref[pl.ds(h, 1), :], SSD_N)
    ys_ref[pl.ds(pl.multiple_of(h * 64, 64), 64), :] = o + col_h(C_DSKIP) * xh

    o = state_step(hg_in, hg_out,
                   lambda i: eh_ref[pl.ds(h * HG_K + i, 1), :],
                   lambda i: kh_ref[pl.ds(h * HG_K + i, 1), :],
                   lambda i: qh_ref[pl.ds(h * HG_K + i, 1), :],
                   rows_h(proj_ref, O_HI), HG_K)
    o = o * _rms_scale(o, 0) * col_h(C_HGG) * _silu(rows_h(proj_ref, O_HGATE))
    mixT_ref[pl.ds(pl.multiple_of(2 * GROUP_W + h * 64, 64), 64), :] = o

    o = state_step(gla_in, gla_out,
                   lambda i: eg_ref[pl.ds(h * GLA_K + i, 1), :],
                   lambda i: kg_ref[pl.ds(h * GLA_K + i, 1), :],
                   lambda i: qg_ref[pl.ds(h * GLA_K + i, 1), :],
                   rows_h(proj_ref, O_GV), GLA_K)
    o = o * _rms_scale(o, 0) * col_h(C_GLAG) * _silu(rows_h(proj_ref, O_GGATE))
    mixT_ref[pl.ds(pl.multiple_of(3 * GROUP_W + h * 64, 64), 64), :] = o

    @pl.when(h == N_HEADS - 1)
    def _():
        y = ys_ref[...] * _silu(proj_ref[O_Z:O_Z + GROUP_W, :])
        mixT_ref[GROUP_W:2 * GROUP_W, :] = y * _rms_scale(y, 0) * col(C_SSDG, GROUP_W)


def _mixer_sample(xT, p, s5T, convT, ssdT, hgT, glaT):
    nbatch = xT.shape[1]
    consts = [xT, p['col'], p['winT'], p['bblkT'], p['cblkT'], p['wgluT'], p['w2pT'], s5T, convT]
    head_blk = lambda a: pl.BlockSpec((a.shape[0] // N_HEADS, nbatch), lambda i: (i, 0))
    res_blk = lambda shape: pl.BlockSpec(shape, lambda i, _nd=len(shape): (0,) * _nd)
    out_shape = (
        jax.ShapeDtypeStruct((D_MODEL, nbatch), F32),
        jax.ShapeDtypeStruct(s5T.shape, F32),
        jax.ShapeDtypeStruct(convT.shape, F32),
        jax.ShapeDtypeStruct(ssdT.shape, F32),
        jax.ShapeDtypeStruct(hgT.shape, F32),
        jax.ShapeDtypeStruct(glaT.shape, F32),
    )
    scratch = [
        pltpu.VMEM((N_PROJ, nbatch), F32),
        pltpu.VMEM((SSD_XBC, nbatch), F32),
        pltpu.VMEM((8, nbatch), F32),
        pltpu.VMEM((8, nbatch), F32),
        pltpu.VMEM((GROUP_W, nbatch), F32),
        pltpu.VMEM((GROUP_W, nbatch), F32),
        pltpu.VMEM((GROUP_W, nbatch), F32),
        pltpu.VMEM((128, nbatch), F32),
        pltpu.VMEM((128, nbatch), F32),
        pltpu.VMEM((128, nbatch), F32),
        pltpu.VMEM((GROUP_W, nbatch), F32),
    ]
    return pl.pallas_call(
        _mixer_sample_kernel,
        grid=(N_HEADS,),
        in_specs=[_const_spec(c.shape) for c in consts] + [head_blk(ssdT), head_blk(hgT), head_blk(glaT)],
        out_specs=(res_blk((D_MODEL, nbatch)), res_blk(s5T.shape), res_blk(convT.shape),
                   head_blk(ssdT), head_blk(hgT), head_blk(glaT)),
        out_shape=out_shape,
        scratch_shapes=scratch,
        compiler_params=pltpu.CompilerParams(dimension_semantics=("arbitrary",),
                                             vmem_limit_bytes=VMEM_LIMIT),
        name="mixer_sample",
    )(*consts, ssdT, hgT, glaT)


FF_TILE = 512


def _mlp_kernel(final, mix_ref, x_ref, wout_ref, g_ref, wup_ref, wdn_ref, gfin_ref, o_ref):
    x1 = x_ref[...] + _dot(mix_ref[...], wout_ref[...])
    hn = (x1 * _rms_scale(x1, -1) * g_ref[...]).astype(BF16)
    acc = x1
    for f in range(D_FF // FF_TILE):
        up = jnp.maximum(_dot(hn, wup_ref[:, f * FF_TILE:(f + 1) * FF_TILE]), 0.0)
        acc = acc + _dot((up * up).astype(BF16), wdn_ref[f * FF_TILE:(f + 1) * FF_TILE, :])
    if final:
        acc = acc * _rms_scale(acc, -1) * gfin_ref[...]
    o_ref[...] = acc


def _mlp(mix, x, wout, g, wup, wdn, gfin, final, tm):
    rows_total = x.shape[0]
    tm = min(tm, rows_total)
    consts = [wout, g, wup, wdn, gfin]
    return pl.pallas_call(
        functools.partial(_mlp_kernel, final),
        grid=(rows_total // tm,),
        in_specs=[pl.BlockSpec((tm, D_MODEL), lambda i: (i, 0)),
                  pl.BlockSpec((tm, D_MODEL), lambda i: (i, 0))] + [_const_spec(c.shape) for c in consts],
        out_specs=pl.BlockSpec((tm, D_MODEL), lambda i: (i, 0)),
        out_shape=jax.ShapeDtypeStruct((rows_total, D_MODEL), F32),
        compiler_params=pltpu.CompilerParams(dimension_semantics=("parallel",),
                                             vmem_limit_bytes=VMEM_LIMIT),
        name="out_mlp",
    )(mix, x, *consts)


def _block_ones(n_rows, row_blk, n_cols, col_blk, scale=1.0):
    r = np.arange(n_rows)[:, None] // row_blk
    c = np.arange(n_cols)[None, :] // col_blk
    return (r == c).astype(np.float32) * scale


def _prepare(norm_mix_g, w_in, s5_lam_re, s5_lam_im, s5_log_dt, s5_b_re, s5_b_im, s5_c_re, s5_c_im,
             s5_d, s5_w_glu, s5_b_glu, ssd_conv_w, ssd_conv_b, ssd_dt_bias, ssd_a_log, ssd_d, ssd_norm_g,
             hg_lb_logits, hg_norm_g, gla_w_gk2, gla_b_gk, gla_norm_g, nbatch_sample):
    depth = w_in.shape[0]
    w = w_in.astype(F32)
    misc = jnp.concatenate([w[:, :, 1024:1028], w[:, :, 2820:2836],
                            jnp.zeros((depth, D_MODEL, 128 - SSD_HEADS - GLA_RANK), F32)], axis=-1)
    win = jnp.concatenate([w[:, :, :1024], w[:, :, 1028:2820], misc], axis=-1).astype(BF16)

    lam_re, lam_im = s5_lam_re.astype(F32), s5_lam_im.astype(F32)
    dt = jnp.exp(s5_log_dt.astype(F32))[..., None]
    ea = jnp.exp(lam_re * dt)
    lb_r, lb_i = ea * jnp.cos(lam_im * dt), ea * jnp.sin(lam_im * dt)
    den = lam_re * lam_re + lam_im * lam_im
    cr = ((lb_r - 1.0) * lam_re + lb_i * lam_im) / den
    ci = (lb_i * lam_re - (lb_r - 1.0) * lam_im) / den
    bb_r = cr[..., None] * s5_b_re - ci[..., None] * s5_b_im
    bb_i = cr[..., None] * s5_b_im + ci[..., None] * s5_b_re
    eye_g = jnp.eye(S5_GROUPS, dtype=F32)
    blk_b = lambda t: jnp.einsum('lgpc,gh->lgchp', t, eye_g).reshape(depth, GROUP_W, S5_STATE)
    blk_c = lambda t: jnp.einsum('lgcp,gh->lgphc', t, eye_g).reshape(depth, S5_STATE, GROUP_W)
    bblk = jnp.concatenate([blk_b(bb_r), blk_b(bb_i)], axis=-1)
    cblk = jnp.concatenate([blk_c(s5_c_re.astype(F32)), -blk_c(s5_c_im.astype(F32))], axis=1)
    lam = jnp.stack([lb_r.reshape(depth, S5_STATE), lb_i.reshape(depth, S5_STATE)], axis=1)

    sm = jax.nn.softmax(hg_lb_logits.astype(F32), axis=0)
    lb = jnp.cumsum(sm, axis=0) - sm[0:1]
    log_lb = jnp.log(jnp.maximum(lb, LB_FLOOR))
    a = -jnp.exp(ssd_a_log.astype(F32))
    rep64 = lambda t: jnp.repeat(t, 64, axis=-1)
    zeros256 = jnp.zeros((depth, GROUP_W), F32)
    v256 = jnp.stack([s5_d, s5_b_glu, rep64(ssd_d), ssd_norm_g, log_lb, 1.0 - lb, hg_norm_g, rep64(a),
                      jnp.tile(gla_norm_g, (1, GLA_HEADS))] + [zeros256] * 7, axis=1).astype(F32)
    pad128 = lambda t: jnp.pad(t, ((0, 0), (0, 128 - t.shape[-1])))
    zeros128 = jnp.zeros((depth, 128), F32)
    v128 = jnp.stack([pad128(ssd_dt_bias), pad128(a), gla_b_gk] + [zeros128] * 5, axis=1).astype(F32)
    v512 = jnp.concatenate([ssd_conv_w, ssd_conv_b[:, None, :], jnp.zeros((depth, 3, SSD_XBC), F32)],
                           axis=1).astype(F32)
    w2p = jnp.pad(gla_w_gk2, ((0, 0), (LR_LO, 128 - LR_LO - GLA_RANK), (0, 0))).astype(BF16)

    shared = dict(
        ehd=jnp.asarray(_block_ones(128, 1, 256, 64) * (np.arange(128)[:, None] < SSD_HEADS), BF16),
        j256=jnp.asarray(_block_ones(256, 64, 256, 64), BF16),
        j128=jnp.asarray(_block_ones(128, 32, 256, 64), BF16),
        jn=jnp.asarray(_block_ones(256, 64, 256, 64, 1.0 / 64), BF16),
    )

    bcast = lambda t: jnp.broadcast_to(t[..., None], t.shape + (nbatch_sample,))
    col = jnp.concatenate([
        norm_mix_g, lam[:, 0], lam[:, 1], s5_d, s5_b_glu, ssd_conv_w.reshape(depth, -1), ssd_conv_b,
        jnp.pad(ssd_dt_bias, ((0, 0), (0, 8 - SSD_HEADS))), jnp.pad(a, ((0, 0), (0, 8 - SSD_HEADS))),
        rep64(ssd_d), ssd_norm_g, log_lb, 1.0 - lb, hg_norm_g, gla_b_gk, jnp.tile(gla_norm_g, (1, GLA_HEADS)),
    ], axis=-1).astype(F32)
    layers = []
    for l in range(depth):
        layers.append(dict(
            gmix=norm_mix_g[l][None, :].astype(F32), win=win[l], bblk=bblk[l].astype(BF16), lam=lam[l],
            cblk=cblk[l].astype(BF16), wglu=s5_w_glu[l].astype(BF16), v256=v256[l], v128=v128[l], v512=v512[l],
            w2p=w2p[l], col=bcast(col[l]),
            winT=win[l].T, bblkT=bblk[l].T.astype(BF16), cblkT=cblk[l].T.astype(BF16),
            wgluT=s5_w_glu[l].T.astype(BF16), w2pT=w2p[l].T, **shared))
    return layers


def _diag_state(st, dk):
    nb = st.shape[0]
    s = st.reshape(nb, N_HEADS, 64, N_HEADS, dk)
    idx = jnp.arange(N_HEADS)
    s = s[:, idx, :, idx, :]
    return s.transpose(1, 0, 3, 2)


def kernel(x_prompt, x_sample, state_s5_re, state_s5_im, state_ssd_conv, state_ssd, state_hgrn, state_gla,
           norm_mix_g, w_in, s5_lam_re, s5_lam_im, s5_log_dt, s5_b_re, s5_b_im, s5_c_re, s5_c_im,
           s5_d, s5_w_glu, s5_b_glu, ssd_conv_w, ssd_conv_b, ssd_dt_bias, ssd_a_log, ssd_d, ssd_norm_g,
           hg_lb_logits, hg_norm_g, gla_w_gk2, gla_b_gk, gla_norm_g, w_out, norm_mlp_g, w_up, w_down,
           norm_final_g):
    nb, seq, _ = x_prompt.shape
    ns = x_sample.shape[0]
    depth = w_in.shape[0]
    t_steps = min(64, seq)

    layers = _prepare(norm_mix_g, w_in, s5_lam_re, s5_lam_im, s5_log_dt, s5_b_re, s5_b_im, s5_c_re, s5_c_im,
                      s5_d, s5_w_glu, s5_b_glu, ssd_conv_w, ssd_conv_b, ssd_dt_bias, ssd_a_log, ssd_d,
                      ssd_norm_g, hg_lb_logits, hg_norm_g, gla_w_gk2, gla_b_gk, gla_norm_g, ns)
    wout16, wup16, wdn16 = w_out.astype(BF16), w_up.astype(BF16), w_down.astype(BF16)
    gfin = norm_final_g[None, :].astype(F32)

    xp = x_prompt.astype(F32).transpose(1, 0, 2).reshape(seq * nb, D_MODEL)
    xs = x_sample.astype(F32).reshape(ns, D_MODEL)

    outs_p = [[] for _ in range(6)]
    outs_s = [[] for _ in range(6)]
    for l in range(depth):
        p = layers[l]
        final = l == depth - 1
        gmlp = norm_mlp_g[l][None, :].astype(F32)

        mix, s5, conv, st_ssd, st_hg, st_gla = _mixer_prompt(xp, nb, t_steps, p)
        xp = _mlp(mix, xp, wout16[l], gmlp, wup16[l], wdn16[l], gfin, final, 512)
        outs_p[0].append(s5[0].reshape(nb, S5_GROUPS, S5_P))
        outs_p[1].append(s5[1].reshape(nb, S5_GROUPS, S5_P))
        outs_p[2].append(conv.reshape(SSD_CONV - 1, nb, SSD_XBC).transpose(1, 0, 2))
        outs_p[3].append(_diag_state(st_ssd, SSD_N))
        outs_p[4].append(_diag_state(st_hg, HG_K))
        outs_p[5].append(_diag_state(st_gla, GLA_K))

        s5T = jnp.stack([state_s5_re[l].reshape(ns, S5_STATE).T, state_s5_im[l].reshape(ns, S5_STATE).T])
        convT = state_ssd_conv[l].transpose(1, 2, 0).reshape((SSD_CONV - 1) * SSD_XBC, ns)
        ssdT = state_ssd[l].transpose(1, 2, 3, 0).reshape(SSD_HEADS * SSD_N * SSD_P, ns)
        hgT = state_hgrn[l].transpose(1, 2, 3, 0).reshape(HG_HEADS * HG_K * HG_V, ns)
        glaT = state_gla[l].transpose(1, 2, 3, 0).reshape(GLA_HEADS * GLA_K * GLA_V, ns)
        mixT, s5n, convn, ssdn, hgn, glan = _mixer_sample(xs.T, p, s5T.astype(F32), convT.astype(F32),
                                                          ssdT.astype(F32), hgT.astype(F32), glaT.astype(F32))
        xs = _mlp(mixT.T.astype(BF16), xs, wout16[l], gmlp, wup16[l], wdn16[l], gfin, final, 512)
        outs_s[0].append(s5n[0].T.reshape(ns, S5_GROUPS, S5_P))
        outs_s[1].append(s5n[1].T.reshape(ns, S5_GROUPS, S5_P))
        outs_s[2].append(convn.reshape(SSD_CONV - 1, SSD_XBC, ns).transpose(2, 0, 1))
        outs_s[3].append(ssdn.reshape(SSD_HEADS, SSD_N, SSD_P, ns).transpose(3, 0, 1, 2))
        outs_s[4].append(hgn.reshape(HG_HEADS, HG_K, HG_V, ns).transpose(3, 0, 1, 2))
        outs_s[5].append(glan.reshape(GLA_HEADS, GLA_K, GLA_V, ns).transpose(3, 0, 1, 2))

    y_prompt = xp.reshape(seq, nb, D_MODEL).transpose(1, 0, 2)
    y_sample = xs.reshape(ns, 1, D_MODEL)
    return (y_prompt, y_sample) + tuple(jnp.stack(v, axis=0) for v in outs_p) \
        + tuple(jnp.stack(v, axis=0) for v in outs_s)
```

```python
import functools
import math

import numpy as np
import jax
import jax.numpy as jnp
from jax import lax
from jax.experimental import pallas as pl
from jax.experimental.pallas import tpu as pltpu

F32 = jnp.float32
BF16 = jnp.bfloat16

D_MODEL = 1024
GROUP_W = 256
S5_GROUPS, S5_CH, S5_P = 16, 16, 64
S5_STATE = S5_GROUPS * S5_P
SSD_HEADS, SSD_N, SSD_P, SSD_NGROUPS, SSD_CONV, SSD_XBC = 4, 64, 64, 2, 4, 512
HG_HEADS, HG_K, HG_V = 4, 64, 64
GLA_HEADS, GLA_K, GLA_V, GLA_RANK, GLA_TAU = 4, 32, 64, 16, 16.0
N_HEADS = 4
D_FF = 4096
EPS = 1e-6
LB_FLOOR = 1e-30

O_U, O_Z, O_XBC = 0, 256, 512
O_HQ, O_HF, O_HI, O_HGATE = 1024, 1280, 1536, 1792
O_GQ, O_GK, O_GV, O_GGATE = 2048, 2176, 2304, 2560
O_MISC = 2816
N_PROJ = 2944
LR_LO = SSD_HEADS

SUB = 16
NEG = -1e30
LOG2E = 1.4426950408889634
VMEM_LIMIT = 56 * 1024 * 1024


def _sigmoid(x):
    return 0.5 * (1.0 + jnp.tanh(0.5 * x))


def _sigmoid_rel(x):
    return 1.0 / (1.0 + jnp.exp(-x))


def _silu(x):
    return x * _sigmoid(x)


def _softplus(x):
    return jnp.maximum(x, 0.0) + jnp.log1p(jnp.exp(-jnp.abs(x)))


def _log_sigmoid(x):
    return -_softplus(-x)


def _gelu_tanh(x):
    c = math.sqrt(2.0 / math.pi)
    return x * (0.5 * (1.0 + jnp.tanh(c * (x + 0.044715 * (x * x * x)))))


def _rms_scale(x, axis):
    return lax.rsqrt(jnp.mean(x * x, axis=axis, keepdims=True) + EPS)


def _dot(a, b):
    return jnp.dot(a, b, preferred_element_type=F32)


def _dot_nt(a, b):
    return lax.dot_general(a, b, (((1,), (1,)), ((), ())), preferred_element_type=F32)


def _dot_tn(a, b):
    return lax.dot_general(a, b, (((0,), (0,)), ((), ())), preferred_element_type=F32)


def _dot_split(a, b16, terms):
    acc, rest = None, a
    for _ in range(terms):
        piece = rest.astype(BF16)
        part = _dot(piece, b16)
        acc = part if acc is None else acc + part
        rest = rest - piece.astype(F32)
    return acc


def _idiv(x, n):
    shift = int(math.log2(n))
    assert 1 << shift == n
    return x >> shift


def _store_tiles(ref, row0, val):
    rows = val.shape[0]
    for j in range(val.shape[1] // 128):
        ref[j, row0:row0 + rows, :] = val[:, j * 128:(j + 1) * 128]


def _load_tiles(ref, row0, rows, n_tiles):
    return jnp.concatenate([ref[j, row0:row0 + rows, :] for j in range(n_tiles)], axis=1)


def _load_seq(ref, start, t_steps, nb, n_tiles):
    tiles = [ref[j, pl.ds(start, t_steps, stride=nb), :] for j in range(n_tiles)]
    return tiles[0] if n_tiles == 1 else jnp.concatenate(tiles, axis=1)


def _load_rows(ref, start, rows, n_tiles):
    tiles = [ref[j, pl.ds(start, rows), :] for j in range(n_tiles)]
    return tiles[0] if n_tiles == 1 else jnp.concatenate(tiles, axis=1)


def _gated_scan_tile(kd_lanes, dk, nb, t_steps, qf_ref, kp_ref, cp_ref, vp_ref, ob_ref, os_ref, st_ref,
                     s16_ref, j_ref, ko_ref, vo_ref):
    rows = nb * t_steps
    sub_rows = SUB * nb
    n_sub = rows // sub_rows
    n_kt = kd_lanes // 128

    def sub_body(sb):
        r0 = pl.multiple_of(sb * sub_rows, sub_rows)
        accs = [[None] * SUB for _ in range(2)]
        for j in range(SUB):
            reps = SUB - j
            rj = pl.multiple_of(r0 + nb * j, nb)
            q = _load_rows(qf_ref, rj, reps * nb, n_kt)
            c = _load_rows(cp_ref, rj, reps * nb, n_kt)
            k_j = jnp.tile(_load_rows(kp_ref, rj, nb, n_kt), (reps, 1))
            c_j = jnp.tile(_load_rows(cp_ref, rj, nb, n_kt), (reps, 1))
            w = _dot((q * k_j * jnp.exp2(c - c_j)).astype(BF16), j_ref[...])
            for n in range(2):
                v_j = vp_ref[n, pl.ds(rj, nb), :]
                for i in range(j, SUB):
                    term = w[(i - j) * nb:(i - j + 1) * nb, n * 128:(n + 1) * 128] * v_j
                    accs[n][i] = term if accs[n][i] is None else accs[n][i] + term
            yield
        for n in range(2):
            ob_ref[n, pl.ds(r0, sub_rows), :] = jnp.concatenate(accs[n], axis=0)

    t_idx = lax.broadcasted_iota(jnp.int32, (t_steps, kd_lanes), 0)
    ri = lax.broadcasted_iota(jnp.int32, (t_steps, N_HEADS * t_steps), 0)
    ci = lax.broadcasted_iota(jnp.int32, (t_steps, N_HEADS * t_steps), 1) & (t_steps - 1)
    lane_tile_w = min(kd_lanes, 128)
    heads_per_tile = lane_tile_w // dk
    tile_head = _idiv(lax.broadcasted_iota(jnp.int32, (64, lane_tile_w), 1), dk)
    levels = []
    c_sz = SUB
    while c_sz < t_steps:
        levels.append(c_sz)
        c_sz *= 2

    def seq_body(b):
        qb = _load_seq(qf_ref, b, t_steps, nb, n_kt)
        kb = _load_seq(kp_ref, b, t_steps, nb, n_kt)
        cb = _load_seq(cp_ref, b, t_steps, nb, n_kt)
        vb16 = _load_seq(vp_ref, b, t_steps, nb, 2).astype(BF16)
        c_last = cb[t_steps - 1:t_steps, :]

        o = _dot_nt((qb * jnp.exp2(cb)).astype(BF16), s16_ref[b])
        yield
        s = None
        for c_sz in levels:
            pieces = []
            for m in range(t_steps // (2 * c_sz)):
                a_row = m * 2 * c_sz + c_sz - 1
                pieces.append(jnp.broadcast_to(cb[a_row:a_row + 1, :], (2 * c_sz, kd_lanes)))
            anchor = pieces[0] if len(pieces) == 1 else jnp.concatenate(pieces, axis=0)
            upper = (t_idx & c_sz) != 0
            q_l = (qb * jnp.exp2(jnp.where(upper, cb - anchor, NEG))).astype(BF16)
            k_l = (kb * jnp.exp2(jnp.where(upper, NEG, anchor - cb))).astype(BF16)
            s_l = _dot_nt(q_l, jnp.tile(k_l, (N_HEADS, 1)) * ko_ref[...])
            if 2 * c_sz < t_steps:
                shift = int(math.log2(2 * c_sz))
                s_l = jnp.where((ri >> shift) == (ci >> shift), s_l, 0.0)
            s = s_l if s is None else s + s_l
            yield
        if s is not None:
            o = o + _dot(s.astype(BF16), jnp.tile(vb16, (N_HEADS, 1)) * vo_ref[...])
        for j in range(2):
            os_ref[j, pl.ds(b, t_steps, stride=nb), :] = o[:, j * 128:(j + 1) * 128]
        yield

        upd = _dot_tn(vb16, (kb * jnp.exp2(c_last - cb)).astype(BF16))
        decay = jnp.exp2(c_last)
        for h in range(N_HEADS):
            r0, l0 = h * 64, (h // heads_per_tile) * lane_tile_w
            own = tile_head == (h % heads_per_tile)
            blk = (decay[:, l0:l0 + lane_tile_w] * st_ref[b, r0:r0 + 64, l0:l0 + lane_tile_w]
                   + jnp.where(own, upd[r0:r0 + 64, l0:l0 + lane_tile_w], 0.0))
            st_ref[b, r0:r0 + 64, l0:l0 + lane_tile_w] = blk
            s16_ref[b, r0:r0 + 64, l0:l0 + lane_tile_w] = blk.astype(BF16)

    assert nb % n_sub == 0
    seq_per_trip = nb // n_sub

    def trip(i, carry):
        sub = sub_body(i)
        seqs = [seq_body(i * seq_per_trip + s) for s in range(seq_per_trip)]
        turn = 0
        while sub is not None or seqs:
            if sub is not None and next(sub, "done") == "done":
                sub = None
            if seqs:
                turn %= len(seqs)
                if next(seqs[turn], "done") == "done":
                    seqs.pop(turn)
                else:
                    turn += 1
        return carry

    lax.fori_loop(0, n_sub, trip, 0, unroll=2)


def _cumsum_time(nb, t_steps, src_ref, cp_ref, kd_lanes):
    def body(t, c):
        r = pl.multiple_of(t * nb, nb)
        c = c + src_ref[pl.ds(r, nb), :kd_lanes] * LOG2E
        for j in range(kd_lanes // 128):
            cp_ref[j, pl.ds(r, nb), :] = c[:, j * 128:(j + 1) * 128]
        return c

    lax.fori_loop(0, t_steps, body, jnp.zeros((nb, kd_lanes), F32), unroll=4)


def _head_rms_scale(o, jn_ref):
    return lax.rsqrt(_dot_split(o * o, jn_ref[...], 2) + EPS)


def _mixer_prompt_kernel(nb, t_steps,
                         x_ref, gmix_ref, win_ref, bblk_ref, lam_ref, cblk_ref, wglu_ref,
                         v256_ref, v128_ref, v512_ref, w2p_ref, ehd_ref, j256_ref, j128_ref, jn_ref,
                         ko256_ref, ko128_ref,
                         mix_ref, s5_ref, conv_ref, st_ssd_ref, st_hg_ref, st_gla_ref,
                         proj_ref, bu_ref, xe_ref, qf_ref, kp_ref, cp_ref, vp_ref, ob_ref, os_ref, la_ref,
                         s16_ssd_ref, s16_hg_ref, s16_gla_ref):
    rows = nb * t_steps
    conv_rows = (SSD_CONV - 1) * nb

    @pl.when(pl.program_id(0) == 0)
    def _():
        for ref in (s5_ref, st_ssd_ref, st_hg_ref, st_gla_ref, s16_ssd_ref, s16_hg_ref, s16_gla_ref):
            ref[...] = jnp.zeros_like(ref)
        xe_ref[0:conv_rows, :] = jnp.zeros((conv_rows, SSD_XBC), F32)

    x = x_ref[...]
    hn = x * _rms_scale(x, -1) * gmix_ref[...]
    proj_ref[...] = _dot(hn.astype(BF16), win_ref[...])

    u = proj_ref[:, O_U:O_U + GROUP_W]
    bu_ref[...] = _dot(u.astype(BF16), bblk_ref[...])
    lam_r = jnp.broadcast_to(lam_ref[0:1, :], (nb, S5_STATE))
    lam_i = jnp.broadcast_to(lam_ref[1:2, :], (nb, S5_STATE))

    def s5_step(t, h):
        h_r, h_i = h
        r = pl.multiple_of(t * nb, nb)
        n_r = lam_r * h_r - lam_i * h_i + bu_ref[pl.ds(r, nb), 0:S5_STATE]
        n_i = lam_r * h_i + lam_i * h_r + bu_ref[pl.ds(r, nb), S5_STATE:2 * S5_STATE]
        bu_ref[pl.ds(r, nb), 0:S5_STATE] = n_r
        bu_ref[pl.ds(r, nb), S5_STATE:2 * S5_STATE] = n_i
        return n_r, n_i

    h_r, h_i = lax.fori_loop(0, t_steps, s5_step, (s5_ref[0], s5_ref[1]), unroll=2)
    s5_ref[0] = h_r
    s5_ref[1] = h_i
    y = _dot(bu_ref[...].astype(BF16), cblk_ref[...]) + v256_ref[0:1, :] * u
    z = _gelu_tanh(y)
    o_s5 = z * _sigmoid(_dot(z.astype(BF16), wglu_ref[...]) + v256_ref[1:2, :])
    mix_ref[:, 0:GROUP_W] = o_s5.astype(mix_ref.dtype)

    xe_ref[conv_rows:conv_rows + rows, :] = proj_ref[:, O_XBC:O_XBC + SSD_XBC]
    conv = v512_ref[SSD_CONV:SSD_CONV + 1, :]
    for j in range(SSD_CONV):
        conv = conv + xe_ref[j * nb:j * nb + rows, :] * v512_ref[j:j + 1, :]
    tail = xe_ref[rows:rows + conv_rows, :]
    xe_ref[0:conv_rows, :] = tail
    conv_ref[...] = tail
    act = _silu(conv)
    xs = act[:, 0:GROUP_W]
    b_g = act[:, GROUP_W:GROUP_W + 128]
    c_g = act[:, GROUP_W + 128:GROUP_W + 256]
    misc = proj_ref[:, O_MISC:O_MISC + 128]
    dt = _softplus(misc + v128_ref[0:1, :])
    dt_hd = _dot_split(dt, ehd_ref[...], 3)
    la_ref[...] = dt_hd * v256_ref[7:8, :]
    _store_tiles(qf_ref, 0, jnp.concatenate(
        [c_g[:, 0:64], c_g[:, 0:64], c_g[:, 64:128], c_g[:, 64:128]], axis=1))
    _store_tiles(kp_ref, 0, jnp.concatenate(
        [b_g[:, 0:64], b_g[:, 0:64], b_g[:, 64:128], b_g[:, 64:128]], axis=1))
    _store_tiles(vp_ref, 0, xs * dt_hd)
    _cumsum_time(nb, t_steps, la_ref, cp_ref, 256)
    _gated_scan_tile(256, 64, nb, t_steps, qf_ref, kp_ref, cp_ref, vp_ref, ob_ref, os_ref, st_ssd_ref,
                     s16_ssd_ref, j256_ref, ko256_ref, ko256_ref)
    y = _load_tiles(ob_ref, 0, rows, 2) + _load_tiles(os_ref, 0, rows, 2) + v256_ref[2:3, :] * xs
    y = y * _silu(proj_ref[:, O_Z:O_Z + GROUP_W])
    y = y * _rms_scale(y, -1) * v256_ref[3:4, :]
    mix_ref[:, GROUP_W:2 * GROUP_W] = y.astype(mix_ref.dtype)

    sig_f = _sigmoid_rel(proj_ref[:, O_HF:O_HF + GROUP_W])
    nsig_f = 1.0 - sig_f
    la_ref[...] = jnp.log(sig_f + v256_ref[4:5, :] * nsig_f)
    _store_tiles(qf_ref, 0, _silu(proj_ref[:, O_HQ:O_HQ + GROUP_W]))
    _store_tiles(kp_ref, 0, v256_ref[5:6, :] * nsig_f)
    _store_tiles(vp_ref, 0, proj_ref[:, O_HI:O_HI + GROUP_W])
    _cumsum_time(nb, t_steps, la_ref, cp_ref, 256)
    _gated_scan_tile(256, 64, nb, t_steps, qf_ref, kp_ref, cp_ref, vp_ref, ob_ref, os_ref, st_hg_ref,
                     s16_hg_ref, j256_ref, ko256_ref, ko256_ref)
    o = _load_tiles(ob_ref, 0, rows, 2) + _load_tiles(os_ref, 0, rows, 2)
    o = o * _head_rms_scale(o, jn_ref) * v256_ref[6:7, :]
    o = o * _silu(proj_ref[:, O_HGATE:O_HGATE + GROUP_W])
    mix_ref[:, 2 * GROUP_W:3 * GROUP_W] = o.astype(mix_ref.dtype)

    gk = _dot(misc.astype(BF16), w2p_ref[...]) + v128_ref[2:3, :]
    la_ref[:, 0:128] = _log_sigmoid(gk) * (1.0 / GLA_TAU)
    _store_tiles(qf_ref, 0, proj_ref[:, O_GQ:O_GQ + 128] * (GLA_K ** -0.5))
    _store_tiles(kp_ref, 0, proj_ref[:, O_GK:O_GK + 128])
    _store_tiles(vp_ref, 0, proj_ref[:, O_GV:O_GV + GROUP_W])
    _cumsum_time(nb, t_steps, la_ref, cp_ref, 128)
    _gated_scan_tile(128, 32, nb, t_steps, qf_ref, kp_ref, cp_ref, vp_ref, ob_ref, os_ref, st_gla_ref,
                     s16_gla_ref, j128_ref, ko128_ref, ko256_ref)
    o = _load_tiles(ob_ref, 0, rows, 2) + _load_tiles(os_ref, 0, rows, 2)
    o = o * _head_rms_scale(o, jn_ref) * v256_ref[8:9, :]
    o = o * _silu(proj_ref[:, O_GGATE:O_GGATE + GROUP_W])
    mix_ref[:, 3 * GROUP_W:4 * GROUP_W] = o.astype(mix_ref.dtype)


def _const_spec(shape):
    nd = len(shape)
    return pl.BlockSpec(shape, lambda i, _nd=nd: (0,) * _nd, pipeline_mode=pl.Buffered(1))


def _layer_spec(arr, layer):
    nd = arr.ndim - 1
    return pl.BlockSpec((None,) + arr.shape[1:], lambda i, _nd=nd: (layer,) + (0,) * _nd,
                        pipeline_mode=pl.Buffered(1))


PROMPT_LAYER_PARAMS = ('gmix', 'win', 'bblk', 'lam', 'cblk', 'wglu', 'v256', 'v128', 'v512', 'w2p')
PROMPT_SHARED_PARAMS = ('ehd', 'j256', 'j128', 'jn')


def _mixer_prompt(x_tm, nb, t_steps, p, layer):
    rows_total = x_tm.shape[0]
    rows = nb * t_steps
    n_tiles = rows_total // rows
    layered = [p[k] for k in PROMPT_LAYER_PARAMS]
    shared = [p[k] for k in PROMPT_SHARED_PARAMS] + [
        jnp.asarray(_block_ones(N_HEADS * t_steps, t_steps, 256, 64), BF16),
        jnp.asarray(_block_ones(N_HEADS * t_steps, t_steps, 128, 32), BF16)]
    consts = layered + shared
    in_specs = ([pl.BlockSpec((rows, D_MODEL), lambda i: (i, 0))]
                + [_layer_spec(c, layer) for c in layered] + [_const_spec(c.shape) for c in shared])
    conv_rows = (SSD_CONV - 1) * nb
    out_shape = (
        jax.ShapeDtypeStruct((rows_total, D_MODEL), BF16),
        jax.ShapeDtypeStruct((2, nb, S5_STATE), F32),
        jax.ShapeDtypeStruct((conv_rows, SSD_XBC), F32),
        jax.ShapeDtypeStruct((nb, 256, 256), F32),
        jax.ShapeDtypeStruct((nb, 256, 256), F32),
        jax.ShapeDtypeStruct((nb, 256, 128), F32),
    )
    out_specs = (
        pl.BlockSpec((rows, D_MODEL), lambda i: (i, 0)),
        pl.BlockSpec((2, nb, S5_STATE), lambda i: (0, 0, 0)),
        pl.BlockSpec((conv_rows, SSD_XBC), lambda i: (0, 0)),
        pl.BlockSpec((nb, 256, 256), lambda i: (0, 0, 0)),
        pl.BlockSpec((nb, 256, 256), lambda i: (0, 0, 0)),
        pl.BlockSpec((nb, 256, 128), lambda i: (0, 0, 0)),
    )
    scratch = [
        pltpu.VMEM((rows, N_PROJ), F32),
        pltpu.VMEM((rows, 2 * S5_STATE), F32),
        pltpu.VMEM((conv_rows + rows, SSD_XBC), F32),
        pltpu.VMEM((2, rows, 128), F32),
        pltpu.VMEM((2, rows, 128), F32),
        pltpu.VMEM((2, rows, 128), F32),
        pltpu.VMEM((2, rows, 128), F32),
        pltpu.VMEM((2, rows, 128), F32),
        pltpu.VMEM((2, rows, 128), F32),
        pltpu.VMEM((rows, 256), F32),
        pltpu.VMEM((nb, 256, 256), BF16),
        pltpu.VMEM((nb, 256, 256), BF16),
        pltpu.VMEM((nb, 256, 128), BF16),
    ]
    return pl.pallas_call(
        functools.partial(_mixer_prompt_kernel, nb, t_steps),
        grid=(n_tiles,),
        in_specs=in_specs,
        out_specs=out_specs,
        out_shape=out_shape,
        scratch_shapes=scratch,
        compiler_params=pltpu.CompilerParams(dimension_semantics=("arbitrary",),
                                             vmem_limit_bytes=VMEM_LIMIT),
        name="mixer_prompt",
    )(x_tm, *consts)


C_GMIX = 0
C_LAMR = C_GMIX + D_MODEL
C_LAMI = C_LAMR + S5_STATE
C_S5D = C_LAMI + S5_STATE
C_BGLU = C_S5D + GROUP_W
C_CONVW = C_BGLU + GROUP_W
C_CONVB = C_CONVW + SSD_CONV * SSD_XBC
C_DTB = C_CONVB + SSD_XBC
C_A = C_DTB + 8
C_DSKIP = C_A + 8
C_SSDG = C_DSKIP + GROUP_W
C_LBF = C_SSDG + GROUP_W
C_OML = C_LBF + GROUP_W
C_HGG = C_OML + GROUP_W
C_BGK = C_HGG + GROUP_W
C_GLAG = C_BGK + 128
C_ROWS = C_GLAG + GROUP_W


def _mixer_sample_kernel(xT_ref, col_ref, winT_ref, bblkT_ref, cblkT_ref, wgluT_ref, w2pT_ref,
                         s5_in, conv_in, ssd_in, hg_in, gla_in,
                         mixT_ref, s5_out, conv_out, ssd_out, hg_out, gla_out,
                         proj_ref, act_ref, dt_ref, es_ref, qh_ref, kh_ref, eh_ref,
                         qg_ref, kg_ref, eg_ref, ys_ref):
    h = pl.program_id(0)

    def col(off, n):
        return col_ref[off:off + n, :]

    def col_h(off):
        return col_ref[pl.ds(pl.multiple_of(off + h * 64, 8), 64), :]

    @pl.when(h == 0)
    def _():
        x = xT_ref[...]
        hn = x * _rms_scale(x, 0) * col(C_GMIX, D_MODEL)
        proj_ref[...] = _dot(winT_ref[...], hn.astype(BF16))

        u = proj_ref[O_U:O_U + GROUP_W, :]
        bu = _dot(bblkT_ref[...], u.astype(BF16))
        lam_r, lam_i = col(C_LAMR, S5_STATE), col(C_LAMI, S5_STATE)
        h0_r, h0_i = s5_in[0], s5_in[1]
        h_r = lam_r * h0_r - lam_i * h0_i + bu[0:S5_STATE]
        h_i = lam_r * h0_i + lam_i * h0_r + bu[S5_STATE:2 * S5_STATE]
        s5_out[0] = h_r
        s5_out[1] = h_i
        y = (_dot(cblkT_ref[...], jnp.concatenate([h_r, h_i], axis=0).astype(BF16))
             + col(C_S5D, GROUP_W) * u)
        z = _gelu_tanh(y)
        mixT_ref[0:GROUP_W, :] = z * _sigmoid(_dot(wgluT_ref[...], z.astype(BF16)) + col(C_BGLU, GROUP_W))

        xbc = proj_ref[O_XBC:O_XBC + SSD_XBC, :]
        conv = col(C_CONVB, SSD_XBC) + col(C_CONVW + 3 * SSD_XBC, SSD_XBC) * xbc
        for j in range(SSD_CONV - 1):
            conv = conv + col(C_CONVW + j * SSD_XBC, SSD_XBC) * conv_in[j * SSD_XBC:(j + 1) * SSD_XBC, :]
        conv_out[0:2 * SSD_XBC, :] = conv_in[SSD_XBC:3 * SSD_XBC, :]
        conv_out[2 * SSD_XBC:3 * SSD_XBC, :] = xbc
        act_ref[...] = _silu(conv)
        misc = proj_ref[O_MISC:O_MISC + 128, :]
        dt = _softplus(misc[0:8] + col(C_DTB, 8))
        dt_ref[...] = dt
        es_ref[...] = jnp.exp(dt * col(C_A, 8))

        sig_f = _sigmoid_rel(proj_ref[O_HF:O_HF + GROUP_W, :])
        nsig_f = 1.0 - sig_f
        eh_ref[...] = sig_f + col(C_LBF, GROUP_W) * nsig_f
        kh_ref[...] = col(C_OML, GROUP_W) * nsig_f
        qh_ref[...] = _silu(proj_ref[O_HQ:O_HQ + GROUP_W, :])

        gk = _dot(w2pT_ref[...], misc.astype(BF16)) + col(C_BGK, 128)
        eg_ref[...] = jnp.exp(_log_sigmoid(gk) * (1.0 / GLA_TAU))
        kg_ref[...] = proj_ref[O_GK:O_GK + 128, :]
        qg_ref[...] = proj_ref[O_GQ:O_GQ + 128, :] * (GLA_K ** -0.5)

    def state_step(st_in, st_out, e_row, k_row, q_row, v, n_keys):
        def body(i, acc):
            r = pl.multiple_of(i * 64, 64)
            s_new = e_row(i) * st_in[pl.ds(r, 64), :] + k_row(i) * v
            st_out[pl.ds(r, 64), :] = s_new
            return acc + q_row(i) * s_new
        return lax.fori_loop(0, n_keys, body, jnp.zeros_like(v), unroll=4)

    def rows_h(ref, base):
        return ref[pl.ds(pl.multiple_of(base + h * 64, 64), 64), :]

    g = _idiv(h, SSD_HEADS // SSD_NGROUPS)
    xh = rows_h(act_ref, 0)
    e_h = es_ref[pl.ds(h, 1), :]
    o = state_step(ssd_in, ssd_out,
                   lambda i: e_h,
                   lambda i: act_ref[pl.ds(GROUP_W + g * SSD_N + i, 1), :],
                   lambda i: act_ref[pl.ds(GROUP_W + 128 + g * SSD_N + i, 1), :],
                   xh * dt_ref[pl.ds(h, 1), :], SSD_N)
    ys_ref[pl.ds(pl.multiple_of(h * 64, 64), 64), :] = o + col_h(C_DSKIP) * xh

    o = state_step(hg_in, hg_out,
                   lambda i: eh_ref[pl.ds(h * HG_K + i, 1), :],
                   lambda i: kh_ref[pl.ds(h * HG_K + i, 1), :],
                   lambda i: qh_ref[pl.ds(h * HG_K + i, 1), :],
                   rows_h(proj_ref, O_HI), HG_K)
    o = o * _rms_scale(o, 0) * col_h(C_HGG) * _silu(rows_h(proj_ref, O_HGATE))
    mixT_ref[pl.ds(pl.multiple_of(2 * GROUP_W + h * 64, 64), 64), :] = o

    o = state_step(gla_in, gla_out,
                   lambda i: eg_ref[pl.ds(h * GLA_K + i, 1), :],
                   lambda i: kg_ref[pl.ds(h * GLA_K + i, 1), :],
                   lambda i: qg_ref[pl.ds(h * GLA_K + i, 1), :],
                   rows_h(proj_ref, O_GV), GLA_K)
    o = o * _rms_scale(o, 0) * col_h(C_GLAG) * _silu(rows_h(proj_ref, O_GGATE))
    mixT_ref[pl.ds(pl.multiple_of(3 * GROUP_W + h * 64, 64), 64), :] = o

    @pl.when(h == N_HEADS - 1)
    def _():
        y = ys_ref[...] * _silu(proj_ref[O_Z:O_Z + GROUP_W, :])
        mixT_ref[GROUP_W:2 * GROUP_W, :] = y * _rms_scale(y, 0) * col(C_SSDG, GROUP_W)


SAMPLE_LAYER_PARAMS = ('col', 'winT', 'bblkT', 'cblkT', 'wgluT', 'w2pT')


def _mixer_sample(xT, p, layer, s5T, convT, ssdT, hgT, glaT):
    nbatch = xT.shape[1]
    layered = [p[k] for k in SAMPLE_LAYER_PARAMS] + [s5T, convT]
    head_in = lambda a: pl.BlockSpec((None, a.shape[1] // N_HEADS, nbatch), lambda i: (layer, i, 0))
    head_blk = lambda a: pl.BlockSpec((a.shape[1] // N_HEADS, nbatch), lambda i: (i, 0))
    res_blk = lambda shape: pl.BlockSpec(shape, lambda i, _nd=len(shape): (0,) * _nd)
    out_shape = (
        jax.ShapeDtypeStruct((D_MODEL, nbatch), F32),
        jax.ShapeDtypeStruct(s5T.shape[1:], F32),
        jax.ShapeDtypeStruct(convT.shape[1:], F32),
        jax.ShapeDtypeStruct(ssdT.shape[1:], F32),
        jax.ShapeDtypeStruct(hgT.shape[1:], F32),
        jax.ShapeDtypeStruct(glaT.shape[1:], F32),
    )
    scratch = [
        pltpu.VMEM((N_PROJ, nbatch), F32),
        pltpu.VMEM((SSD_XBC, nbatch), F32),
        pltpu.VMEM((8, nbatch), F32),
        pltpu.VMEM((8, nbatch), F32),
        pltpu.VMEM((GROUP_W, nbatch), F32),
        pltpu.VMEM((GROUP_W, nbatch), F32),
        pltpu.VMEM((GROUP_W, nbatch), F32),
        pltpu.VMEM((128, nbatch), F32),
        pltpu.VMEM((128, nbatch), F32),
        pltpu.VMEM((128, nbatch), F32),
        pltpu.VMEM((GROUP_W, nbatch), F32),
    ]
    return pl.pallas_call(
        _mixer_sample_kernel,
        grid=(N_HEADS,),
        in_specs=([_const_spec(xT.shape)] + [_layer_spec(c, layer) for c in layered]
                  + [head_in(ssdT), head_in(hgT), head_in(glaT)]),
        out_specs=(res_blk((D_MODEL, nbatch)), res_blk(s5T.shape[1:]), res_blk(convT.shape[1:]),
                   head_blk(ssdT), head_blk(hgT), head_blk(glaT)),
        out_shape=out_shape,
        scratch_shapes=scratch,
        compiler_params=pltpu.CompilerParams(dimension_semantics=("arbitrary",),
                                             vmem_limit_bytes=VMEM_LIMIT),
        name="mixer_sample",
    )(xT, *layered, ssdT, hgT, glaT)


FF_TILE = 512


def _mlp_kernel(final, mix_ref, x_ref, wout_ref, g_ref, wup_ref, wdn_ref, gfin_ref, o_ref):
    x1 = x_ref[...] + _dot(mix_ref[...], wout_ref[...])
    hn = (x1 * _rms_scale(x1, -1) * g_ref[...]).astype(BF16)
    acc = x1
    for f in range(D_FF // FF_TILE):
        up = jnp.maximum(_dot(hn, wup_ref[:, f * FF_TILE:(f + 1) * FF_TILE]), 0.0)
        acc = acc + _dot((up * up).astype(BF16), wdn_ref[f * FF_TILE:(f + 1) * FF_TILE, :])
    if final:
        acc = acc * _rms_scale(acc, -1) * gfin_ref[...]
    o_ref[...] = acc


def _mlp(mix, x, p, layer, final, tm):
    rows_total = x.shape[0]
    tm = min(tm, rows_total)
    layered = [p['wout'], p['gmlp'], p['wup'], p['wdn']]
    consts = layered + [p['gfin']]
    return pl.pallas_call(
        functools.partial(_mlp_kernel, final),
        grid=(rows_total // tm,),
        in_specs=[pl.BlockSpec((tm, D_MODEL), lambda i: (i, 0)),
                  pl.BlockSpec((tm, D_MODEL), lambda i: (i, 0))]
                 + [_layer_spec(c, layer) for c in layered] + [_const_spec(p['gfin'].shape)],
        out_specs=pl.BlockSpec((tm, D_MODEL), lambda i: (i, 0)),
        out_shape=jax.ShapeDtypeStruct((rows_total, D_MODEL), F32),
        compiler_params=pltpu.CompilerParams(dimension_semantics=("parallel",),
                                             vmem_limit_bytes=VMEM_LIMIT),
        name="out_mlp",
    )(mix, x, *consts)


def _block_ones(n_rows, row_blk, n_cols, col_blk, scale=1.0):
    r = np.arange(n_rows)[:, None] // row_blk
    c = np.arange(n_cols)[None, :] // col_blk
    return (r == c).astype(np.float32) * scale


def _prepare(norm_mix_g, w_in, s5_lam_re, s5_lam_im, s5_log_dt, s5_b_re, s5_b_im, s5_c_re, s5_c_im,
             s5_d, s5_w_glu, s5_b_glu, ssd_conv_w, ssd_conv_b, ssd_dt_bias, ssd_a_log, ssd_d, ssd_norm_g,
             hg_lb_logits, hg_norm_g, gla_w_gk2, gla_b_gk, gla_norm_g, nbatch_sample):
    depth = w_in.shape[0]
    cut = lambda lo, hi: w_in[:, :, lo:hi].astype(BF16)
    win = jnp.concatenate([cut(0, 1024), cut(1028, 2820), cut(1024, 1028), cut(2820, 2836),
                           jnp.zeros((depth, D_MODEL, 128 - SSD_HEADS - GLA_RANK), BF16)], axis=-1)

    lam_re, lam_im = s5_lam_re.astype(F32), s5_lam_im.astype(F32)
    dt = jnp.exp(s5_log_dt.astype(F32))[..., None]
    ea = jnp.exp(lam_re * dt)
    lb_r, lb_i = ea * jnp.cos(lam_im * dt), ea * jnp.sin(lam_im * dt)
    den = lam_re * lam_re + lam_im * lam_im
    cr = ((lb_r - 1.0) * lam_re + lb_i * lam_im) / den
    ci = (lb_i * lam_re - (lb_r - 1.0) * lam_im) / den
    bb_r = cr[..., None] * s5_b_re - ci[..., None] * s5_b_im
    bb_i = cr[..., None] * s5_b_im + ci[..., None] * s5_b_re
    eye_g = jnp.eye(S5_GROUPS, dtype=F32)
    blk_b = lambda t: jnp.einsum('lgpc,gh->lgchp', t, eye_g).reshape(depth, GROUP_W, S5_STATE)
    blk_c = lambda t: jnp.einsum('lgcp,gh->lgphc', t, eye_g).reshape(depth, S5_STATE, GROUP_W)
    bblk = jnp.concatenate([blk_b(bb_r), blk_b(bb_i)], axis=-1)
    cblk = jnp.concatenate([blk_c(s5_c_re.astype(F32)), -blk_c(s5_c_im.astype(F32))], axis=1)
    lam = jnp.stack([lb_r.reshape(depth, S5_STATE), lb_i.reshape(depth, S5_STATE)], axis=1)

    sm = jax.nn.softmax(hg_lb_logits.astype(F32), axis=0)
    lb = jnp.cumsum(sm, axis=0) - sm[0:1]
    lb_floor = jnp.maximum(lb, LB_FLOOR)
    a = -jnp.exp(ssd_a_log.astype(F32))
    rep64 = lambda t: jnp.repeat(t, 64, axis=-1)
    zeros256 = jnp.zeros((depth, GROUP_W), F32)
    v256 = jnp.stack([s5_d, s5_b_glu, rep64(ssd_d), ssd_norm_g, lb_floor, 1.0 - lb, hg_norm_g, rep64(a),
                      jnp.tile(gla_norm_g, (1, GLA_HEADS))] + [zeros256] * 7, axis=1).astype(F32)
    pad128 = lambda t: jnp.pad(t, ((0, 0), (0, 128 - t.shape[-1])))
    zeros128 = jnp.zeros((depth, 128), F32)
    v128 = jnp.stack([pad128(ssd_dt_bias), pad128(a), gla_b_gk] + [zeros128] * 5, axis=1).astype(F32)
    v512 = jnp.concatenate([ssd_conv_w, ssd_conv_b[:, None, :], jnp.zeros((depth, 3, SSD_XBC), F32)],
                           axis=1).astype(F32)
    w2p = jnp.pad(gla_w_gk2, ((0, 0), (LR_LO, 128 - LR_LO - GLA_RANK), (0, 0))).astype(BF16)

    col = jnp.concatenate([
        norm_mix_g, lam[:, 0], lam[:, 1], s5_d, s5_b_glu, ssd_conv_w.reshape(depth, -1), ssd_conv_b,
        jnp.pad(ssd_dt_bias, ((0, 0), (0, 8 - SSD_HEADS))), jnp.pad(a, ((0, 0), (0, 8 - SSD_HEADS))),
        rep64(ssd_d), ssd_norm_g, lb_floor, 1.0 - lb, hg_norm_g, gla_b_gk, jnp.tile(gla_norm_g, (1, GLA_HEADS)),
    ], axis=-1).astype(F32)
    tr = lambda t: t.transpose(0, 2, 1)
    bblk16, cblk16, wglu16 = bblk.astype(BF16), cblk.astype(BF16), s5_w_glu.astype(BF16)
    return dict(
        gmix=norm_mix_g[:, None, :].astype(F32), win=win, bblk=bblk16, lam=lam, cblk=cblk16, wglu=wglu16,
        v256=v256, v128=v128, v512=v512, w2p=w2p,
        col=jnp.broadcast_to(col[..., None], col.shape + (nbatch_sample,)),
        winT=tr(win), bblkT=tr(bblk16), cblkT=tr(cblk16), wgluT=tr(wglu16), w2pT=tr(w2p),
        ehd=jnp.asarray(_block_ones(128, 1, 256, 64) * (np.arange(128)[:, None] < SSD_HEADS), BF16),
        j256=jnp.asarray(_block_ones(256, 64, 256, 64), BF16),
        j128=jnp.asarray(_block_ones(128, 32, 256, 64), BF16),
        jn=jnp.asarray(_block_ones(256, 64, 256, 64, 1.0 / 64), BF16),
    )


def _diag_state(st, dk):
    nb = st.shape[0]
    s = st.reshape(nb, N_HEADS, 64, N_HEADS, dk)
    idx = jnp.arange(N_HEADS)
    s = s[:, idx, :, idx, :]
    return s.transpose(1, 0, 3, 2)


def kernel(x_prompt, x_sample, state_s5_re, state_s5_im, state_ssd_conv, state_ssd, state_hgrn, state_gla,
           norm_mix_g, w_in, s5_lam_re, s5_lam_im, s5_log_dt, s5_b_re, s5_b_im, s5_c_re, s5_c_im,
           s5_d, s5_w_glu, s5_b_glu, ssd_conv_w, ssd_conv_b, ssd_dt_bias, ssd_a_log, ssd_d, ssd_norm_g,
           hg_lb_logits, hg_norm_g, gla_w_gk2, gla_b_gk, gla_norm_g, w_out, norm_mlp_g, w_up, w_down,
           norm_final_g):
    nb, seq, _ = x_prompt.shape
    ns = x_sample.shape[0]
    depth = w_in.shape[0]
    t_steps = min(64, seq)

    p = _prepare(norm_mix_g, w_in, s5_lam_re, s5_lam_im, s5_log_dt, s5_b_re, s5_b_im, s5_c_re, s5_c_im,
                 s5_d, s5_w_glu, s5_b_glu, ssd_conv_w, ssd_conv_b, ssd_dt_bias, ssd_a_log, ssd_d,
                 ssd_norm_g, hg_lb_logits, hg_norm_g, gla_w_gk2, gla_b_gk, gla_norm_g, ns)
    p.update(wout=w_out.astype(BF16), wup=w_up.astype(BF16), wdn=w_down.astype(BF16),
             gmlp=norm_mlp_g[:, None, :].astype(F32), gfin=norm_final_g[None, :].astype(F32))

    xp = x_prompt.astype(F32).transpose(1, 0, 2).reshape(seq * nb, D_MODEL)
    xs = x_sample.astype(F32).reshape(ns, D_MODEL)

    s5T = jnp.stack([state_s5_re.reshape(depth, ns, S5_STATE), state_s5_im.reshape(depth, ns, S5_STATE)],
                    axis=1).transpose(0, 1, 3, 2).astype(F32)
    convT = state_ssd_conv.transpose(0, 2, 3, 1).reshape(depth, (SSD_CONV - 1) * SSD_XBC, ns).astype(F32)
    ssdT = state_ssd.transpose(0, 2, 3, 4, 1).reshape(depth, SSD_HEADS * SSD_N * SSD_P, ns).astype(F32)
    hgT = state_hgrn.transpose(0, 2, 3, 4, 1).reshape(depth, HG_HEADS * HG_K * HG_V, ns).astype(F32)
    glaT = state_gla.transpose(0, 2, 3, 4, 1).reshape(depth, GLA_HEADS * GLA_K * GLA_V, ns).astype(F32)

    outs_p = [[] for _ in range(6)]
    outs_s = [[] for _ in range(5)]
    for l in range(depth):
        final = l == depth - 1
        mix, s5, conv, st_ssd, st_hg, st_gla = _mixer_prompt(xp, nb, t_steps, p, l)
        xp = _mlp(mix, xp, p, l, final, 512)
        for dst, val in zip(outs_p, (s5[0], s5[1], conv, st_ssd, st_hg, st_gla)):
            dst.append(val)

        res = _mixer_sample(xs.T, p, l, s5T, convT, ssdT, hgT, glaT)
        xs = _mlp(res[0].T.astype(BF16), xs, p, l, final, 512)
        for dst, val in zip(outs_s, res[1:]):
            dst.append(val)

    y_prompt = xp.reshape(seq, nb, D_MODEL).transpose(1, 0, 2)
    y_sample = xs.reshape(ns, 1, D_MODEL)
    ps5r, ps5i, pconv, pssd, phg, pgla = [jnp.stack(v, axis=0) for v in outs_p]
    ss5, sconv, sssd, shg, sgla = [jnp.stack(v, axis=0) for v in outs_s]
    diag = lambda st, dk: _diag_state(st.reshape((depth * nb,) + st.shape[2:]), dk).reshape(
        depth, nb, N_HEADS, dk, 64)
    return (
        y_prompt, y_sample,
        ps5r.reshape(depth, nb, S5_GROUPS, S5_P), ps5i.reshape(depth, nb, S5_GROUPS, S5_P),
        pconv.reshape(depth, SSD_CONV - 1, nb, SSD_XBC).transpose(0, 2, 1, 3),
        diag(pssd, SSD_N), diag(phg, HG_K), diag(pgla, GLA_K),
        ss5[:, 0].transpose(0, 2, 1).reshape(depth, ns, S5_GROUPS, S5_P),
        ss5[:, 1].transpose(0, 2, 1).reshape(depth, ns, S5_GROUPS, S5_P),
        sconv.reshape(depth, SSD_CONV - 1, SSD_XBC, ns).transpose(0, 3, 1, 2),
        sssd.reshape(depth, SSD_HEADS, SSD_N, SSD_P, ns).transpose(0, 4, 1, 2, 3),
        shg.reshape(depth, HG_HEADS, HG_K, HG_V, ns).transpose(0, 4, 1, 2, 3),
        sgla.reshape(depth, GLA_HEADS, GLA_K, GLA_V, ns).transpose(0, 4, 1, 2, 3),
    )
```

```python
import functools
import math

import numpy as np
import jax
import jax.numpy as jnp
from jax import lax
from jax.experimental import pallas as pl
from jax.experimental.pallas import tpu as pltpu

F32 = jnp.float32
BF16 = jnp.bfloat16

D_MODEL = 1024
GROUP_W = 256
S5_GROUPS, S5_CH, S5_P = 16, 16, 64
S5_STATE = S5_GROUPS * S5_P
SSD_HEADS, SSD_N, SSD_P, SSD_NGROUPS, SSD_CONV, SSD_XBC = 4, 64, 64, 2, 4, 512
HG_HEADS, HG_K, HG_V = 4, 64, 64
GLA_HEADS, GLA_K, GLA_V, GLA_RANK, GLA_TAU = 4, 32, 64, 16, 16.0
N_HEADS = 4
D_FF = 4096
EPS = 1e-6
LB_FLOOR = 1e-30

O_U, O_Z, O_XBC = 0, 256, 512
O_HQ, O_HF, O_HI, O_HGATE = 1024, 1280, 1536, 1792
O_GQ, O_GK, O_GV, O_GGATE = 2048, 2176, 2304, 2560
O_MISC = 2816
N_PROJ = 2944
LR_LO = SSD_HEADS

SUB = 16
NEG = -1e30
LOG2E = 1.4426950408889634
VMEM_LIMIT = 56 * 1024 * 1024


def _sigmoid(x):
    return 0.5 * (1.0 + jnp.tanh(0.5 * x))


def _sigmoid_rel(x):
    return 1.0 / (1.0 + jnp.exp(-x))


def _silu(x):
    return x * _sigmoid(x)


def _softplus(x):
    return jnp.maximum(x, 0.0) + jnp.log1p(jnp.exp(-jnp.abs(x)))


def _log_sigmoid(x):
    return -_softplus(-x)


def _gelu_tanh(x):
    c = math.sqrt(2.0 / math.pi)
    return x * (0.5 * (1.0 + jnp.tanh(c * (x + 0.044715 * (x * x * x)))))


def _rms_scale(x, axis):
    return lax.rsqrt(jnp.mean(x * x, axis=axis, keepdims=True) + EPS)


def _dot(a, b):
    return jnp.dot(a, b, preferred_element_type=F32)


def _dot_nt(a, b):
    return lax.dot_general(a, b, (((1,), (1,)), ((), ())), preferred_element_type=F32)


def _dot_tn(a, b):
    return lax.dot_general(a, b, (((0,), (0,)), ((), ())), preferred_element_type=F32)


def _dot_split(a, b16, terms):
    acc, rest = None, a
    for _ in range(terms):
        piece = rest.astype(BF16)
        part = _dot(piece, b16)
        acc = part if acc is None else acc + part
        rest = rest - piece.astype(F32)
    return acc


def _idiv(x, n):
    shift = int(math.log2(n))
    assert 1 << shift == n
    return x >> shift


def _store_tiles(ref, row0, val):
    rows = val.shape[0]
    for j in range(val.shape[1] // 128):
        ref[j, row0:row0 + rows, :] = val[:, j * 128:(j + 1) * 128]


def _load_tiles(ref, row0, rows, n_tiles):
    return jnp.concatenate([ref[j, row0:row0 + rows, :] for j in range(n_tiles)], axis=1)


def _load_seq(ref, start, t_steps, nb, n_tiles):
    tiles = [ref[j, pl.ds(start, t_steps, stride=nb), :] for j in range(n_tiles)]
    return tiles[0] if n_tiles == 1 else jnp.concatenate(tiles, axis=1)


def _load_rows(ref, start, rows, n_tiles):
    tiles = [ref[j, pl.ds(start, rows), :] for j in range(n_tiles)]
    return tiles[0] if n_tiles == 1 else jnp.concatenate(tiles, axis=1)


def _gated_scan_tile(kd_lanes, dk, nb, t_steps, qf_ref, kp_ref, cp_ref, vp_ref, ob_ref, os_ref, st_ref,
                     s16_ref, j_ref, ko_ref, vo_ref):
    rows = nb * t_steps
    sub_rows = SUB * nb
    n_sub = rows // sub_rows
    n_kt = kd_lanes // 128

    def sub_body(sb):
        r0 = pl.multiple_of(sb * sub_rows, sub_rows)
        accs = [[None] * SUB for _ in range(2)]
        for j in range(SUB):
            reps = SUB - j
            rj = pl.multiple_of(r0 + nb * j, nb)
            q = _load_rows(qf_ref, rj, reps * nb, n_kt)
            c = _load_rows(cp_ref, rj, reps * nb, n_kt)
            k_j = jnp.tile(_load_rows(kp_ref, rj, nb, n_kt), (reps, 1))
            c_j = jnp.tile(_load_rows(cp_ref, rj, nb, n_kt), (reps, 1))
            w = _dot((q * k_j * jnp.exp2(c - c_j)).astype(BF16), j_ref[...])
            for n in range(2):
                v_j = vp_ref[n, pl.ds(rj, nb), :]
                for i in range(j, SUB):
                    term = w[(i - j) * nb:(i - j + 1) * nb, n * 128:(n + 1) * 128] * v_j
                    accs[n][i] = term if accs[n][i] is None else accs[n][i] + term
            yield
        for n in range(2):
            ob_ref[n, pl.ds(r0, sub_rows), :] = jnp.concatenate(accs[n], axis=0)

    t_idx = lax.broadcasted_iota(jnp.int32, (t_steps, kd_lanes), 0)
    ri = lax.broadcasted_iota(jnp.int32, (t_steps, N_HEADS * t_steps), 0)
    ci = lax.broadcasted_iota(jnp.int32, (t_steps, N_HEADS * t_steps), 1) & (t_steps - 1)
    lane_tile_w = min(kd_lanes, 128)
    heads_per_tile = lane_tile_w // dk
    tile_head = _idiv(lax.broadcasted_iota(jnp.int32, (64, lane_tile_w), 1), dk)
    levels = []
    c_sz = SUB
    while c_sz < t_steps:
        levels.append(c_sz)
        c_sz *= 2

    def seq_body(b):
        qb = _load_seq(qf_ref, b, t_steps, nb, n_kt)
        kb = _load_seq(kp_ref, b, t_steps, nb, n_kt)
        cb = _load_seq(cp_ref, b, t_steps, nb, n_kt)
        vb16 = _load_seq(vp_ref, b, t_steps, nb, 2).astype(BF16)
        c_last = cb[t_steps - 1:t_steps, :]

        o = _dot_nt((qb * jnp.exp2(cb)).astype(BF16), s16_ref[b])
        yield
        s = None
        for c_sz in levels:
            pieces = []
            for m in range(t_steps // (2 * c_sz)):
                a_row = m * 2 * c_sz + c_sz - 1
                pieces.append(jnp.broadcast_to(cb[a_row:a_row + 1, :], (2 * c_sz, kd_lanes)))
            anchor = pieces[0] if len(pieces) == 1 else jnp.concatenate(pieces, axis=0)
            upper = (t_idx & c_sz) != 0
            q_l = (qb * jnp.exp2(jnp.where(upper, cb - anchor, NEG))).astype(BF16)
            k_l = (kb * jnp.exp2(jnp.where(upper, NEG, anchor - cb))).astype(BF16)
            s_l = _dot_nt(q_l, jnp.tile(k_l, (N_HEADS, 1)) * ko_ref[...])
            if 2 * c_sz < t_steps:
                shift = int(math.log2(2 * c_sz))
                s_l = jnp.where((ri >> shift) == (ci >> shift), s_l, 0.0)
            s = s_l if s is None else s + s_l
            yield
        if s is not None:
            o = o + _dot(s.astype(BF16), jnp.tile(vb16, (N_HEADS, 1)) * vo_ref[...])
        for j in range(2):
            os_ref[j, pl.ds(b, t_steps, stride=nb), :] = o[:, j * 128:(j + 1) * 128]
        yield

        upd = _dot_tn(vb16, (kb * jnp.exp2(c_last - cb)).astype(BF16))
        decay = jnp.exp2(c_last)
        for h in range(N_HEADS):
            r0, l0 = h * 64, (h // heads_per_tile) * lane_tile_w
            own = tile_head == (h % heads_per_tile)
            blk = (decay[:, l0:l0 + lane_tile_w] * st_ref[b, r0:r0 + 64, l0:l0 + lane_tile_w]
                   + jnp.where(own, upd[r0:r0 + 64, l0:l0 + lane_tile_w], 0.0))
            st_ref[b, r0:r0 + 64, l0:l0 + lane_tile_w] = blk
            s16_ref[b, r0:r0 + 64, l0:l0 + lane_tile_w] = blk.astype(BF16)

    assert nb % n_sub == 0
    seq_per_trip = nb // n_sub

    def trip(i, carry):
        sub = sub_body(i)
        seqs = [seq_body(i * seq_per_trip + s) for s in range(seq_per_trip)]
        turn = 0
        while sub is not None or seqs:
            if sub is not None and next(sub, "done") == "done":
                sub = None
            if seqs:
                turn %= len(seqs)
                if next(seqs[turn], "done") == "done":
                    seqs.pop(turn)
                else:
                    turn += 1
        return carry

    lax.fori_loop(0, n_sub, trip, 0, unroll=2)


def _cumsum_time(nb, t_steps, src_ref, cp_ref, kd_lanes):
    def body(t, c):
        r = pl.multiple_of(t * nb, nb)
        c = c + src_ref[pl.ds(r, nb), :kd_lanes] * LOG2E
        for j in range(kd_lanes // 128):
            cp_ref[j, pl.ds(r, nb), :] = c[:, j * 128:(j + 1) * 128]
        return c

    lax.fori_loop(0, t_steps, body, jnp.zeros((nb, kd_lanes), F32), unroll=4)


def _head_rms_scale(o, jn_ref):
    return lax.rsqrt(_dot_split(o * o, jn_ref[...], 2) + EPS)


def _mixer_prompt_kernel(nb, t_steps,
                         x_ref, gmix_ref, wint_ref, bblk_ref, lam_ref, cblk_ref, wglu_ref,
                         v256_ref, v128_ref, v512_ref, w2p_ref, ehd_ref, j256_ref, j128_ref, jn_ref,
                         ko256_ref, ko128_ref,
                         mix_ref, s5_ref, conv_ref, st_ssd_ref, st_hg_ref, st_gla_ref,
                         proj_ref, bu_ref, xe_ref, qf_ref, kp_ref, cp_ref, vp_ref, ob_ref, os_ref, la_ref,
                         s16_ssd_ref, s16_hg_ref, s16_gla_ref, hn_ref):
    rows = nb * t_steps
    conv_rows = (SSD_CONV - 1) * nb

    @pl.when(pl.program_id(0) == 0)
    def _():
        for ref in (s5_ref, st_ssd_ref, st_hg_ref, st_gla_ref, s16_ssd_ref, s16_hg_ref, s16_gla_ref):
            ref[...] = jnp.zeros_like(ref)
        xe_ref[0:conv_rows, :] = jnp.zeros((conv_rows, SSD_XBC), F32)

    x = x_ref[...]
    hn_ref[...] = (x * _rms_scale(x, -1) * gmix_ref[...]).astype(BF16)

    def project(lo, hi):
        proj_ref[:, lo:hi] = _dot_nt(hn_ref[...], wint_ref[lo:hi, :])

    project(O_U, O_HQ)
    project(O_MISC, N_PROJ)

    u = proj_ref[:, O_U:O_U + GROUP_W]
    bu_ref[...] = _dot(u.astype(BF16), bblk_ref[...])
    lam_r = jnp.broadcast_to(lam_ref[0:1, :], (nb, S5_STATE))
    lam_i = jnp.broadcast_to(lam_ref[1:2, :], (nb, S5_STATE))

    def s5_step(t, h):
        h_r, h_i = h
        r = pl.multiple_of(t * nb, nb)
        n_r = lam_r * h_r - lam_i * h_i + bu_ref[pl.ds(r, nb), 0:S5_STATE]
        n_i = lam_r * h_i + lam_i * h_r + bu_ref[pl.ds(r, nb), S5_STATE:2 * S5_STATE]
        bu_ref[pl.ds(r, nb), 0:S5_STATE] = n_r
        bu_ref[pl.ds(r, nb), S5_STATE:2 * S5_STATE] = n_i
        return n_r, n_i

    h_r, h_i = lax.fori_loop(0, t_steps, s5_step, (s5_ref[0], s5_ref[1]), unroll=2)
    s5_ref[0] = h_r
    s5_ref[1] = h_i
    project(O_HQ, O_GQ)
    y = _dot(bu_ref[...].astype(BF16), cblk_ref[...]) + v256_ref[0:1, :] * u
    z = _gelu_tanh(y)
    o_s5 = z * _sigmoid(_dot(z.astype(BF16), wglu_ref[...]) + v256_ref[1:2, :])
    mix_ref[:, 0:GROUP_W] = o_s5.astype(mix_ref.dtype)

    xe_ref[conv_rows:conv_rows + rows, :] = proj_ref[:, O_XBC:O_XBC + SSD_XBC]
    conv = v512_ref[SSD_CONV:SSD_CONV + 1, :]
    for j in range(SSD_CONV):
        conv = conv + xe_ref[j * nb:j * nb + rows, :] * v512_ref[j:j + 1, :]
    tail = xe_ref[rows:rows + conv_rows, :]
    xe_ref[0:conv_rows, :] = tail
    conv_ref[...] = tail
    act = _silu(conv)
    xs = act[:, 0:GROUP_W]
    b_g = act[:, GROUP_W:GROUP_W + 128]
    c_g = act[:, GROUP_W + 128:GROUP_W + 256]
    misc = proj_ref[:, O_MISC:O_MISC + 128]
    dt = _softplus(misc + v128_ref[0:1, :])
    dt_hd = _dot_split(dt, ehd_ref[...], 3)
    la_ref[...] = dt_hd * v256_ref[7:8, :]
    _store_tiles(qf_ref, 0, jnp.concatenate(
        [c_g[:, 0:64], c_g[:, 0:64], c_g[:, 64:128], c_g[:, 64:128]], axis=1))
    _store_tiles(kp_ref, 0, jnp.concatenate(
        [b_g[:, 0:64], b_g[:, 0:64], b_g[:, 64:128], b_g[:, 64:128]], axis=1))
    _store_tiles(vp_ref, 0, xs * dt_hd)
    _cumsum_time(nb, t_steps, la_ref, cp_ref, 256)
    _gated_scan_tile(256, 64, nb, t_steps, qf_ref, kp_ref, cp_ref, vp_ref, ob_ref, os_ref, st_ssd_ref,
                     s16_ssd_ref, j256_ref, ko256_ref, ko256_ref)
    project(O_GQ, O_MISC)
    y = _load_tiles(ob_ref, 0, rows, 2) + _load_tiles(os_ref, 0, rows, 2) + v256_ref[2:3, :] * xs
    y = y * _silu(proj_ref[:, O_Z:O_Z + GROUP_W])
    y = y * _rms_scale(y, -1) * v256_ref[3:4, :]
    mix_ref[:, GROUP_W:2 * GROUP_W] = y.astype(mix_ref.dtype)

    sig_f = _sigmoid_rel(proj_ref[:, O_HF:O_HF + GROUP_W])
    nsig_f = 1.0 - sig_f
    la_ref[...] = jnp.log(sig_f + v256_ref[4:5, :] * nsig_f)
    _store_tiles(qf_ref, 0, _silu(proj_ref[:, O_HQ:O_HQ + GROUP_W]))
    _store_tiles(kp_ref, 0, v256_ref[5:6, :] * nsig_f)
    _store_tiles(vp_ref, 0, proj_ref[:, O_HI:O_HI + GROUP_W])
    _cumsum_time(nb, t_steps, la_ref, cp_ref, 256)
    _gated_scan_tile(256, 64, nb, t_steps, qf_ref, kp_ref, cp_ref, vp_ref, ob_ref, os_ref, st_hg_ref,
                     s16_hg_ref, j256_ref, ko256_ref, ko256_ref)
    o = _load_tiles(ob_ref, 0, rows, 2) + _load_tiles(os_ref, 0, rows, 2)
    o = o * _head_rms_scale(o, jn_ref) * v256_ref[6:7, :]
    o = o * _silu(proj_ref[:, O_HGATE:O_HGATE + GROUP_W])
    mix_ref[:, 2 * GROUP_W:3 * GROUP_W] = o.astype(mix_ref.dtype)

    gk = _dot(misc.astype(BF16), w2p_ref[...]) + v128_ref[2:3, :]
    la_ref[:, 0:128] = _log_sigmoid(gk) * (1.0 / GLA_TAU)
    _store_tiles(qf_ref, 0, proj_ref[:, O_GQ:O_GQ + 128] * (GLA_K ** -0.5))
    _store_tiles(kp_ref, 0, proj_ref[:, O_GK:O_GK + 128])
    _store_tiles(vp_ref, 0, proj_ref[:, O_GV:O_GV + GROUP_W])
    _cumsum_time(nb, t_steps, la_ref, cp_ref, 128)
    _gated_scan_tile(128, 32, nb, t_steps, qf_ref, kp_ref, cp_ref, vp_ref, ob_ref, os_ref, st_gla_ref,
                     s16_gla_ref, j128_ref, ko128_ref, ko256_ref)
    o = _load_tiles(ob_ref, 0, rows, 2) + _load_tiles(os_ref, 0, rows, 2)
    o = o * _head_rms_scale(o, jn_ref) * v256_ref[8:9, :]
    o = o * _silu(proj_ref[:, O_GGATE:O_GGATE + GROUP_W])
    mix_ref[:, 3 * GROUP_W:4 * GROUP_W] = o.astype(mix_ref.dtype)


def _const_spec(shape):
    nd = len(shape)
    return pl.BlockSpec(shape, lambda i, _nd=nd: (0,) * _nd, pipeline_mode=pl.Buffered(1))


def _layer_spec(arr, layer):
    nd = arr.ndim - 1
    return pl.BlockSpec((None,) + arr.shape[1:], lambda i, _nd=nd: (layer,) + (0,) * _nd,
                        pipeline_mode=pl.Buffered(1))


PROMPT_LAYER_PARAMS = ('gmix', 'winT', 'bblk', 'lam', 'cblk', 'wglu', 'v256', 'v128', 'v512', 'w2p')
PROMPT_SHARED_PARAMS = ('ehd', 'j256', 'j128', 'jn')


def _mixer_prompt(x_tm, nb, t_steps, p, layer):
    rows_total = x_tm.shape[0]
    rows = nb * t_steps
    n_tiles = rows_total // rows
    layered = [p[k] for k in PROMPT_LAYER_PARAMS]
    shared = [p[k] for k in PROMPT_SHARED_PARAMS] + [
        jnp.asarray(_block_ones(N_HEADS * t_steps, t_steps, 256, 64), BF16),
        jnp.asarray(_block_ones(N_HEADS * t_steps, t_steps, 128, 32), BF16)]
    consts = layered + shared
    in_specs = ([pl.BlockSpec((rows, D_MODEL), lambda i: (i, 0))]
                + [_layer_spec(c, layer) for c in layered] + [_const_spec(c.shape) for c in shared])
    conv_rows = (SSD_CONV - 1) * nb
    out_shape = (
        jax.ShapeDtypeStruct((rows_total, D_MODEL), BF16),
        jax.ShapeDtypeStruct((2, nb, S5_STATE), F32),
        jax.ShapeDtypeStruct((conv_rows, SSD_XBC), F32),
        jax.ShapeDtypeStruct((nb, 256, 256), F32),
        jax.ShapeDtypeStruct((nb, 256, 256), F32),
        jax.ShapeDtypeStruct((nb, 256, 128), F32),
    )
    out_specs = (
        pl.BlockSpec((rows, D_MODEL), lambda i: (i, 0)),
        pl.BlockSpec((2, nb, S5_STATE), lambda i: (0, 0, 0)),
        pl.BlockSpec((conv_rows, SSD_XBC), lambda i: (0, 0)),
        pl.BlockSpec((nb, 256, 256), lambda i: (0, 0, 0)),
        pl.BlockSpec((nb, 256, 256), lambda i: (0, 0, 0)),
        pl.BlockSpec((nb, 256, 128), lambda i: (0, 0, 0)),
    )
    scratch = [
        pltpu.VMEM((rows, N_PROJ), F32),
        pltpu.VMEM((rows, 2 * S5_STATE), F32),
        pltpu.VMEM((conv_rows + rows, SSD_XBC), F32),
        pltpu.VMEM((2, rows, 128), F32),
        pltpu.VMEM((2, rows, 128), F32),
        pltpu.VMEM((2, rows, 128), F32),
        pltpu.VMEM((2, rows, 128), F32),
        pltpu.VMEM((2, rows, 128), F32),
        pltpu.VMEM((2, rows, 128), F32),
        pltpu.VMEM((rows, 256), F32),
        pltpu.VMEM((nb, 256, 256), BF16),
        pltpu.VMEM((nb, 256, 256), BF16),
        pltpu.VMEM((nb, 256, 128), BF16),
        pltpu.VMEM((rows, D_MODEL), BF16),
    ]
    return pl.pallas_call(
        functools.partial(_mixer_prompt_kernel, nb, t_steps),
        grid=(n_tiles,),
        in_specs=in_specs,
        out_specs=out_specs,
        out_shape=out_shape,
        scratch_shapes=scratch,
        compiler_params=pltpu.CompilerParams(dimension_semantics=("arbitrary",),
                                             vmem_limit_bytes=VMEM_LIMIT),
        name="mixer_prompt",
    )(x_tm, *consts)


SAMPLE_LAYER_PARAMS = ('gmix', 'winT', 'bblk', 'lam', 'cblk', 'wglu', 'v256', 'v128', 'v512', 'w2p')


def _mixer_sample_kernel(x_ref, gmix_ref, wint_ref, bblk_ref, lam_ref, cblk_ref, wglu_ref,
                         v256_ref, v128_ref, v512_ref, w2p_ref, jn_ref,
                         s5_in, conv_in, ssd_in, hg_in, gla_in,
                         mix_ref, s5_out, conv_out, ssd_out, hg_out, gla_out,
                         proj_ref, xs_ref, act_ref, dt_ref, es_ref, qh_ref, kh_ref, eh_ref, vh_ref,
                         qg_ref, kg_ref, eg_ref, vg_ref, os_ref, oh_ref, og_ref):
    h = pl.program_id(0)

    @pl.when(h == 0)
    def _():
        x = x_ref[...]
        hn = (x * _rms_scale(x, -1) * gmix_ref[...]).astype(BF16)
        proj_ref[...] = _dot_nt(hn, wint_ref[...])

        u = proj_ref[:, O_U:O_U + GROUP_W]
        bu = _dot(u.astype(BF16), bblk_ref[...])
        lam_r, lam_i = lam_ref[0:1, :], lam_ref[1:2, :]
        h0_r, h0_i = s5_in[0], s5_in[1]
        h_r = lam_r * h0_r - lam_i * h0_i + bu[:, 0:S5_STATE]
        h_i = lam_r * h0_i + lam_i * h0_r + bu[:, S5_STATE:2 * S5_STATE]
        s5_out[0] = h_r
        s5_out[1] = h_i
        y = (_dot(jnp.concatenate([h_r, h_i], axis=1).astype(BF16), cblk_ref[...])
             + v256_ref[0:1, :] * u)
        z = _gelu_tanh(y)
        o_s5 = z * _sigmoid(_dot(z.astype(BF16), wglu_ref[...]) + v256_ref[1:2, :])
        mix_ref[:, 0:GROUP_W] = o_s5.astype(mix_ref.dtype)

        xbc = proj_ref[:, O_XBC:O_XBC + SSD_XBC]
        conv = v512_ref[SSD_CONV:SSD_CONV + 1, :] + v512_ref[SSD_CONV - 1:SSD_CONV, :] * xbc
        for j in range(SSD_CONV - 1):
            conv = conv + v512_ref[j:j + 1, :] * conv_in[:, j * SSD_XBC:(j + 1) * SSD_XBC]
        conv_out[:, 0:2 * SSD_XBC] = conv_in[:, SSD_XBC:3 * SSD_XBC]
        conv_out[:, 2 * SSD_XBC:3 * SSD_XBC] = xbc
        act = _silu(conv)
        xs_ref[...] = act[:, 0:GROUP_W]
        act_ref[...] = act.T
        misc = proj_ref[:, O_MISC:O_MISC + 128]
        dt = _softplus(misc + v128_ref[0:1, :])
        dt_ref[...] = dt.T
        es_ref[...] = jnp.exp(dt * v128_ref[1:2, :]).T

        sig_f = _sigmoid_rel(proj_ref[:, O_HF:O_HF + GROUP_W])
        nsig_f = 1.0 - sig_f
        eh_ref[...] = (sig_f + v256_ref[4:5, :] * nsig_f).T
        kh_ref[...] = (v256_ref[5:6, :] * nsig_f).T
        qh_ref[...] = _silu(proj_ref[:, O_HQ:O_HQ + GROUP_W]).T
        vh_ref[...] = proj_ref[:, O_HI:O_HI + GROUP_W].T

        gk = _dot(misc.astype(BF16), w2p_ref[...]) + v128_ref[2:3, :]
        eg_ref[...] = jnp.exp(_log_sigmoid(gk) * (1.0 / GLA_TAU)).T
        kg_ref[...] = proj_ref[:, O_GK:O_GK + 128].T
        qg_ref[...] = (proj_ref[:, O_GQ:O_GQ + 128] * (GLA_K ** -0.5)).T
        vg_ref[...] = proj_ref[:, O_GV:O_GV + GROUP_W].T

    def state_step(st_in, st_out, e_row, k_row, q_row, v, n_keys):
        def body(i, acc):
            r = pl.multiple_of(i * 64, 64)
            s_new = e_row(i) * st_in[pl.ds(r, 64), :] + k_row(i) * v
            st_out[pl.ds(r, 64), :] = s_new
            return acc + q_row(i) * s_new
        return lax.fori_loop(0, n_keys, body, jnp.zeros_like(v), unroll=4)

    def head_rows():
        return pl.ds(pl.multiple_of(h * 64, 64), 64)

    g = _idiv(h, SSD_HEADS // SSD_NGROUPS)
    e_h = es_ref[pl.ds(h, 1), :]
    os_ref[head_rows(), :] = state_step(
        ssd_in, ssd_out,
        lambda i: e_h,
        lambda i: act_ref[pl.ds(GROUP_W + g * SSD_N + i, 1), :],
        lambda i: act_ref[pl.ds(GROUP_W + 128 + g * SSD_N + i, 1), :],
        act_ref[head_rows(), :] * dt_ref[pl.ds(h, 1), :], SSD_N)

    oh_ref[head_rows(), :] = state_step(
        hg_in, hg_out,
        lambda i: eh_ref[pl.ds(h * HG_K + i, 1), :],
        lambda i: kh_ref[pl.ds(h * HG_K + i, 1), :],
        lambda i: qh_ref[pl.ds(h * HG_K + i, 1), :],
        vh_ref[head_rows(), :], HG_K)

    og_ref[head_rows(), :] = state_step(
        gla_in, gla_out,
        lambda i: eg_ref[pl.ds(h * GLA_K + i, 1), :],
        lambda i: kg_ref[pl.ds(h * GLA_K + i, 1), :],
        lambda i: qg_ref[pl.ds(h * GLA_K + i, 1), :],
        vg_ref[head_rows(), :], GLA_K)

    @pl.when(h == N_HEADS - 1)
    def _():
        y = os_ref[...].T + v256_ref[2:3, :] * xs_ref[...]
        y = y * _silu(proj_ref[:, O_Z:O_Z + GROUP_W])
        mix_ref[:, GROUP_W:2 * GROUP_W] = (y * _rms_scale(y, -1) * v256_ref[3:4, :]).astype(mix_ref.dtype)
        o = oh_ref[...].T
        o = o * _head_rms_scale(o, jn_ref) * v256_ref[6:7, :] * _silu(proj_ref[:, O_HGATE:O_HGATE + GROUP_W])
        mix_ref[:, 2 * GROUP_W:3 * GROUP_W] = o.astype(mix_ref.dtype)
        o = og_ref[...].T
        o = o * _head_rms_scale(o, jn_ref) * v256_ref[8:9, :] * _silu(proj_ref[:, O_GGATE:O_GGATE + GROUP_W])
        mix_ref[:, 3 * GROUP_W:4 * GROUP_W] = o.astype(mix_ref.dtype)


def _mixer_sample(x, p, layer, s5, conv, ssd_t, hg_t, gla_t):
    nbatch = x.shape[0]
    layered = [p[k] for k in SAMPLE_LAYER_PARAMS]
    head_in = lambda a: pl.BlockSpec((None, a.shape[1] // N_HEADS, nbatch), lambda i: (layer, i, 0))
    head_blk = lambda a: pl.BlockSpec((a.shape[1] // N_HEADS, nbatch), lambda i: (i, 0))
    res_blk = lambda shape: pl.BlockSpec(shape, lambda i, _nd=len(shape): (0,) * _nd)
    out_shape = (
        jax.ShapeDtypeStruct((nbatch, D_MODEL), BF16),
        jax.ShapeDtypeStruct(s5.shape[1:], F32),
        jax.ShapeDtypeStruct(conv.shape[1:], F32),
        jax.ShapeDtypeStruct(ssd_t.shape[1:], F32),
        jax.ShapeDtypeStruct(hg_t.shape[1:], F32),
        jax.ShapeDtypeStruct(gla_t.shape[1:], F32),
    )
    fm = lambda n: pltpu.VMEM((n, nbatch), F32)
    scratch = [
        pltpu.VMEM((nbatch, N_PROJ), F32),
        pltpu.VMEM((nbatch, GROUP_W), F32),
        fm(SSD_XBC),
        fm(128), fm(128),
        fm(GROUP_W), fm(GROUP_W), fm(GROUP_W), fm(GROUP_W),
        fm(128), fm(128), fm(128), fm(GROUP_W),
        fm(GROUP_W), fm(GROUP_W), fm(GROUP_W),
    ]
    return pl.pallas_call(
        _mixer_sample_kernel,
        grid=(N_HEADS,),
        in_specs=([_const_spec(x.shape)] + [_layer_spec(c, layer) for c in layered]
                  + [_const_spec(p['jn'].shape), _layer_spec(s5, layer), _layer_spec(conv, layer),
                     head_in(ssd_t), head_in(hg_t), head_in(gla_t)]),
        out_specs=(res_blk((nbatch, D_MODEL)), res_blk(s5.shape[1:]), res_blk(conv.shape[1:]),
                   head_blk(ssd_t), head_blk(hg_t), head_blk(gla_t)),
        out_shape=out_shape,
        scratch_shapes=scratch,
        compiler_params=pltpu.CompilerParams(dimension_semantics=("arbitrary",),
                                             vmem_limit_bytes=VMEM_LIMIT),
        name="mixer_sample",
    )(x, *layered, p['jn'], s5, conv, ssd_t, hg_t, gla_t)


FF_TILE = 512


def _mlp_kernel(final, mix_ref, x_ref, wout_ref, g_ref, wup_ref, wdn_ref, gfin_ref, o_ref):
    x1 = x_ref[...] + _dot(mix_ref[...], wout_ref[...])
    hn = (x1 * _rms_scale(x1, -1) * g_ref[...]).astype(BF16)
    acc = x1
    for f in range(D_FF // FF_TILE):
        up = jnp.maximum(_dot(hn, wup_ref[:, f * FF_TILE:(f + 1) * FF_TILE]), 0.0)
        acc = acc + _dot((up * up).astype(BF16), wdn_ref[f * FF_TILE:(f + 1) * FF_TILE, :])
    if final:
        acc = acc * _rms_scale(acc, -1) * gfin_ref[...]
    o_ref[...] = acc


def _mlp(mix, x, p, layer, final, tm):
    rows_total = x.shape[0]
    tm = min(tm, rows_total)
    layered = [p['wout'], p['gmlp'], p['wup'], p['wdn']]
    consts = layered + [p['gfin']]
    return pl.pallas_call(
        functools.partial(_mlp_kernel, final),
        grid=(rows_total // tm,),
        in_specs=[pl.BlockSpec((tm, D_MODEL), lambda i: (i, 0)),
                  pl.BlockSpec((tm, D_MODEL), lambda i: (i, 0))]
                 + [_layer_spec(c, layer) for c in layered] + [_const_spec(p['gfin'].shape)],
        out_specs=pl.BlockSpec((tm, D_MODEL), lambda i: (i, 0)),
        out_shape=jax.ShapeDtypeStruct((rows_total, D_MODEL), F32),
        compiler_params=pltpu.CompilerParams(dimension_semantics=("parallel",),
                                             vmem_limit_bytes=VMEM_LIMIT),
        name="out_mlp",
    )(mix, x, *consts)


def _block_ones(n_rows, row_blk, n_cols, col_blk, scale=1.0):
    r = np.arange(n_rows)[:, None] // row_blk
    c = np.arange(n_cols)[None, :] // col_blk
    return (r == c).astype(np.float32) * scale


def _prepare(norm_mix_g, w_in, s5_lam_re, s5_lam_im, s5_log_dt, s5_b_re, s5_b_im, s5_c_re, s5_c_im,
             s5_d, s5_w_glu, s5_b_glu, ssd_conv_w, ssd_conv_b, ssd_dt_bias, ssd_a_log, ssd_d, ssd_norm_g,
             hg_lb_logits, hg_norm_g, gla_w_gk2, gla_b_gk, gla_norm_g):
    depth = w_in.shape[0]
    w_t = w_in.transpose(0, 2, 1).astype(BF16)
    win_t = jnp.concatenate([w_t[:, 0:1024], w_t[:, 1028:2820], w_t[:, 1024:1028], w_t[:, 2820:2836],
                             jnp.zeros((depth, 128 - SSD_HEADS - GLA_RANK, D_MODEL), BF16)], axis=1)

    lam_re, lam_im = s5_lam_re.astype(F32), s5_lam_im.astype(F32)
    dt = jnp.exp(s5_log_dt.astype(F32))[..., None]
    ea = jnp.exp(lam_re * dt)
    lb_r, lb_i = ea * jnp.cos(lam_im * dt), ea * jnp.sin(lam_im * dt)
    den = lam_re * lam_re + lam_im * lam_im
    cr = ((lb_r - 1.0) * lam_re + lb_i * lam_im) / den
    ci = (lb_i * lam_re - (lb_r - 1.0) * lam_im) / den
    bb_r = cr[..., None] * s5_b_re - ci[..., None] * s5_b_im
    bb_i = cr[..., None] * s5_b_im + ci[..., None] * s5_b_re
    eye_g = jnp.eye(S5_GROUPS, dtype=F32)
    blk_b = lambda t: jnp.einsum('lgpc,gh->lgchp', t, eye_g).reshape(depth, GROUP_W, S5_STATE)
    blk_c = lambda t: jnp.einsum('lgcp,gh->lgphc', t, eye_g).reshape(depth, S5_STATE, GROUP_W)
    bblk = jnp.concatenate([blk_b(bb_r), blk_b(bb_i)], axis=-1)
    cblk = jnp.concatenate([blk_c(s5_c_re.astype(F32)), -blk_c(s5_c_im.astype(F32))], axis=1)
    lam = jnp.stack([lb_r.reshape(depth, S5_STATE), lb_i.reshape(depth, S5_STATE)], axis=1)

    sm = jax.nn.softmax(hg_lb_logits.astype(F32), axis=0)
    lb = jnp.cumsum(sm, axis=0) - sm[0:1]
    lb_floor = jnp.maximum(lb, LB_FLOOR)
    a = -jnp.exp(ssd_a_log.astype(F32))
    rep64 = lambda t: jnp.repeat(t, 64, axis=-1)
    zeros256 = jnp.zeros((depth, GROUP_W), F32)
    v256 = jnp.stack([s5_d, s5_b_glu, rep64(ssd_d), ssd_norm_g, lb_floor, 1.0 - lb, hg_norm_g, rep64(a),
                      jnp.tile(gla_norm_g, (1, GLA_HEADS))] + [zeros256] * 7, axis=1).astype(F32)
    pad128 = lambda t: jnp.pad(t, ((0, 0), (0, 128 - t.shape[-1])))
    zeros128 = jnp.zeros((depth, 128), F32)
    v128 = jnp.stack([pad128(ssd_dt_bias), pad128(a), gla_b_gk] + [zeros128] * 5, axis=1).astype(F32)
    v512 = jnp.concatenate([ssd_conv_w, ssd_conv_b[:, None, :], jnp.zeros((depth, 3, SSD_XBC), F32)],
                           axis=1).astype(F32)
    w2p = jnp.pad(gla_w_gk2, ((0, 0), (LR_LO, 128 - LR_LO - GLA_RANK), (0, 0))).astype(BF16)

    return dict(
        gmix=norm_mix_g[:, None, :].astype(F32), winT=win_t, bblk=bblk.astype(BF16), lam=lam,
        cblk=cblk.astype(BF16), wglu=s5_w_glu.astype(BF16), v256=v256, v128=v128, v512=v512, w2p=w2p,
        ehd=jnp.asarray(_block_ones(128, 1, 256, 64) * (np.arange(128)[:, None] < SSD_HEADS), BF16),
        j256=jnp.asarray(_block_ones(256, 64, 256, 64), BF16),
        j128=jnp.asarray(_block_ones(128, 32, 256, 64), BF16),
        jn=jnp.asarray(_block_ones(256, 64, 256, 64, 1.0 / 64), BF16),
    )


def _diag_state(st, dk):
    nb = st.shape[0]
    s = st.reshape(nb, N_HEADS, 64, N_HEADS, dk)
    idx = jnp.arange(N_HEADS)
    s = s[:, idx, :, idx, :]
    return s.transpose(1, 0, 3, 2)


def kernel(x_prompt, x_sample, state_s5_re, state_s5_im, state_ssd_conv, state_ssd, state_hgrn, state_gla,
           norm_mix_g, w_in, s5_lam_re, s5_lam_im, s5_log_dt, s5_b_re, s5_b_im, s5_c_re, s5_c_im,
           s5_d, s5_w_glu, s5_b_glu, ssd_conv_w, ssd_conv_b, ssd_dt_bias, ssd_a_log, ssd_d, ssd_norm_g,
           hg_lb_logits, hg_norm_g, gla_w_gk2, gla_b_gk, gla_norm_g, w_out, norm_mlp_g, w_up, w_down,
           norm_final_g):
    nb, seq, _ = x_prompt.shape
    ns = x_sample.shape[0]
    depth = w_in.shape[0]
    t_steps = min(64, seq)

    p = _prepare(norm_mix_g, w_in, s5_lam_re, s5_lam_im, s5_log_dt, s5_b_re, s5_b_im, s5_c_re, s5_c_im,
                 s5_d, s5_w_glu, s5_b_glu, ssd_conv_w, ssd_conv_b, ssd_dt_bias, ssd_a_log, ssd_d,
                 ssd_norm_g, hg_lb_logits, hg_norm_g, gla_w_gk2, gla_b_gk, gla_norm_g)
    p.update(wout=w_out.astype(BF16), wup=w_up.astype(BF16), wdn=w_down.astype(BF16),
             gmlp=norm_mlp_g[:, None, :].astype(F32), gfin=norm_final_g[None, :].astype(F32))

    xp = x_prompt.astype(F32).transpose(1, 0, 2).reshape(seq * nb, D_MODEL)
    xs = x_sample.astype(F32).reshape(ns, D_MODEL)

    s5s = jnp.stack([state_s5_re.reshape(depth, ns, S5_STATE), state_s5_im.reshape(depth, ns, S5_STATE)],
                    axis=1).astype(F32)
    convs = state_ssd_conv.reshape(depth, ns, (SSD_CONV - 1) * SSD_XBC).astype(F32)
    ssdT = state_ssd.transpose(0, 2, 3, 4, 1).reshape(depth, SSD_HEADS * SSD_N * SSD_P, ns).astype(F32)
    hgT = state_hgrn.transpose(0, 2, 3, 4, 1).reshape(depth, HG_HEADS * HG_K * HG_V, ns).astype(F32)
    glaT = state_gla.transpose(0, 2, 3, 4, 1).reshape(depth, GLA_HEADS * GLA_K * GLA_V, ns).astype(F32)

    outs_p = [[] for _ in range(6)]
    outs_s = [[] for _ in range(5)]
    for l in range(depth):
        final = l == depth - 1
        mix, s5, conv, st_ssd, st_hg, st_gla = _mixer_prompt(xp, nb, t_steps, p, l)
        xp = _mlp(mix, xp, p, l, final, 512)
        for dst, val in zip(outs_p, (s5[0], s5[1], conv, st_ssd, st_hg, st_gla)):
            dst.append(val)

        res = _mixer_sample(xs, p, l, s5s, convs, ssdT, hgT, glaT)
        xs = _mlp(res[0], xs, p, l, final, 512)
        for dst, val in zip(outs_s, res[1:]):
            dst.append(val)

    y_prompt = xp.reshape(seq, nb, D_MODEL).transpose(1, 0, 2)
    y_sample = xs.reshape(ns, 1, D_MODEL)
    ps5r, ps5i, pconv, pssd, phg, pgla = [jnp.stack(v, axis=0) for v in outs_p]
    ss5, sconv, sssd, shg, sgla = [jnp.stack(v, axis=0) for v in outs_s]
    diag = lambda st, dk: _diag_state(st.reshape((depth * nb,) + st.shape[2:]), dk).reshape(
        depth, nb, N_HEADS, dk, 64)
    return (
        y_prompt, y_sample,
        ps5r.reshape(depth, nb, S5_GROUPS, S5_P), ps5i.reshape(depth, nb, S5_GROUPS, S5_P),
        pconv.reshape(depth, SSD_CONV - 1, nb, SSD_XBC).transpose(0, 2, 1, 3),
        diag(pssd, SSD_N), diag(phg, HG_K), diag(pgla, GLA_K),
        ss5[:, 0].reshape(depth, ns, S5_GROUPS, S5_P), ss5[:, 1].reshape(depth, ns, S5_GROUPS, S5_P),
        sconv.reshape(depth, ns, SSD_CONV - 1, SSD_XBC),
        sssd.reshape(depth, SSD_HEADS, SSD_N, SSD_P, ns).transpose(0, 4, 1, 2, 3),
        shg.reshape(depth, HG_HEADS, HG_K, HG_V, ns).transpose(0, 4, 1, 2, 3),
        sgla.reshape(depth, GLA_HEADS, GLA_K, GLA_V, ns).transpose(0, 4, 1, 2, 3),
    )
```

```python
import functools
import math

import numpy as np
import jax
import jax.numpy as jnp
from jax import lax
from jax.experimental import pallas as pl
from jax.experimental.pallas import tpu as pltpu

F32 = jnp.float32
BF16 = jnp.bfloat16

D_MODEL = 1024
GROUP_W = 256
S5_GROUPS, S5_CH, S5_P = 16, 16, 64
S5_STATE = S5_GROUPS * S5_P
SSD_HEADS, SSD_N, SSD_P, SSD_NGROUPS, SSD_CONV, SSD_XBC = 4, 64, 64, 2, 4, 512
HG_HEADS, HG_K, HG_V = 4, 64, 64
GLA_HEADS, GLA_K, GLA_V, GLA_RANK, GLA_TAU = 4, 32, 64, 16, 16.0
N_HEADS = 4
D_FF = 4096
EPS = 1e-6
LB_FLOOR = 1e-30

O_U, O_Z, O_XBC = 0, 256, 512
O_HQ, O_HF, O_HI, O_HGATE = 1024, 1280, 1536, 1792
O_GQ, O_GK, O_GV, O_GGATE = 2048, 2176, 2304, 2560
O_MISC = 2816
N_PROJ = 2944
LR_LO = SSD_HEADS

SUB = 16
NEG = -1e30
LOG2E = 1.4426950408889634
VMEM_LIMIT = 56 * 1024 * 1024


def _sigmoid(x):
    return 0.5 * (1.0 + jnp.tanh(0.5 * x))


def _sigmoid_rel(x):
    return 1.0 / (1.0 + jnp.exp(-x))


def _silu(x):
    return x * _sigmoid(x)


def _softplus(x):
    return jnp.maximum(x, 0.0) + jnp.log1p(jnp.exp(-jnp.abs(x)))


def _log_sigmoid(x):
    return -_softplus(-x)


def _gelu_tanh(x):
    c = math.sqrt(2.0 / math.pi)
    return x * (0.5 * (1.0 + jnp.tanh(c * (x + 0.044715 * (x * x * x)))))


def _rms_scale(x, axis):
    return lax.rsqrt(jnp.mean(x * x, axis=axis, keepdims=True) + EPS)


def _dot(a, b):
    return jnp.dot(a, b, preferred_element_type=F32)


def _dot_nt(a, b):
    return lax.dot_general(a, b, (((1,), (1,)), ((), ())), preferred_element_type=F32)


def _dot_tn(a, b):
    return lax.dot_general(a, b, (((0,), (0,)), ((), ())), preferred_element_type=F32)


def _dot_split(a, b16, terms):
    acc, rest = None, a
    for _ in range(terms):
        piece = rest.astype(BF16)
        part = _dot(piece, b16)
        acc = part if acc is None else acc + part
        rest = rest - piece.astype(F32)
    return acc


def _idiv(x, n):
    shift = int(math.log2(n))
    assert 1 << shift == n
    return x >> shift


def _store_tiles(ref, row0, val):
    rows = val.shape[0]
    for j in range(val.shape[1] // 128):
        ref[j, row0:row0 + rows, :] = val[:, j * 128:(j + 1) * 128]


def _load_tiles(ref, row0, rows, n_tiles):
    return jnp.concatenate([ref[j, row0:row0 + rows, :] for j in range(n_tiles)], axis=1)


def _load_seq(ref, start, t_steps, nb, n_tiles):
    tiles = [ref[j, pl.ds(start, t_steps, stride=nb), :] for j in range(n_tiles)]
    return tiles[0] if n_tiles == 1 else jnp.concatenate(tiles, axis=1)


def _load_rows(ref, start, rows, n_tiles):
    tiles = [ref[j, pl.ds(start, rows), :] for j in range(n_tiles)]
    return tiles[0] if n_tiles == 1 else jnp.concatenate(tiles, axis=1)


def _gated_scan_tile(kd_lanes, dk, nb, t_steps, qf_ref, kp_ref, cp_ref, vp_ref, ob_ref, os_ref, st_ref,
                     s16_ref, j_ref, ko_ref, vo_ref):
    rows = nb * t_steps
    sub_rows = SUB * nb
    n_sub = rows // sub_rows
    n_kt = kd_lanes // 128

    def sub_body(sb):
        r0 = pl.multiple_of(sb * sub_rows, sub_rows)
        accs = [[None] * SUB for _ in range(2)]
        for j in range(SUB):
            reps = SUB - j
            rj = pl.multiple_of(r0 + nb * j, nb)
            q = _load_rows(qf_ref, rj, reps * nb, n_kt)
            c = _load_rows(cp_ref, rj, reps * nb, n_kt)
            k_j = jnp.tile(_load_rows(kp_ref, rj, nb, n_kt), (reps, 1))
            c_j = jnp.tile(_load_rows(cp_ref, rj, nb, n_kt), (reps, 1))
            w = _dot((q * k_j * jnp.exp2(c - c_j)).astype(BF16), j_ref[...])
            for n in range(2):
                v_j = vp_ref[n, pl.ds(rj, nb), :]
                for i in range(j, SUB):
                    term = w[(i - j) * nb:(i - j + 1) * nb, n * 128:(n + 1) * 128] * v_j
                    accs[n][i] = term if accs[n][i] is None else accs[n][i] + term
            yield
        for n in range(2):
            ob_ref[n, pl.ds(r0, sub_rows), :] = jnp.concatenate(accs[n], axis=0)

    t_idx = lax.broadcasted_iota(jnp.int32, (t_steps, kd_lanes), 0)
    ri = lax.broadcasted_iota(jnp.int32, (t_steps, N_HEADS * t_steps), 0)
    ci = lax.broadcasted_iota(jnp.int32, (t_steps, N_HEADS * t_steps), 1) & (t_steps - 1)
    lane_tile_w = min(kd_lanes, 128)
    heads_per_tile = lane_tile_w // dk
    tile_head = _idiv(lax.broadcasted_iota(jnp.int32, (64, lane_tile_w), 1), dk)
    levels = []
    c_sz = SUB
    while c_sz < t_steps:
        levels.append(c_sz)
        c_sz *= 2

    def seq_body(b):
        qb = _load_seq(qf_ref, b, t_steps, nb, n_kt)
        kb = _load_seq(kp_ref, b, t_steps, nb, n_kt)
        cb = _load_seq(cp_ref, b, t_steps, nb, n_kt)
        vb16 = _load_seq(vp_ref, b, t_steps, nb, 2).astype(BF16)
        c_last = cb[t_steps - 1:t_steps, :]

        o = _dot_nt((qb * jnp.exp2(cb)).astype(BF16), s16_ref[b])
        yield
        s = None
        for c_sz in levels:
            pieces = []
            for m in range(t_steps // (2 * c_sz)):
                a_row = m * 2 * c_sz + c_sz - 1
                pieces.append(jnp.broadcast_to(cb[a_row:a_row + 1, :], (2 * c_sz, kd_lanes)))
            anchor = pieces[0] if len(pieces) == 1 else jnp.concatenate(pieces, axis=0)
            upper = (t_idx & c_sz) != 0
            q_l = (qb * jnp.exp2(jnp.where(upper, cb - anchor, NEG))).astype(BF16)
            k_l = (kb * jnp.exp2(jnp.where(upper, NEG, anchor - cb))).astype(BF16)
            s_l = _dot_nt(q_l, jnp.tile(k_l, (N_HEADS, 1)) * ko_ref[...])
            if 2 * c_sz < t_steps:
                shift = int(math.log2(2 * c_sz))
                s_l = jnp.where((ri >> shift) == (ci >> shift), s_l, 0.0)
            s = s_l if s is None else s + s_l
            yield
        if s is not None:
            o = o + _dot(s.astype(BF16), jnp.tile(vb16, (N_HEADS, 1)) * vo_ref[...])
        for j in range(2):
            os_ref[j, pl.ds(b, t_steps, stride=nb), :] = o[:, j * 128:(j + 1) * 128]
        yield

        upd = _dot_tn(vb16, (kb * jnp.exp2(c_last - cb)).astype(BF16))
        decay = jnp.exp2(c_last)
        for h in range(N_HEADS):
            r0, l0 = h * 64, (h // heads_per_tile) * lane_tile_w
            own = tile_head == (h % heads_per_tile)
            blk = (decay[:, l0:l0 + lane_tile_w] * st_ref[b, r0:r0 + 64, l0:l0 + lane_tile_w]
                   + jnp.where(own, upd[r0:r0 + 64, l0:l0 + lane_tile_w], 0.0))
            st_ref[b, r0:r0 + 64, l0:l0 + lane_tile_w] = blk
            s16_ref[b, r0:r0 + 64, l0:l0 + lane_tile_w] = blk.astype(BF16)

    assert nb % n_sub == 0
    seq_per_trip = nb // n_sub

    def trip(i, carry):
        sub = sub_body(i)
        seqs = [seq_body(i * seq_per_trip + s) for s in range(seq_per_trip)]
        turn = 0
        while sub is not None or seqs:
            if sub is not None and next(sub, "done") == "done":
                sub = None
            if seqs:
                turn %= len(seqs)
                if next(seqs[turn], "done") == "done":
                    seqs.pop(turn)
                else:
                    turn += 1
        return carry

    lax.fori_loop(0, n_sub, trip, 0, unroll=2)


def _cumsum_time(nb, t_steps, src_ref, cp_ref, kd_lanes):
    def body(t, c):
        r = pl.multiple_of(t * nb, nb)
        c = c + src_ref[pl.ds(r, nb), :kd_lanes] * LOG2E
        for j in range(kd_lanes // 128):
            cp_ref[j, pl.ds(r, nb), :] = c[:, j * 128:(j + 1) * 128]
        return c

    lax.fori_loop(0, t_steps, body, jnp.zeros((nb, kd_lanes), F32), unroll=4)


def _head_rms_scale(o, jn_ref):
    return lax.rsqrt(_dot_split(o * o, jn_ref[...], 2) + EPS)


def _mixer_prompt_kernel(nb, t_steps,
                         x_ref, gmix_ref, win_ref, bblk_ref, lam_ref, cblk_ref, wglu_ref,
                         v256_ref, v128_ref, v512_ref, w2p_ref, ehd_ref, j256_ref, j128_ref, jn_ref,
                         ko256_ref, ko128_ref,
                         mix_ref, s5_ref, conv_ref, st_ssd_ref, st_hg_ref, st_gla_ref,
                         proj_ref, bu_ref, xe_ref, qf_ref, kp_ref, cp_ref, vp_ref, ob_ref, os_ref, la_ref,
                         s16_ssd_ref, s16_hg_ref, s16_gla_ref, hn_ref):
    rows = nb * t_steps
    conv_rows = (SSD_CONV - 1) * nb

    @pl.when(pl.program_id(0) == 0)
    def _():
        for ref in (s5_ref, st_ssd_ref, st_hg_ref, st_gla_ref, s16_ssd_ref, s16_hg_ref, s16_gla_ref):
            ref[...] = jnp.zeros_like(ref)
        xe_ref[0:conv_rows, :] = jnp.zeros((conv_rows, SSD_XBC), F32)

    x = x_ref[...]
    hn_ref[...] = (x * _rms_scale(x, -1) * gmix_ref[...]).astype(BF16)

    def project(lo, hi):
        proj_ref[:, lo:hi] = _dot(hn_ref[...], win_ref[:, lo:hi])

    project(O_U, O_HQ)
    project(O_MISC, N_PROJ)

    u = proj_ref[:, O_U:O_U + GROUP_W]
    bu_ref[...] = _dot(u.astype(BF16), bblk_ref[...])
    lam_r = jnp.broadcast_to(lam_ref[0:1, :], (nb, S5_STATE))
    lam_i = jnp.broadcast_to(lam_ref[1:2, :], (nb, S5_STATE))

    def s5_step(t, h):
        h_r, h_i = h
        r = pl.multiple_of(t * nb, nb)
        n_r = lam_r * h_r - lam_i * h_i + bu_ref[pl.ds(r, nb), 0:S5_STATE]
        n_i = lam_r * h_i + lam_i * h_r + bu_ref[pl.ds(r, nb), S5_STATE:2 * S5_STATE]
        bu_ref[pl.ds(r, nb), 0:S5_STATE] = n_r
        bu_ref[pl.ds(r, nb), S5_STATE:2 * S5_STATE] = n_i
        return n_r, n_i

    h_r, h_i = lax.fori_loop(0, t_steps, s5_step, (s5_ref[0], s5_ref[1]), unroll=2)
    s5_ref[0] = h_r
    s5_ref[1] = h_i
    project(O_HQ, O_GQ)
    y = _dot(bu_ref[...].astype(BF16), cblk_ref[...]) + v256_ref[0:1, :] * u
    z = _gelu_tanh(y)
    o_s5 = z * _sigmoid(_dot(z.astype(BF16), wglu_ref[...]) + v256_ref[1:2, :])
    mix_ref[:, 0:GROUP_W] = o_s5.astype(mix_ref.dtype)

    xe_ref[conv_rows:conv_rows + rows, :] = proj_ref[:, O_XBC:O_XBC + SSD_XBC]
    conv = v512_ref[SSD_CONV:SSD_CONV + 1, :]
    for j in range(SSD_CONV):
        conv = conv + xe_ref[j * nb:j * nb + rows, :] * v512_ref[j:j + 1, :]
    tail = xe_ref[rows:rows + conv_rows, :]
    xe_ref[0:conv_rows, :] = tail
    conv_ref[...] = tail
    act = _silu(conv)
    xs = act[:, 0:GROUP_W]
    b_g = act[:, GROUP_W:GROUP_W + 128]
    c_g = act[:, GROUP_W + 128:GROUP_W + 256]
    misc = proj_ref[:, O_MISC:O_MISC + 128]
    dt = _softplus(misc + v128_ref[0:1, :])
    dt_hd = _dot_split(dt, ehd_ref[...], 3)
    la_ref[...] = dt_hd * v256_ref[7:8, :]
    _store_tiles(qf_ref, 0, jnp.concatenate(
        [c_g[:, 0:64], c_g[:, 0:64], c_g[:, 64:128], c_g[:, 64:128]], axis=1))
    _store_tiles(kp_ref, 0, jnp.concatenate(
        [b_g[:, 0:64], b_g[:, 0:64], b_g[:, 64:128], b_g[:, 64:128]], axis=1))
    _store_tiles(vp_ref, 0, xs * dt_hd)
    _cumsum_time(nb, t_steps, la_ref, cp_ref, 256)
    _gated_scan_tile(256, 64, nb, t_steps, qf_ref, kp_ref, cp_ref, vp_ref, ob_ref, os_ref, st_ssd_ref,
                     s16_ssd_ref, j256_ref, ko256_ref, ko256_ref)
    project(O_GQ, O_MISC)
    y = _load_tiles(ob_ref, 0, rows, 2) + _load_tiles(os_ref, 0, rows, 2) + v256_ref[2:3, :] * xs
    y = y * _silu(proj_ref[:, O_Z:O_Z + GROUP_W])
    y = y * _rms_scale(y, -1) * v256_ref[3:4, :]
    mix_ref[:, GROUP_W:2 * GROUP_W] = y.astype(mix_ref.dtype)

    sig_f = _sigmoid_rel(proj_ref[:, O_HF:O_HF + GROUP_W])
    nsig_f = 1.0 - sig_f
    la_ref[...] = jnp.log(sig_f + v256_ref[4:5, :] * nsig_f)
    _store_tiles(qf_ref, 0, _silu(proj_ref[:, O_HQ:O_HQ + GROUP_W]))
    _store_tiles(kp_ref, 0, v256_ref[5:6, :] * nsig_f)
    _store_tiles(vp_ref, 0, proj_ref[:, O_HI:O_HI + GROUP_W])
    _cumsum_time(nb, t_steps, la_ref, cp_ref, 256)
    _gated_scan_tile(256, 64, nb, t_steps, qf_ref, kp_ref, cp_ref, vp_ref, ob_ref, os_ref, st_hg_ref,
                     s16_hg_ref, j256_ref, ko256_ref, ko256_ref)
    o = _load_tiles(ob_ref, 0, rows, 2) + _load_tiles(os_ref, 0, rows, 2)
    o = o * _head_rms_scale(o, jn_ref) * v256_ref[6:7, :]
    o = o * _silu(proj_ref[:, O_HGATE:O_HGATE + GROUP_W])
    mix_ref[:, 2 * GROUP_W:3 * GROUP_W] = o.astype(mix_ref.dtype)

    gk = _dot(misc.astype(BF16), w2p_ref[...]) + v128_ref[2:3, :]
    la_ref[:, 0:128] = _log_sigmoid(gk) * (1.0 / GLA_TAU)
    _store_tiles(qf_ref, 0, proj_ref[:, O_GQ:O_GQ + 128] * (GLA_K ** -0.5))
    _store_tiles(kp_ref, 0, proj_ref[:, O_GK:O_GK + 128])
    _store_tiles(vp_ref, 0, proj_ref[:, O_GV:O_GV + GROUP_W])
    _cumsum_time(nb, t_steps, la_ref, cp_ref, 128)
    _gated_scan_tile(128, 32, nb, t_steps, qf_ref, kp_ref, cp_ref, vp_ref, ob_ref, os_ref, st_gla_ref,
                     s16_gla_ref, j128_ref, ko128_ref, ko256_ref)
    o = _load_tiles(ob_ref, 0, rows, 2) + _load_tiles(os_ref, 0, rows, 2)
    o = o * _head_rms_scale(o, jn_ref) * v256_ref[8:9, :]
    o = o * _silu(proj_ref[:, O_GGATE:O_GGATE + GROUP_W])
    mix_ref[:, 3 * GROUP_W:4 * GROUP_W] = o.astype(mix_ref.dtype)


def _const_spec(shape):
    nd = len(shape)
    return pl.BlockSpec(shape, lambda i, _nd=nd: (0,) * _nd, pipeline_mode=pl.Buffered(1))


def _layer_spec(arr, layer):
    nd = arr.ndim - 1
    return pl.BlockSpec((None,) + arr.shape[1:], lambda i, _nd=nd: (layer,) + (0,) * _nd,
                        pipeline_mode=pl.Buffered(1))


PROMPT_LAYER_PARAMS = ('gmix', 'win', 'bblk', 'lam', 'cblk', 'wglu', 'v256', 'v128', 'v512', 'w2p')
PROMPT_SHARED_PARAMS = ('ehd', 'j256', 'j128', 'jn')


def _mixer_prompt(x_tm, nb, t_steps, p, layer):
    rows_total = x_tm.shape[0]
    rows = nb * t_steps
    n_tiles = rows_total // rows
    layered = [p[k] for k in PROMPT_LAYER_PARAMS]
    shared = [p[k] for k in PROMPT_SHARED_PARAMS] + [
        jnp.asarray(_block_ones(N_HEADS * t_steps, t_steps, 256, 64), BF16),
        jnp.asarray(_block_ones(N_HEADS * t_steps, t_steps, 128, 32), BF16)]
    consts = layered + shared
    in_specs = ([pl.BlockSpec((rows, D_MODEL), lambda i: (i, 0))]
                + [_layer_spec(c, layer) for c in layered] + [_const_spec(c.shape) for c in shared])
    conv_rows = (SSD_CONV - 1) * nb
    out_shape = (
        jax.ShapeDtypeStruct((rows_total, D_MODEL), BF16),
        jax.ShapeDtypeStruct((2, nb, S5_STATE), F32),
        jax.ShapeDtypeStruct((conv_rows, SSD_XBC), F32),
        jax.ShapeDtypeStruct((nb, 256, 256), F32),
        jax.ShapeDtypeStruct((nb, 256, 256), F32),
        jax.ShapeDtypeStruct((nb, 256, 128), F32),
    )
    out_specs = (
        pl.BlockSpec((rows, D_MODEL), lambda i: (i, 0)),
        pl.BlockSpec((2, nb, S5_STATE), lambda i: (0, 0, 0)),
        pl.BlockSpec((conv_rows, SSD_XBC), lambda i: (0, 0)),
        pl.BlockSpec((nb, 256, 256), lambda i: (0, 0, 0)),
        pl.BlockSpec((nb, 256, 256), lambda i: (0, 0, 0)),
        pl.BlockSpec((nb, 256, 128), lambda i: (0, 0, 0)),
    )
    scratch = [
        pltpu.VMEM((rows, N_PROJ), F32),
        pltpu.VMEM((rows, 2 * S5_STATE), F32),
        pltpu.VMEM((conv_rows + rows, SSD_XBC), F32),
        pltpu.VMEM((2, rows, 128), F32),
        pltpu.VMEM((2, rows, 128), F32),
        pltpu.VMEM((2, rows, 128), F32),
        pltpu.VMEM((2, rows, 128), F32),
        pltpu.VMEM((2, rows, 128), F32),
        pltpu.VMEM((2, rows, 128), F32),
        pltpu.VMEM((rows, 256), F32),
        pltpu.VMEM((nb, 256, 256), BF16),
        pltpu.VMEM((nb, 256, 256), BF16),
        pltpu.VMEM((nb, 256, 128), BF16),
        pltpu.VMEM((rows, D_MODEL), BF16),
    ]
    return pl.pallas_call(
        functools.partial(_mixer_prompt_kernel, nb, t_steps),
        grid=(n_tiles,),
        in_specs=in_specs,
        out_specs=out_specs,
        out_shape=out_shape,
        scratch_shapes=scratch,
        compiler_params=pltpu.CompilerParams(dimension_semantics=("arbitrary",),
                                             vmem_limit_bytes=VMEM_LIMIT),
        name="mixer_prompt",
    )(x_tm, *consts)


SAMPLE_LAYER_PARAMS = ('gmix', 'win', 'bblk', 'lam', 'cblk', 'wglu', 'v256', 'v128', 'v512', 'w2p')


def _mixer_sample_kernel(x_ref, gmix_ref, win_ref, bblk_ref, lam_ref, cblk_ref, wglu_ref,
                         v256_ref, v128_ref, v512_ref, w2p_ref, jn_ref,
                         s5_in, conv_in, ssd_in, hg_in, gla_in,
                         mix_ref, s5_out, conv_out, ssd_out, hg_out, gla_out,
                         proj_ref, xs_ref, act_ref, dt_ref, es_ref, qh_ref, kh_ref, eh_ref, vh_ref,
                         qg_ref, kg_ref, eg_ref, vg_ref, os_ref, oh_ref, og_ref):
    h = pl.program_id(0)

    @pl.when(h == 0)
    def _():
        x = x_ref[...]
        hn = (x * _rms_scale(x, -1) * gmix_ref[...]).astype(BF16)
        proj_ref[...] = _dot(hn, win_ref[...])

        u = proj_ref[:, O_U:O_U + GROUP_W]
        bu = _dot(u.astype(BF16), bblk_ref[...])
        lam_r, lam_i = lam_ref[0:1, :], lam_ref[1:2, :]
        h0_r, h0_i = s5_in[0], s5_in[1]
        h_r = lam_r * h0_r - lam_i * h0_i + bu[:, 0:S5_STATE]
        h_i = lam_r * h0_i + lam_i * h0_r + bu[:, S5_STATE:2 * S5_STATE]
        s5_out[0] = h_r
        s5_out[1] = h_i
        y = (_dot(jnp.concatenate([h_r, h_i], axis=1).astype(BF16), cblk_ref[...])
             + v256_ref[0:1, :] * u)
        z = _gelu_tanh(y)
        o_s5 = z * _sigmoid(_dot(z.astype(BF16), wglu_ref[...]) + v256_ref[1:2, :])
        mix_ref[:, 0:GROUP_W] = o_s5.astype(mix_ref.dtype)

        xbc = proj_ref[:, O_XBC:O_XBC + SSD_XBC]
        conv = v512_ref[SSD_CONV:SSD_CONV + 1, :] + v512_ref[SSD_CONV - 1:SSD_CONV, :] * xbc
        for j in range(SSD_CONV - 1):
            conv = conv + v512_ref[j:j + 1, :] * conv_in[:, j * SSD_XBC:(j + 1) * SSD_XBC]
        conv_out[:, 0:2 * SSD_XBC] = conv_in[:, SSD_XBC:3 * SSD_XBC]
        conv_out[:, 2 * SSD_XBC:3 * SSD_XBC] = xbc
        act = _silu(conv)
        xs_ref[...] = act[:, 0:GROUP_W]
        act_ref[...] = act.T
        misc = proj_ref[:, O_MISC:O_MISC + 128]
        dt = _softplus(misc + v128_ref[0:1, :])
        dt_ref[...] = dt.T
        es_ref[...] = jnp.exp(dt * v128_ref[1:2, :]).T

        sig_f = _sigmoid_rel(proj_ref[:, O_HF:O_HF + GROUP_W])
        nsig_f = 1.0 - sig_f
        eh_ref[...] = (sig_f + v256_ref[4:5, :] * nsig_f).T
        kh_ref[...] = (v256_ref[5:6, :] * nsig_f).T
        qh_ref[...] = _silu(proj_ref[:, O_HQ:O_HQ + GROUP_W]).T
        vh_ref[...] = proj_ref[:, O_HI:O_HI + GROUP_W].T

        gk = _dot(misc.astype(BF16), w2p_ref[...]) + v128_ref[2:3, :]
        eg_ref[...] = jnp.exp(_log_sigmoid(gk) * (1.0 / GLA_TAU)).T
        kg_ref[...] = proj_ref[:, O_GK:O_GK + 128].T
        qg_ref[...] = (proj_ref[:, O_GQ:O_GQ + 128] * (GLA_K ** -0.5)).T
        vg_ref[...] = proj_ref[:, O_GV:O_GV + GROUP_W].T

    def state_step(st_in, st_out, e_row, k_row, q_row, v, n_keys):
        nbatch = v.shape[1]
        v2 = jnp.concatenate([v, v], axis=0)

        def pair(row, i):
            return jnp.concatenate([jnp.broadcast_to(row(2 * i), (64, nbatch)),
                                    jnp.broadcast_to(row(2 * i + 1), (64, nbatch))], axis=0)

        def body(i, acc):
            lanes = pl.ds(pl.multiple_of(i * 128, 128), 128)
            s_new = pair(e_row, i) * st_in[:, lanes].T + pair(k_row, i) * v2
            st_out[:, lanes] = s_new.T
            qs = pair(q_row, i) * s_new
            return acc + qs[0:64, :] + qs[64:128, :]
        return lax.fori_loop(0, n_keys // 2, body, jnp.zeros_like(v), unroll=2)

    def head_rows():
        return pl.ds(pl.multiple_of(h * 64, 64), 64)

    g = _idiv(h, SSD_HEADS // SSD_NGROUPS)
    e_h = es_ref[pl.ds(h, 1), :]
    os_ref[head_rows(), :] = state_step(
        ssd_in, ssd_out,
        lambda i: e_h,
        lambda i: act_ref[pl.ds(GROUP_W + g * SSD_N + i, 1), :],
        lambda i: act_ref[pl.ds(GROUP_W + 128 + g * SSD_N + i, 1), :],
        act_ref[head_rows(), :] * dt_ref[pl.ds(h, 1), :], SSD_N)

    oh_ref[head_rows(), :] = state_step(
        hg_in, hg_out,
        lambda i: eh_ref[pl.ds(h * HG_K + i, 1), :],
        lambda i: kh_ref[pl.ds(h * HG_K + i, 1), :],
        lambda i: qh_ref[pl.ds(h * HG_K + i, 1), :],
        vh_ref[head_rows(), :], HG_K)

    og_ref[head_rows(), :] = state_step(
        gla_in, gla_out,
        lambda i: eg_ref[pl.ds(h * GLA_K + i, 1), :],
        lambda i: kg_ref[pl.ds(h * GLA_K + i, 1), :],
        lambda i: qg_ref[pl.ds(h * GLA_K + i, 1), :],
        vg_ref[head_rows(), :], GLA_K)

    @pl.when(h == N_HEADS - 1)
    def _():
        y = os_ref[...].T + v256_ref[2:3, :] * xs_ref[...]
        y = y * _silu(proj_ref[:, O_Z:O_Z + GROUP_W])
        mix_ref[:, GROUP_W:2 * GROUP_W] = (y * _rms_scale(y, -1) * v256_ref[3:4, :]).astype(mix_ref.dtype)
        o = oh_ref[...].T
        o = o * _head_rms_scale(o, jn_ref) * v256_ref[6:7, :] * _silu(proj_ref[:, O_HGATE:O_HGATE + GROUP_W])
        mix_ref[:, 2 * GROUP_W:3 * GROUP_W] = o.astype(mix_ref.dtype)
        o = og_ref[...].T
        o = o * _head_rms_scale(o, jn_ref) * v256_ref[8:9, :] * _silu(proj_ref[:, O_GGATE:O_GGATE + GROUP_W])
        mix_ref[:, 3 * GROUP_W:4 * GROUP_W] = o.astype(mix_ref.dtype)


def _mixer_sample(x, p, layer, s5, conv, ssd_t, hg_t, gla_t):
    nbatch = x.shape[0]
    layered = [p[k] for k in SAMPLE_LAYER_PARAMS]
    head_in = lambda a: pl.BlockSpec((None, nbatch, a.shape[2] // N_HEADS), lambda i: (layer, 0, i))
    head_blk = lambda a: pl.BlockSpec((nbatch, a.shape[2] // N_HEADS), lambda i: (0, i))
    res_blk = lambda shape: pl.BlockSpec(shape, lambda i, _nd=len(shape): (0,) * _nd)
    out_shape = (
        jax.ShapeDtypeStruct((nbatch, D_MODEL), BF16),
        jax.ShapeDtypeStruct(s5.shape[1:], F32),
        jax.ShapeDtypeStruct(conv.shape[1:], F32),
        jax.ShapeDtypeStruct(ssd_t.shape[1:], F32),
        jax.ShapeDtypeStruct(hg_t.shape[1:], F32),
        jax.ShapeDtypeStruct(gla_t.shape[1:], F32),
    )
    fm = lambda n: pltpu.VMEM((n, nbatch), F32)
    scratch = [
        pltpu.VMEM((nbatch, N_PROJ), F32),
        pltpu.VMEM((nbatch, GROUP_W), F32),
        fm(SSD_XBC),
        fm(128), fm(128),
        fm(GROUP_W), fm(GROUP_W), fm(GROUP_W), fm(GROUP_W),
        fm(128), fm(128), fm(128), fm(GROUP_W),
        fm(GROUP_W), fm(GROUP_W), fm(GROUP_W),
    ]
    return pl.pallas_call(
        _mixer_sample_kernel,
        grid=(N_HEADS,),
        in_specs=([_const_spec(x.shape)] + [_layer_spec(c, layer) for c in layered]
                  + [_const_spec(p['jn'].shape), _layer_spec(s5, layer), _layer_spec(conv, layer),
                     head_in(ssd_t), head_in(hg_t), head_in(gla_t)]),
        out_specs=(res_blk((nbatch, D_MODEL)), res_blk(s5.shape[1:]), res_blk(conv.shape[1:]),
                   head_blk(ssd_t), head_blk(hg_t), head_blk(gla_t)),
        out_shape=out_shape,
        scratch_shapes=scratch,
        compiler_params=pltpu.CompilerParams(dimension_semantics=("arbitrary",),
                                             vmem_limit_bytes=VMEM_LIMIT),
        name="mixer_sample",
    )(x, *layered, p['jn'], s5, conv, ssd_t, hg_t, gla_t)


FF_TILE = 512


def _mlp_kernel(final, mix_ref, x_ref, wout_ref, g_ref, wup_ref, wdn_ref, gfin_ref, o_ref):
    x1 = x_ref[...] + _dot(mix_ref[...], wout_ref[...])
    hn = (x1 * _rms_scale(x1, -1) * g_ref[...]).astype(BF16)
    acc = x1
    for f in range(D_FF // FF_TILE):
        up = jnp.maximum(_dot(hn, wup_ref[:, f * FF_TILE:(f + 1) * FF_TILE]), 0.0)
        acc = acc + _dot((up * up).astype(BF16), wdn_ref[f * FF_TILE:(f + 1) * FF_TILE, :])
    if final:
        acc = acc * _rms_scale(acc, -1) * gfin_ref[...]
    o_ref[...] = acc


def _mlp(mix, x, p, layer, final, tm):
    rows_total = x.shape[0]
    tm = min(tm, rows_total)
    layered = [p['wout'], p['gmlp'], p['wup'], p['wdn']]
    consts = layered + [p['gfin']]
    return pl.pallas_call(
        functools.partial(_mlp_kernel, final),
        grid=(rows_total // tm,),
        in_specs=[pl.BlockSpec((tm, D_MODEL), lambda i: (i, 0)),
                  pl.BlockSpec((tm, D_MODEL), lambda i: (i, 0))]
                 + [_layer_spec(c, layer) for c in layered] + [_const_spec(p['gfin'].shape)],
        out_specs=pl.BlockSpec((tm, D_MODEL), lambda i: (i, 0)),
        out_shape=jax.ShapeDtypeStruct((rows_total, D_MODEL), F32),
        compiler_params=pltpu.CompilerParams(dimension_semantics=("parallel",),
                                             vmem_limit_bytes=VMEM_LIMIT),
        name="out_mlp",
    )(mix, x, *consts)


N_IN = 2836
IN_MISC_LO, IN_LR_LO = 1024, 2820
PACK_ROWS = 256


def _pack_win_kernel(w_ref, o_ref):
    w = w_ref[...]
    n_dt, n_lr = SSD_HEADS, GLA_RANK
    o_ref[:, 0:IN_MISC_LO] = w[:, 0:IN_MISC_LO].astype(BF16)
    o_ref[:, IN_MISC_LO:O_MISC] = w[:, IN_MISC_LO + n_dt:IN_LR_LO].astype(BF16)
    tail = jnp.concatenate([w[:, IN_MISC_LO:IN_MISC_LO + n_dt], w[:, IN_LR_LO:IN_LR_LO + n_lr],
                            jnp.zeros((w.shape[0], 128 - n_dt - n_lr), w.dtype)], axis=1)
    o_ref[:, O_MISC:N_PROJ] = tail.astype(BF16)


def _pack_win(w_in):
    depth, d_in, n_in = w_in.shape
    assert n_in == N_IN and d_in % PACK_ROWS == 0
    return pl.pallas_call(
        _pack_win_kernel,
        grid=(depth, d_in // PACK_ROWS),
        in_specs=[pl.BlockSpec((None, PACK_ROWS, n_in), lambda l, i: (l, i, 0))],
        out_specs=pl.BlockSpec((None, PACK_ROWS, N_PROJ), lambda l, i: (l, i, 0)),
        out_shape=jax.ShapeDtypeStruct((depth, d_in, N_PROJ), BF16),
        compiler_params=pltpu.CompilerParams(dimension_semantics=("parallel", "parallel")),
        name="pack_win",
    )(w_in.astype(F32))


def _block_ones(n_rows, row_blk, n_cols, col_blk, scale=1.0):
    r = np.arange(n_rows)[:, None] // row_blk
    c = np.arange(n_cols)[None, :] // col_blk
    return (r == c).astype(np.float32) * scale


def _prepare(norm_mix_g, w_in, s5_lam_re, s5_lam_im, s5_log_dt, s5_b_re, s5_b_im, s5_c_re, s5_c_im,
             s5_d, s5_w_glu, s5_b_glu, ssd_conv_w, ssd_conv_b, ssd_dt_bias, ssd_a_log, ssd_d, ssd_norm_g,
             hg_lb_logits, hg_norm_g, gla_w_gk2, gla_b_gk, gla_norm_g):
    depth = w_in.shape[0]
    win = _pack_win(w_in)

    lam_re, lam_im = s5_lam_re.astype(F32), s5_lam_im.astype(F32)
    dt = jnp.exp(s5_log_dt.astype(F32))[..., None]
    ea = jnp.exp(lam_re * dt)
    lb_r, lb_i = ea * jnp.cos(lam_im * dt), ea * jnp.sin(lam_im * dt)
    den = lam_re * lam_re + lam_im * lam_im
    cr = ((lb_r - 1.0) * lam_re + lb_i * lam_im) / den
    ci = (lb_i * lam_re - (lb_r - 1.0) * lam_im) / den
    bb_r = cr[..., None] * s5_b_re - ci[..., None] * s5_b_im
    bb_i = cr[..., None] * s5_b_im + ci[..., None] * s5_b_re
    eye_g = jnp.eye(S5_GROUPS, dtype=F32)
    blk_b = lambda t: jnp.einsum('lgpc,gh->lgchp', t, eye_g).reshape(depth, GROUP_W, S5_STATE)
    blk_c = lambda t: jnp.einsum('lgcp,gh->lgphc', t, eye_g).reshape(depth, S5_STATE, GROUP_W)
    bblk = jnp.concatenate([blk_b(bb_r), blk_b(bb_i)], axis=-1)
    cblk = jnp.concatenate([blk_c(s5_c_re.astype(F32)), -blk_c(s5_c_im.astype(F32))], axis=1)
    lam = jnp.stack([lb_r.reshape(depth, S5_STATE), lb_i.reshape(depth, S5_STATE)], axis=1)

    sm = jax.nn.softmax(hg_lb_logits.astype(F32), axis=0)
    lb = jnp.cumsum(sm, axis=0) - sm[0:1]
    lb_floor = jnp.maximum(lb, LB_FLOOR)
    a = -jnp.exp(ssd_a_log.astype(F32))
    rep64 = lambda t: jnp.repeat(t, 64, axis=-1)
    zeros256 = jnp.zeros((depth, GROUP_W), F32)
    v256 = jnp.stack([s5_d, s5_b_glu, rep64(ssd_d), ssd_norm_g, lb_floor, 1.0 - lb, hg_norm_g, rep64(a),
                      jnp.tile(gla_norm_g, (1, GLA_HEADS))] + [zeros256] * 7, axis=1).astype(F32)
    pad128 = lambda t: jnp.pad(t, ((0, 0), (0, 128 - t.shape[-1])))
    zeros128 = jnp.zeros((depth, 128), F32)
    v128 = jnp.stack([pad128(ssd_dt_bias), pad128(a), gla_b_gk] + [zeros128] * 5, axis=1).astype(F32)
    v512 = jnp.concatenate([ssd_conv_w, ssd_conv_b[:, None, :], jnp.zeros((depth, 3, SSD_XBC), F32)],
                           axis=1).astype(F32)
    w2p = jnp.pad(gla_w_gk2, ((0, 0), (LR_LO, 128 - LR_LO - GLA_RANK), (0, 0))).astype(BF16)

    return dict(
        gmix=norm_mix_g[:, None, :].astype(F32), win=win, bblk=bblk.astype(BF16), lam=lam,
        cblk=cblk.astype(BF16), wglu=s5_w_glu.astype(BF16), v256=v256, v128=v128, v512=v512, w2p=w2p,
        ehd=jnp.asarray(_block_ones(128, 1, 256, 64) * (np.arange(128)[:, None] < SSD_HEADS), BF16),
        j256=jnp.asarray(_block_ones(256, 64, 256, 64), BF16),
        j128=jnp.asarray(_block_ones(128, 32, 256, 64), BF16),
        jn=jnp.asarray(_block_ones(256, 64, 256, 64, 1.0 / 64), BF16),
    )


def _diag_state(st, dk):
    nb = st.shape[0]
    s = st.reshape(nb, N_HEADS, 64, N_HEADS, dk)
    idx = jnp.arange(N_HEADS)
    s = s[:, idx, :, idx, :]
    return s.transpose(1, 0, 3, 2)


def kernel(x_prompt, x_sample, state_s5_re, state_s5_im, state_ssd_conv, state_ssd, state_hgrn, state_gla,
           norm_mix_g, w_in, s5_lam_re, s5_lam_im, s5_log_dt, s5_b_re, s5_b_im, s5_c_re, s5_c_im,
           s5_d, s5_w_glu, s5_b_glu, ssd_conv_w, ssd_conv_b, ssd_dt_bias, ssd_a_log, ssd_d, ssd_norm_g,
           hg_lb_logits, hg_norm_g, gla_w_gk2, gla_b_gk, gla_norm_g, w_out, norm_mlp_g, w_up, w_down,
           norm_final_g):
    nb, seq, _ = x_prompt.shape
    ns = x_sample.shape[0]
    depth = w_in.shape[0]
    t_steps = min(64, seq)

    p = _prepare(norm_mix_g, w_in, s5_lam_re, s5_lam_im, s5_log_dt, s5_b_re, s5_b_im, s5_c_re, s5_c_im,
                 s5_d, s5_w_glu, s5_b_glu, ssd_conv_w, ssd_conv_b, ssd_dt_bias, ssd_a_log, ssd_d,
                 ssd_norm_g, hg_lb_logits, hg_norm_g, gla_w_gk2, gla_b_gk, gla_norm_g)
    p.update(wout=w_out.astype(BF16), wup=w_up.astype(BF16), wdn=w_down.astype(BF16),
             gmlp=norm_mlp_g[:, None, :].astype(F32), gfin=norm_final_g[None, :].astype(F32))

    xp = x_prompt.astype(F32).transpose(1, 0, 2).reshape(seq * nb, D_MODEL)
    xs = x_sample.astype(F32).reshape(ns, D_MODEL)

    s5s = jnp.stack([state_s5_re.reshape(depth, ns, S5_STATE), state_s5_im.reshape(depth, ns, S5_STATE)],
                    axis=1).astype(F32)
    convs = state_ssd_conv.reshape(depth, ns, (SSD_CONV - 1) * SSD_XBC).astype(F32)
    ssds = state_ssd.reshape(depth, ns, SSD_HEADS * SSD_N * SSD_P).astype(F32)
    hgs = state_hgrn.reshape(depth, ns, HG_HEADS * HG_K * HG_V).astype(F32)
    glas = state_gla.reshape(depth, ns, GLA_HEADS * GLA_K * GLA_V).astype(F32)

    outs_p = [[] for _ in range(6)]
    outs_s = [[] for _ in range(5)]
    for l in range(depth):
        final = l == depth - 1
        mix, s5, conv, st_ssd, st_hg, st_gla = _mixer_prompt(xp, nb, t_steps, p, l)
        xp = _mlp(mix, xp, p, l, final, 512)
        for dst, val in zip(outs_p, (s5[0], s5[1], conv, st_ssd, st_hg, st_gla)):
            dst.append(val)

        res = _mixer_sample(xs, p, l, s5s, convs, ssds, hgs, glas)
        xs = _mlp(res[0], xs, p, l, final, 512)
        for dst, val in zip(outs_s, res[1:]):
            dst.append(val)

    y_prompt = xp.reshape(seq, nb, D_MODEL).transpose(1, 0, 2)
    y_sample = xs.reshape(ns, 1, D_MODEL)
    ps5r, ps5i, pconv, pssd, phg, pgla = [jnp.stack(v, axis=0) for v in outs_p]
    ss5, sconv, sssd, shg, sgla = [jnp.stack(v, axis=0) for v in outs_s]
    diag = lambda st, dk: _diag_state(st.reshape((depth * nb,) + st.shape[2:]), dk).reshape(
        depth, nb, N_HEADS, dk, 64)
    return (
        y_prompt, y_sample,
        ps5r.reshape(depth, nb, S5_GROUPS, S5_P), ps5i.reshape(depth, nb, S5_GROUPS, S5_P),
        pconv.reshape(depth, SSD_CONV - 1, nb, SSD_XBC).transpose(0, 2, 1, 3),
        diag(pssd, SSD_N), diag(phg, HG_K), diag(pgla, GLA_K),
        ss5[:, 0].reshape(depth, ns, S5_GROUPS, S5_P), ss5[:, 1].reshape(depth, ns, S5_GROUPS, S5_P),
        sconv.reshape(depth, ns, SSD_CONV - 1, SSD_XBC),
        sssd.reshape(depth, ns, SSD_HEADS, SSD_N, SSD_P),
        shg.reshape(depth, ns, HG_HEADS, HG_K, HG_V),
        sgla.reshape(depth, ns, GLA_HEADS, GLA_K, GLA_V),
    )
```

```python
import functools
import math

import numpy as np
import jax
import jax.numpy as jnp
from jax import lax
from jax.experimental import pallas as pl
from jax.experimental.pallas import tpu as pltpu

F32 = jnp.float32
BF16 = jnp.bfloat16

D_MODEL = 1024
GROUP_W = 256
S5_GROUPS, S5_CH, S5_P = 16, 16, 64
S5_STATE = S5_GROUPS * S5_P
SSD_HEADS, SSD_N, SSD_P, SSD_NGROUPS, SSD_CONV, SSD_XBC = 4, 64, 64, 2, 4, 512
HG_HEADS, HG_K, HG_V = 4, 64, 64
GLA_HEADS, GLA_K, GLA_V, GLA_RANK, GLA_TAU = 4, 32, 64, 16, 16.0
N_HEADS = 4
D_FF = 4096
EPS = 1e-6
LB_FLOOR = 1e-30

O_U, O_Z, O_XBC = 0, 256, 512
O_HQ, O_HF, O_HI, O_HGATE = 1024, 1280, 1536, 1792
O_GQ, O_GK, O_GV, O_GGATE = 2048, 2176, 2304, 2560
O_MISC = 2816
N_PROJ = 2944
LR_LO = SSD_HEADS

SUB = 16
NEG = -1e30
LOG2E = 1.4426950408889634
VMEM_LIMIT = 56 * 1024 * 1024


def _sigmoid(x):
    return 0.5 * (1.0 + jnp.tanh(0.5 * x))


def _sigmoid_rel(x):
    return 1.0 / (1.0 + jnp.exp(-x))


def _silu(x):
    return x * _sigmoid(x)


def _softplus(x):
    return jnp.maximum(x, 0.0) + jnp.log1p(jnp.exp(-jnp.abs(x)))


def _log_sigmoid(x):
    return -_softplus(-x)


def _gelu_tanh(x):
    c = math.sqrt(2.0 / math.pi)
    return x * (0.5 * (1.0 + jnp.tanh(c * (x + 0.044715 * (x * x * x)))))


def _rms_scale(x, axis):
    return lax.rsqrt(jnp.mean(x * x, axis=axis, keepdims=True) + EPS)


def _dot(a, b):
    return jnp.dot(a, b, preferred_element_type=F32)


def _dot_nt(a, b):
    return lax.dot_general(a, b, (((1,), (1,)), ((), ())), preferred_element_type=F32)


def _dot_tn(a, b):
    return lax.dot_general(a, b, (((0,), (0,)), ((), ())), preferred_element_type=F32)


def _dot_split(a, b16, terms):
    acc, rest = None, a
    for _ in range(terms):
        piece = rest.astype(BF16)
        part = _dot(piece, b16)
        acc = part if acc is None else acc + part
        rest = rest - piece.astype(F32)
    return acc


def _idiv(x, n):
    shift = int(math.log2(n))
    assert 1 << shift == n
    return x >> shift


def _store_tiles(ref, row0, val):
    rows = val.shape[0]
    for j in range(val.shape[1] // 128):
        ref[j, row0:row0 + rows, :] = val[:, j * 128:(j + 1) * 128]


def _load_tiles(ref, row0, rows, n_tiles):
    return jnp.concatenate([ref[j, row0:row0 + rows, :] for j in range(n_tiles)], axis=1)


def _load_seq(ref, start, t_steps, nb, n_tiles):
    tiles = [ref[j, pl.ds(start, t_steps, stride=nb), :] for j in range(n_tiles)]
    return tiles[0] if n_tiles == 1 else jnp.concatenate(tiles, axis=1)


def _load_rows(ref, start, rows, n_tiles):
    tiles = [ref[j, pl.ds(start, rows), :] for j in range(n_tiles)]
    return tiles[0] if n_tiles == 1 else jnp.concatenate(tiles, axis=1)


def _gated_scan_tile(kd_lanes, dk, nb, t_steps, qf_ref, kp_ref, cp_ref, vp_ref, ob_ref, os_ref, st_ref,
                     s16_ref, j_ref, ko_ref, vo_ref):
    rows = nb * t_steps
    sub_rows = SUB * nb
    n_sub = rows // sub_rows
    n_kt = kd_lanes // 128

    def sub_body(sb):
        r0 = pl.multiple_of(sb * sub_rows, sub_rows)
        accs = [[None] * SUB for _ in range(2)]
        for j in range(SUB):
            reps = SUB - j
            rj = pl.multiple_of(r0 + nb * j, nb)
            q = _load_rows(qf_ref, rj, reps * nb, n_kt)
            c = _load_rows(cp_ref, rj, reps * nb, n_kt)
            k_j = jnp.tile(_load_rows(kp_ref, rj, nb, n_kt), (reps, 1))
            c_j = jnp.tile(_load_rows(cp_ref, rj, nb, n_kt), (reps, 1))
            w = _dot((q * k_j * jnp.exp2(c - c_j)).astype(BF16), j_ref[...])
            for n in range(2):
                v_j = vp_ref[n, pl.ds(rj, nb), :]
                for i in range(j, SUB):
                    term = w[(i - j) * nb:(i - j + 1) * nb, n * 128:(n + 1) * 128] * v_j
                    accs[n][i] = term if accs[n][i] is None else accs[n][i] + term
            yield
        for n in range(2):
            ob_ref[n, pl.ds(r0, sub_rows), :] = jnp.concatenate(accs[n], axis=0)

    t_idx = lax.broadcasted_iota(jnp.int32, (t_steps, kd_lanes), 0)
    ri = lax.broadcasted_iota(jnp.int32, (t_steps, N_HEADS * t_steps), 0)
    ci = lax.broadcasted_iota(jnp.int32, (t_steps, N_HEADS * t_steps), 1) & (t_steps - 1)
    lane_tile_w = min(kd_lanes, 128)
    heads_per_tile = lane_tile_w // dk
    tile_head = _idiv(lax.broadcasted_iota(jnp.int32, (64, lane_tile_w), 1), dk)
    levels = []
    c_sz = SUB
    while c_sz < t_steps:
        levels.append(c_sz)
        c_sz *= 2

    def seq_body(b):
        qb = _load_seq(qf_ref, b, t_steps, nb, n_kt)
        kb = _load_seq(kp_ref, b, t_steps, nb, n_kt)
        cb = _load_seq(cp_ref, b, t_steps, nb, n_kt)
        vb16 = _load_seq(vp_ref, b, t_steps, nb, 2).astype(BF16)
        c_last = cb[t_steps - 1:t_steps, :]

        o = _dot_nt((qb * jnp.exp2(cb)).astype(BF16), s16_ref[b])
        yield
        s = None
        for c_sz in levels:
            pieces = []
            for m in range(t_steps // (2 * c_sz)):
                a_row = m * 2 * c_sz + c_sz - 1
                pieces.append(jnp.broadcast_to(cb[a_row:a_row + 1, :], (2 * c_sz, kd_lanes)))
            anchor = pieces[0] if len(pieces) == 1 else jnp.concatenate(pieces, axis=0)
            upper = (t_idx & c_sz) != 0
            q_l = (qb * jnp.exp2(jnp.where(upper, cb - anchor, NEG))).astype(BF16)
            k_l = (kb * jnp.exp2(jnp.where(upper, NEG, anchor - cb))).astype(BF16)
            s_l = _dot_nt(q_l, jnp.tile(k_l, (N_HEADS, 1)) * ko_ref[...])
            if 2 * c_sz < t_steps:
                shift = int(math.log2(2 * c_sz))
                s_l = jnp.where((ri >> shift) == (ci >> shift), s_l, 0.0)
            s = s_l if s is None else s + s_l
            yield
        if s is not None:
            o = o + _dot(s.astype(BF16), jnp.tile(vb16, (N_HEADS, 1)) * vo_ref[...])
        for j in range(2):
            os_ref[j, pl.ds(b, t_steps, stride=nb), :] = o[:, j * 128:(j + 1) * 128]
        yield

        upd = _dot_tn(vb16, (kb * jnp.exp2(c_last - cb)).astype(BF16))
        decay = jnp.exp2(c_last)
        for h in range(N_HEADS):
            r0, l0 = h * 64, (h // heads_per_tile) * lane_tile_w
            own = tile_head == (h % heads_per_tile)
            blk = (decay[:, l0:l0 + lane_tile_w] * st_ref[b, r0:r0 + 64, l0:l0 + lane_tile_w]
                   + jnp.where(own, upd[r0:r0 + 64, l0:l0 + lane_tile_w], 0.0))
            st_ref[b, r0:r0 + 64, l0:l0 + lane_tile_w] = blk
            s16_ref[b, r0:r0 + 64, l0:l0 + lane_tile_w] = blk.astype(BF16)

    assert nb % n_sub == 0
    seq_per_trip = nb // n_sub

    def trip(i, carry):
        sub = sub_body(i)
        seqs = [seq_body(i * seq_per_trip + s) for s in range(seq_per_trip)]
        turn = 0
        while sub is not None or seqs:
            if sub is not None and next(sub, "done") == "done":
                sub = None
            if seqs:
                turn %= len(seqs)
                if next(seqs[turn], "done") == "done":
                    seqs.pop(turn)
                else:
                    turn += 1
        return carry

    lax.fori_loop(0, n_sub, trip, 0, unroll=2)


def _cumsum_time(nb, t_steps, src_ref, cp_ref, kd_lanes):
    def body(t, c):
        r = pl.multiple_of(t * nb, nb)
        c = c + src_ref[pl.ds(r, nb), :kd_lanes] * LOG2E
        for j in range(kd_lanes // 128):
            cp_ref[j, pl.ds(r, nb), :] = c[:, j * 128:(j + 1) * 128]
        return c

    lax.fori_loop(0, t_steps, body, jnp.zeros((nb, kd_lanes), F32), unroll=4)


def _head_rms_scale(o, jn_ref):
    return lax.rsqrt(_dot_split(o * o, jn_ref[...], 2) + EPS)


def _mixer_prompt_kernel(nb, t_steps, batch_major_in,
                         x_ref, gmix_ref, win_ref, bblk_ref, lam_ref, cblk_ref, wglu_ref,
                         v256_ref, v128_ref, v512_ref, w2p_ref, ehd_ref, j256_ref, j128_ref, jn_ref,
                         ko256_ref, ko128_ref,
                         mix_ref, s5_ref, conv_ref, st_ssd_ref, st_hg_ref, st_gla_ref, *rest):
    if batch_major_in:
        xtm_ref, xt_ref = rest[0], rest[-1]
        rest = rest[1:-1]
    (proj_ref, bu_ref, xe_ref, qf_ref, kp_ref, cp_ref, vp_ref, ob_ref, os_ref, la_ref,
     s16_ssd_ref, s16_hg_ref, s16_gla_ref, hn_ref) = rest
    rows = nb * t_steps
    conv_rows = (SSD_CONV - 1) * nb

    @pl.when(pl.program_id(0) == 0)
    def _():
        for ref in (s5_ref, st_ssd_ref, st_hg_ref, st_gla_ref, s16_ssd_ref, s16_hg_ref, s16_gla_ref):
            ref[...] = jnp.zeros_like(ref)
        xe_ref[0:conv_rows, :] = jnp.zeros((conv_rows, SSD_XBC), F32)

    if batch_major_in:
        for b in range(nb):
            for j in range(D_MODEL // 128):
                xt_ref[j, pl.ds(b, t_steps, stride=nb), :] = x_ref[b, :, j * 128:(j + 1) * 128]
        x = _load_tiles(xt_ref, 0, rows, D_MODEL // 128)
        xtm_ref[...] = x
    else:
        x = x_ref[...]
    hn_ref[...] = (x * _rms_scale(x, -1) * gmix_ref[...]).astype(BF16)

    def project(lo, hi):
        proj_ref[:, lo:hi] = _dot(hn_ref[...], win_ref[:, lo:hi])

    project(O_U, O_HQ)
    project(O_MISC, N_PROJ)

    u = proj_ref[:, O_U:O_U + GROUP_W]
    bu_ref[...] = _dot(u.astype(BF16), bblk_ref[...])
    lam_r = jnp.broadcast_to(lam_ref[0:1, :], (nb, S5_STATE))
    lam_i = jnp.broadcast_to(lam_ref[1:2, :], (nb, S5_STATE))

    def s5_step(t, h):
        h_r, h_i = h
        r = pl.multiple_of(t * nb, nb)
        n_r = lam_r * h_r - lam_i * h_i + bu_ref[pl.ds(r, nb), 0:S5_STATE]
        n_i = lam_r * h_i + lam_i * h_r + bu_ref[pl.ds(r, nb), S5_STATE:2 * S5_STATE]
        bu_ref[pl.ds(r, nb), 0:S5_STATE] = n_r
        bu_ref[pl.ds(r, nb), S5_STATE:2 * S5_STATE] = n_i
        return n_r, n_i

    h_r, h_i = lax.fori_loop(0, t_steps, s5_step, (s5_ref[0], s5_ref[1]), unroll=2)
    s5_ref[0] = h_r
    s5_ref[1] = h_i
    project(O_HQ, O_GQ)
    y = _dot(bu_ref[...].astype(BF16), cblk_ref[...]) + v256_ref[0:1, :] * u
    z = _gelu_tanh(y)
    o_s5 = z * _sigmoid(_dot(z.astype(BF16), wglu_ref[...]) + v256_ref[1:2, :])
    mix_ref[:, 0:GROUP_W] = o_s5.astype(mix_ref.dtype)

    xe_ref[conv_rows:conv_rows + rows, :] = proj_ref[:, O_XBC:O_XBC + SSD_XBC]
    conv = v512_ref[SSD_CONV:SSD_CONV + 1, :]
    for j in range(SSD_CONV):
        conv = conv + xe_ref[j * nb:j * nb + rows, :] * v512_ref[j:j + 1, :]
    tail = xe_ref[rows:rows + conv_rows, :]
    xe_ref[0:conv_rows, :] = tail
    conv_ref[...] = tail
    act = _silu(conv)
    xs = act[:, 0:GROUP_W]
    b_g = act[:, GROUP_W:GROUP_W + 128]
    c_g = act[:, GROUP_W + 128:GROUP_W + 256]
    misc = proj_ref[:, O_MISC:O_MISC + 128]
    dt = _softplus(misc + v128_ref[0:1, :])
    dt_hd = _dot_split(dt, ehd_ref[...], 3)
    la_ref[...] = dt_hd * v256_ref[7:8, :]
    _store_tiles(qf_ref, 0, jnp.concatenate(
        [c_g[:, 0:64], c_g[:, 0:64], c_g[:, 64:128], c_g[:, 64:128]], axis=1))
    _store_tiles(kp_ref, 0, jnp.concatenate(
        [b_g[:, 0:64], b_g[:, 0:64], b_g[:, 64:128], b_g[:, 64:128]], axis=1))
    _store_tiles(vp_ref, 0, xs * dt_hd)
    _cumsum_time(nb, t_steps, la_ref, cp_ref, 256)
    _gated_scan_tile(256, 64, nb, t_steps, qf_ref, kp_ref, cp_ref, vp_ref, ob_ref, os_ref, st_ssd_ref,
                     s16_ssd_ref, j256_ref, ko256_ref, ko256_ref)
    project(O_GQ, O_MISC)
    y = _load_tiles(ob_ref, 0, rows, 2) + _load_tiles(os_ref, 0, rows, 2) + v256_ref[2:3, :] * xs
    y = y * _silu(proj_ref[:, O_Z:O_Z + GROUP_W])
    y = y * _rms_scale(y, -1) * v256_ref[3:4, :]
    mix_ref[:, GROUP_W:2 * GROUP_W] = y.astype(mix_ref.dtype)

    sig_f = _sigmoid_rel(proj_ref[:, O_HF:O_HF + GROUP_W])
    nsig_f = 1.0 - sig_f
    la_ref[...] = jnp.log(sig_f + v256_ref[4:5, :] * nsig_f)
    _store_tiles(qf_ref, 0, _silu(proj_ref[:, O_HQ:O_HQ + GROUP_W]))
    _store_tiles(kp_ref, 0, v256_ref[5:6, :] * nsig_f)
    _store_tiles(vp_ref, 0, proj_ref[:, O_HI:O_HI + GROUP_W])
    _cumsum_time(nb, t_steps, la_ref, cp_ref, 256)
    _gated_scan_tile(256, 64, nb, t_steps, qf_ref, kp_ref, cp_ref, vp_ref, ob_ref, os_ref, st_hg_ref,
                     s16_hg_ref, j256_ref, ko256_ref, ko256_ref)
    o = _load_tiles(ob_ref, 0, rows, 2) + _load_tiles(os_ref, 0, rows, 2)
    o = o * _head_rms_scale(o, jn_ref) * v256_ref[6:7, :]
    o = o * _silu(proj_ref[:, O_HGATE:O_HGATE + GROUP_W])
    mix_ref[:, 2 * GROUP_W:3 * GROUP_W] = o.astype(mix_ref.dtype)

    gk = _dot(misc.astype(BF16), w2p_ref[...]) + v128_ref[2:3, :]
    la_ref[:, 0:128] = _log_sigmoid(gk) * (1.0 / GLA_TAU)
    _store_tiles(qf_ref, 0, proj_ref[:, O_GQ:O_GQ + 128] * (GLA_K ** -0.5))
    _store_tiles(kp_ref, 0, proj_ref[:, O_GK:O_GK + 128])
    _store_tiles(vp_ref, 0, proj_ref[:, O_GV:O_GV + GROUP_W])
    _cumsum_time(nb, t_steps, la_ref, cp_ref, 128)
    _gated_scan_tile(128, 32, nb, t_steps, qf_ref, kp_ref, cp_ref, vp_ref, ob_ref, os_ref, st_gla_ref,
                     s16_gla_ref, j128_ref, ko128_ref, ko256_ref)
    o = _load_tiles(ob_ref, 0, rows, 2) + _load_tiles(os_ref, 0, rows, 2)
    o = o * _head_rms_scale(o, jn_ref) * v256_ref[8:9, :]
    o = o * _silu(proj_ref[:, O_GGATE:O_GGATE + GROUP_W])
    mix_ref[:, 3 * GROUP_W:4 * GROUP_W] = o.astype(mix_ref.dtype)


def _const_spec(shape):
    nd = len(shape)
    return pl.BlockSpec(shape, lambda i, _nd=nd: (0,) * _nd, pipeline_mode=pl.Buffered(1))


def _layer_spec(arr, layer):
    nd = arr.ndim - 1
    return pl.BlockSpec((None,) + arr.shape[1:], lambda i, _nd=nd: (layer,) + (0,) * _nd,
                        pipeline_mode=pl.Buffered(1))


PROMPT_LAYER_PARAMS = ('gmix', 'win', 'bblk', 'lam', 'cblk', 'wglu', 'v256', 'v128', 'v512', 'w2p')
PROMPT_SHARED_PARAMS = ('ehd', 'j256', 'j128', 'jn')


def _mixer_prompt(x, nb, t_steps, p, layer):
    batch_major_in = x.ndim == 3
    rows_total = x.shape[0] * x.shape[1] if batch_major_in else x.shape[0]
    rows = nb * t_steps
    n_tiles = rows_total // rows
    layered = [p[k] for k in PROMPT_LAYER_PARAMS]
    shared = [p[k] for k in PROMPT_SHARED_PARAMS] + [
        jnp.asarray(_block_ones(N_HEADS * t_steps, t_steps, 256, 64), BF16),
        jnp.asarray(_block_ones(N_HEADS * t_steps, t_steps, 128, 32), BF16)]
    consts = layered + shared
    x_spec = (pl.BlockSpec((nb, t_steps, D_MODEL), lambda i: (0, i, 0)) if batch_major_in
              else pl.BlockSpec((rows, D_MODEL), lambda i: (i, 0)))
    in_specs = [x_spec] + [_layer_spec(c, layer) for c in layered] + [_const_spec(c.shape) for c in shared]
    conv_rows = (SSD_CONV - 1) * nb
    out_shape = (
        jax.ShapeDtypeStruct((rows_total, D_MODEL), BF16),
        jax.ShapeDtypeStruct((2, nb, S5_STATE), F32),
        jax.ShapeDtypeStruct((conv_rows, SSD_XBC), F32),
        jax.ShapeDtypeStruct((nb, 256, 256), F32),
        jax.ShapeDtypeStruct((nb, 256, 256), F32),
        jax.ShapeDtypeStruct((nb, 256, 128), F32),
    )
    out_specs = (
        pl.BlockSpec((rows, D_MODEL), lambda i: (i, 0)),
        pl.BlockSpec((2, nb, S5_STATE), lambda i: (0, 0, 0)),
        pl.BlockSpec((conv_rows, SSD_XBC), lambda i: (0, 0)),
        pl.BlockSpec((nb, 256, 256), lambda i: (0, 0, 0)),
        pl.BlockSpec((nb, 256, 256), lambda i: (0, 0, 0)),
        pl.BlockSpec((nb, 256, 128), lambda i: (0, 0, 0)),
    )
    scratch = [
        pltpu.VMEM((rows, N_PROJ), F32),
        pltpu.VMEM((rows, 2 * S5_STATE), F32),
        pltpu.VMEM((conv_rows + rows, SSD_XBC), F32),
        pltpu.VMEM((2, rows, 128), F32),
        pltpu.VMEM((2, rows, 128), F32),
        pltpu.VMEM((2, rows, 128), F32),
        pltpu.VMEM((2, rows, 128), F32),
        pltpu.VMEM((2, rows, 128), F32),
        pltpu.VMEM((2, rows, 128), F32),
        pltpu.VMEM((rows, 256), F32),
        pltpu.VMEM((nb, 256, 256), BF16),
        pltpu.VMEM((nb, 256, 256), BF16),
        pltpu.VMEM((nb, 256, 128), BF16),
        pltpu.VMEM((rows, D_MODEL), BF16),
    ]
    if batch_major_in:
        out_shape += (jax.ShapeDtypeStruct((rows_total, D_MODEL), F32),)
        out_specs += (pl.BlockSpec((rows, D_MODEL), lambda i: (i, 0)),)
        scratch.append(pltpu.VMEM((D_MODEL // 128, rows, 128), F32))
    return pl.pallas_call(
        functools.partial(_mixer_prompt_kernel, nb, t_steps, batch_major_in),
        grid=(n_tiles,),
        in_specs=in_specs,
        out_specs=out_specs,
        out_shape=out_shape,
        scratch_shapes=scratch,
        compiler_params=pltpu.CompilerParams(dimension_semantics=("arbitrary",),
                                             vmem_limit_bytes=VMEM_LIMIT),
        name="mixer_prompt",
    )(x, *consts)


SAMPLE_LAYER_PARAMS = ('gmix', 'win', 'bblk', 'lam', 'cblk', 'wglu', 'v256', 'v128', 'v512', 'w2p')


def _mixer_sample_kernel(x_ref, gmix_ref, win_ref, bblk_ref, lam_ref, cblk_ref, wglu_ref,
                         v256_ref, v128_ref, v512_ref, w2p_ref, jn_ref,
                         s5_in, conv_in, ssd_in, hg_in, gla_in,
                         mix_ref, s5_out, conv_out, ssd_out, hg_out, gla_out,
                         proj_ref, xs_ref, act_ref, dt_ref, es_ref, qh_ref, kh_ref, eh_ref, vh_ref,
                         qg_ref, kg_ref, eg_ref, vg_ref, os_ref, oh_ref, og_ref):
    h = pl.program_id(0)

    @pl.when(h == 0)
    def _():
        x = x_ref[...]
        hn = (x * _rms_scale(x, -1) * gmix_ref[...]).astype(BF16)
        proj_ref[...] = _dot(hn, win_ref[...])

        u = proj_ref[:, O_U:O_U + GROUP_W]
        bu = _dot(u.astype(BF16), bblk_ref[...])
        lam_r, lam_i = lam_ref[0:1, :], lam_ref[1:2, :]
        h0_r, h0_i = s5_in[0], s5_in[1]
        h_r = lam_r * h0_r - lam_i * h0_i + bu[:, 0:S5_STATE]
        h_i = lam_r * h0_i + lam_i * h0_r + bu[:, S5_STATE:2 * S5_STATE]
        s5_out[0] = h_r
        s5_out[1] = h_i
        y = (_dot(jnp.concatenate([h_r, h_i], axis=1).astype(BF16), cblk_ref[...])
             + v256_ref[0:1, :] * u)
        z = _gelu_tanh(y)
        o_s5 = z * _sigmoid(_dot(z.astype(BF16), wglu_ref[...]) + v256_ref[1:2, :])
        mix_ref[:, 0:GROUP_W] = o_s5.astype(mix_ref.dtype)

        xbc = proj_ref[:, O_XBC:O_XBC + SSD_XBC]
        conv = v512_ref[SSD_CONV:SSD_CONV + 1, :] + v512_ref[SSD_CONV - 1:SSD_CONV, :] * xbc
        for j in range(SSD_CONV - 1):
            conv = conv + v512_ref[j:j + 1, :] * conv_in[:, j * SSD_XBC:(j + 1) * SSD_XBC]
        conv_out[:, 0:2 * SSD_XBC] = conv_in[:, SSD_XBC:3 * SSD_XBC]
        conv_out[:, 2 * SSD_XBC:3 * SSD_XBC] = xbc
        act = _silu(conv)
        xs_ref[...] = act[:, 0:GROUP_W]
        act_ref[...] = act.T
        misc = proj_ref[:, O_MISC:O_MISC + 128]
        dt = _softplus(misc + v128_ref[0:1, :])
        dt_ref[...] = dt.T
        es_ref[...] = jnp.exp(dt * v128_ref[1:2, :]).T

        sig_f = _sigmoid_rel(proj_ref[:, O_HF:O_HF + GROUP_W])
        nsig_f = 1.0 - sig_f
        eh_ref[...] = (sig_f + v256_ref[4:5, :] * nsig_f).T
        kh_ref[...] = (v256_ref[5:6, :] * nsig_f).T
        qh_ref[...] = _silu(proj_ref[:, O_HQ:O_HQ + GROUP_W]).T
        vh_ref[...] = proj_ref[:, O_HI:O_HI + GROUP_W].T

        gk = _dot(misc.astype(BF16), w2p_ref[...]) + v128_ref[2:3, :]
        eg_ref[...] = jnp.exp(_log_sigmoid(gk) * (1.0 / GLA_TAU)).T
        kg_ref[...] = proj_ref[:, O_GK:O_GK + 128].T
        qg_ref[...] = (proj_ref[:, O_GQ:O_GQ + 128] * (GLA_K ** -0.5)).T
        vg_ref[...] = proj_ref[:, O_GV:O_GV + GROUP_W].T

    def state_step(st_in, st_out, e_row, k_row, q_row, v, n_keys):
        def body(i, acc):
            r = pl.multiple_of(i * 64, 64)
            s_new = e_row(i) * st_in[pl.ds(r, 64), :] + k_row(i) * v
            st_out[pl.ds(r, 64), :] = s_new
            return acc + q_row(i) * s_new
        return lax.fori_loop(0, n_keys, body, jnp.zeros_like(v), unroll=4)

    def head_rows():
        return pl.ds(pl.multiple_of(h * 64, 64), 64)

    g = _idiv(h, SSD_HEADS // SSD_NGROUPS)
    e_h = es_ref[pl.ds(h, 1), :]
    os_ref[head_rows(), :] = state_step(
        ssd_in, ssd_out,
        lambda i: e_h,
        lambda i: act_ref[pl.ds(GROUP_W + g * SSD_N + i, 1), :],
        lambda i: act_ref[pl.ds(GROUP_W + 128 + g * SSD_N + i, 1), :],
        act_ref[head_rows(), :] * dt_ref[pl.ds(h, 1), :], SSD_N)

    oh_ref[head_rows(), :] = state_step(
        hg_in, hg_out,
        lambda i: eh_ref[pl.ds(h * HG_K + i, 1), :],
        lambda i: kh_ref[pl.ds(h * HG_K + i, 1), :],
        lambda i: qh_ref[pl.ds(h * HG_K + i, 1), :],
        vh_ref[head_rows(), :], HG_K)

    og_ref[head_rows(), :] = state_step(
        gla_in, gla_out,
        lambda i: eg_ref[pl.ds(h * GLA_K + i, 1), :],
        lambda i: kg_ref[pl.ds(h * GLA_K + i, 1), :],
        lambda i: qg_ref[pl.ds(h * GLA_K + i, 1), :],
        vg_ref[head_rows(), :], GLA_K)

    @pl.when(h == N_HEADS - 1)
    def _():
        y = os_ref[...].T + v256_ref[2:3, :] * xs_ref[...]
        y = y * _silu(proj_ref[:, O_Z:O_Z + GROUP_W])
        mix_ref[:, GROUP_W:2 * GROUP_W] = (y * _rms_scale(y, -1) * v256_ref[3:4, :]).astype(mix_ref.dtype)
        o = oh_ref[...].T
        o = o * _head_rms_scale(o, jn_ref) * v256_ref[6:7, :] * _silu(proj_ref[:, O_HGATE:O_HGATE + GROUP_W])
        mix_ref[:, 2 * GROUP_W:3 * GROUP_W] = o.astype(mix_ref.dtype)
        o = og_ref[...].T
        o = o * _head_rms_scale(o, jn_ref) * v256_ref[8:9, :] * _silu(proj_ref[:, O_GGATE:O_GGATE + GROUP_W])
        mix_ref[:, 3 * GROUP_W:4 * GROUP_W] = o.astype(mix_ref.dtype)


def _mixer_sample(x, p, layer, s5, conv, ssd_t, hg_t, gla_t):
    nbatch = x.shape[0]
    layered = [p[k] for k in SAMPLE_LAYER_PARAMS]
    head_in = lambda a: pl.BlockSpec((None, a.shape[1] // N_HEADS, nbatch), lambda i: (layer, i, 0))
    head_blk = lambda a: pl.BlockSpec((a.shape[1] // N_HEADS, nbatch), lambda i: (i, 0))
    res_blk = lambda shape: pl.BlockSpec(shape, lambda i, _nd=len(shape): (0,) * _nd)
    out_shape = (
        jax.ShapeDtypeStruct((nbatch, D_MODEL), BF16),
        jax.ShapeDtypeStruct(s5.shape[1:], F32),
        jax.ShapeDtypeStruct(conv.shape[1:], F32),
        jax.ShapeDtypeStruct(ssd_t.shape[1:], F32),
        jax.ShapeDtypeStruct(hg_t.shape[1:], F32),
        jax.ShapeDtypeStruct(gla_t.shape[1:], F32),
    )
    fm = lambda n: pltpu.VMEM((n, nbatch), F32)
    scratch = [
        pltpu.VMEM((nbatch, N_PROJ), F32),
        pltpu.VMEM((nbatch, GROUP_W), F32),
        fm(SSD_XBC),
        fm(128), fm(128),
        fm(GROUP_W), fm(GROUP_W), fm(GROUP_W), fm(GROUP_W),
        fm(128), fm(128), fm(128), fm(GROUP_W),
        fm(GROUP_W), fm(GROUP_W), fm(GROUP_W),
    ]
    return pl.pallas_call(
        _mixer_sample_kernel,
        grid=(N_HEADS,),
        in_specs=([_const_spec(x.shape)] + [_layer_spec(c, layer) for c in layered]
                  + [_const_spec(p['jn'].shape), _layer_spec(s5, layer), _layer_spec(conv, layer),
                     head_in(ssd_t), head_in(hg_t), head_in(gla_t)]),
        out_specs=(res_blk((nbatch, D_MODEL)), res_blk(s5.shape[1:]), res_blk(conv.shape[1:]),
                   head_blk(ssd_t), head_blk(hg_t), head_blk(gla_t)),
        out_shape=out_shape,
        scratch_shapes=scratch,
        compiler_params=pltpu.CompilerParams(dimension_semantics=("arbitrary",),
                                             vmem_limit_bytes=VMEM_LIMIT),
        name="mixer_sample",
    )(x, *layered, p['jn'], s5, conv, ssd_t, hg_t, gla_t)


FF_TILE = 512


def _mlp_kernel(final, n_seq, mix_ref, x_ref, wout_ref, g_ref, wup_ref, wdn_ref, gfin_ref, o_ref, *scratch):
    x1 = x_ref[...] + _dot(mix_ref[...], wout_ref[...])
    hn = (x1 * _rms_scale(x1, -1) * g_ref[...]).astype(BF16)
    acc = x1
    for f in range(D_FF // FF_TILE):
        up = jnp.maximum(_dot(hn, wup_ref[:, f * FF_TILE:(f + 1) * FF_TILE]), 0.0)
        acc = acc + _dot((up * up).astype(BF16), wdn_ref[f * FF_TILE:(f + 1) * FF_TILE, :])
    if final:
        acc = acc * _rms_scale(acc, -1) * gfin_ref[...]
    if n_seq is None:
        o_ref[...] = acc
    else:
        tm_ref, = scratch
        _store_tiles(tm_ref, 0, acc)
        steps = acc.shape[0] // n_seq
        for b in range(n_seq):
            for j in range(D_MODEL // 128):
                o_ref[b, :, j * 128:(j + 1) * 128] = tm_ref[j, pl.ds(b, steps, stride=n_seq), :]


def _mlp(mix, x, p, layer, final, tm, n_seq=None):
    rows_total = x.shape[0]
    tm = min(tm, rows_total)
    layered = [p['wout'], p['gmlp'], p['wup'], p['wdn']]
    consts = layered + [p['gfin']]
    if n_seq is None:
        out_spec = pl.BlockSpec((tm, D_MODEL), lambda i: (i, 0))
        out_shape = jax.ShapeDtypeStruct((rows_total, D_MODEL), F32)
        scratch = []
    else:
        out_spec = pl.BlockSpec((n_seq, tm // n_seq, D_MODEL), lambda i: (0, i, 0))
        out_shape = jax.ShapeDtypeStruct((n_seq, rows_total // n_seq, D_MODEL), F32)
        scratch = [pltpu.VMEM((D_MODEL // 128, tm, 128), F32)]
    return pl.pallas_call(
        functools.partial(_mlp_kernel, final, n_seq),
        grid=(rows_total // tm,),
        in_specs=[pl.BlockSpec((tm, D_MODEL), lambda i: (i, 0)),
                  pl.BlockSpec((tm, D_MODEL), lambda i: (i, 0))]
                 + [_layer_spec(c, layer) for c in layered] + [_const_spec(p['gfin'].shape)],
        out_specs=out_spec,
        out_shape=out_shape,
        scratch_shapes=scratch,
        compiler_params=pltpu.CompilerParams(dimension_semantics=("parallel",),
                                             vmem_limit_bytes=VMEM_LIMIT),
        name="out_mlp",
    )(mix, x, *consts)


N_IN = 2836
IN_MISC_LO, IN_LR_LO = 1024, 2820
PACK_ROWS = 256


def _pack_win_kernel(w_ref, o_ref):
    w = w_ref[...]
    n_dt, n_lr = SSD_HEADS, GLA_RANK
    o_ref[:, 0:IN_MISC_LO] = w[:, 0:IN_MISC_LO].astype(BF16)
    o_ref[:, IN_MISC_LO:O_MISC] = w[:, IN_MISC_LO + n_dt:IN_LR_LO].astype(BF16)
    tail = jnp.concatenate([w[:, IN_MISC_LO:IN_MISC_LO + n_dt], w[:, IN_LR_LO:IN_LR_LO + n_lr],
                            jnp.zeros((w.shape[0], 128 - n_dt - n_lr), w.dtype)], axis=1)
    o_ref[:, O_MISC:N_PROJ] = tail.astype(BF16)


def _pack_win(w_in):
    depth, d_in, n_in = w_in.shape
    assert n_in == N_IN and d_in % PACK_ROWS == 0
    return pl.pallas_call(
        _pack_win_kernel,
        grid=(depth, d_in // PACK_ROWS),
        in_specs=[pl.BlockSpec((None, PACK_ROWS, n_in), lambda l, i: (l, i, 0))],
        out_specs=pl.BlockSpec((None, PACK_ROWS, N_PROJ), lambda l, i: (l, i, 0)),
        out_shape=jax.ShapeDtypeStruct((depth, d_in, N_PROJ), BF16),
        compiler_params=pltpu.CompilerParams(dimension_semantics=("parallel", "parallel")),
        name="pack_win",
    )(w_in.astype(F32))


def _block_ones(n_rows, row_blk, n_cols, col_blk, scale=1.0):
    r = np.arange(n_rows)[:, None] // row_blk
    c = np.arange(n_cols)[None, :] // col_blk
    return (r == c).astype(np.float32) * scale


def _prepare(norm_mix_g, w_in, s5_lam_re, s5_lam_im, s5_log_dt, s5_b_re, s5_b_im, s5_c_re, s5_c_im,
             s5_d, s5_w_glu, s5_b_glu, ssd_conv_w, ssd_conv_b, ssd_dt_bias, ssd_a_log, ssd_d, ssd_norm_g,
             hg_lb_logits, hg_norm_g, gla_w_gk2, gla_b_gk, gla_norm_g):
    depth = w_in.shape[0]
    win = _pack_win(w_in)

    lam_re, lam_im = s5_lam_re.astype(F32), s5_lam_im.astype(F32)
    dt = jnp.exp(s5_log_dt.astype(F32))[..., None]
    ea = jnp.exp(lam_re * dt)
    lb_r, lb_i = ea * jnp.cos(lam_im * dt), ea * jnp.sin(lam_im * dt)
    den = lam_re * lam_re + lam_im * lam_im
    cr = ((lb_r - 1.0) * lam_re + lb_i * lam_im) / den
    ci = (lb_i * lam_re - (lb_r - 1.0) * lam_im) / den
    bb_r = cr[..., None] * s5_b_re - ci[..., None] * s5_b_im
    bb_i = cr[..., None] * s5_b_im + ci[..., None] * s5_b_re
    eye_g = jnp.eye(S5_GROUPS, dtype=F32)
    blk_b = lambda t: jnp.einsum('lgpc,gh->lgchp', t, eye_g).reshape(depth, GROUP_W, S5_STATE)
    blk_c = lambda t: jnp.einsum('lgcp,gh->lgphc', t, eye_g).reshape(depth, S5_STATE, GROUP_W)
    bblk = jnp.concatenate([blk_b(bb_r), blk_b(bb_i)], axis=-1)
    cblk = jnp.concatenate([blk_c(s5_c_re.astype(F32)), -blk_c(s5_c_im.astype(F32))], axis=1)
    lam = jnp.stack([lb_r.reshape(depth, S5_STATE), lb_i.reshape(depth, S5_STATE)], axis=1)

    sm = jax.nn.softmax(hg_lb_logits.astype(F32), axis=0)
    lb = jnp.cumsum(sm, axis=0) - sm[0:1]
    lb_floor = jnp.maximum(lb, LB_FLOOR)
    a = -jnp.exp(ssd_a_log.astype(F32))
    rep64 = lambda t: jnp.repeat(t, 64, axis=-1)
    zeros256 = jnp.zeros((depth, GROUP_W), F32)
    v256 = jnp.stack([s5_d, s5_b_glu, rep64(ssd_d), ssd_norm_g, lb_floor, 1.0 - lb, hg_norm_g, rep64(a),
                      jnp.tile(gla_norm_g, (1, GLA_HEADS))] + [zeros256] * 7, axis=1).astype(F32)
    pad128 = lambda t: jnp.pad(t, ((0, 0), (0, 128 - t.shape[-1])))
    zeros128 = jnp.zeros((depth, 128), F32)
    v128 = jnp.stack([pad128(ssd_dt_bias), pad128(a), gla_b_gk] + [zeros128] * 5, axis=1).astype(F32)
    v512 = jnp.concatenate([ssd_conv_w, ssd_conv_b[:, None, :], jnp.zeros((depth, 3, SSD_XBC), F32)],
                           axis=1).astype(F32)
    w2p = jnp.pad(gla_w_gk2, ((0, 0), (LR_LO, 128 - LR_LO - GLA_RANK), (0, 0))).astype(BF16)

    return dict(
        gmix=norm_mix_g[:, None, :].astype(F32), win=win, bblk=bblk.astype(BF16), lam=lam,
        cblk=cblk.astype(BF16), wglu=s5_w_glu.astype(BF16), v256=v256, v128=v128, v512=v512, w2p=w2p,
        ehd=jnp.asarray(_block_ones(128, 1, 256, 64) * (np.arange(128)[:, None] < SSD_HEADS), BF16),
        j256=jnp.asarray(_block_ones(256, 64, 256, 64), BF16),
        j128=jnp.asarray(_block_ones(128, 32, 256, 64), BF16),
        jn=jnp.asarray(_block_ones(256, 64, 256, 64, 1.0 / 64), BF16),
    )


def _diag_state(st, dk):
    nb = st.shape[0]
    s = st.reshape(nb, N_HEADS, 64, N_HEADS, dk)
    idx = jnp.arange(N_HEADS)
    s = s[:, idx, :, idx, :]
    return s.transpose(1, 0, 3, 2)


def kernel(x_prompt, x_sample, state_s5_re, state_s5_im, state_ssd_conv, state_ssd, state_hgrn, state_gla,
           norm_mix_g, w_in, s5_lam_re, s5_lam_im, s5_log_dt, s5_b_re, s5_b_im, s5_c_re, s5_c_im,
           s5_d, s5_w_glu, s5_b_glu, ssd_conv_w, ssd_conv_b, ssd_dt_bias, ssd_a_log, ssd_d, ssd_norm_g,
           hg_lb_logits, hg_norm_g, gla_w_gk2, gla_b_gk, gla_norm_g, w_out, norm_mlp_g, w_up, w_down,
           norm_final_g):
    nb, seq, _ = x_prompt.shape
    ns = x_sample.shape[0]
    depth = w_in.shape[0]
    t_steps = min(64, seq)

    p = _prepare(norm_mix_g, w_in, s5_lam_re, s5_lam_im, s5_log_dt, s5_b_re, s5_b_im, s5_c_re, s5_c_im,
                 s5_d, s5_w_glu, s5_b_glu, ssd_conv_w, ssd_conv_b, ssd_dt_bias, ssd_a_log, ssd_d,
                 ssd_norm_g, hg_lb_logits, hg_norm_g, gla_w_gk2, gla_b_gk, gla_norm_g)
    p.update(wout=w_out.astype(BF16), wup=w_up.astype(BF16), wdn=w_down.astype(BF16),
             gmlp=norm_mlp_g[:, None, :].astype(F32), gfin=norm_final_g[None, :].astype(F32))

    xp = x_prompt.astype(F32)
    xs = x_sample.astype(F32).reshape(ns, D_MODEL)

    s5s = jnp.stack([state_s5_re.reshape(depth, ns, S5_STATE), state_s5_im.reshape(depth, ns, S5_STATE)],
                    axis=1).astype(F32)
    convs = state_ssd_conv.reshape(depth, ns, (SSD_CONV - 1) * SSD_XBC).astype(F32)
    ssds = state_ssd.transpose(0, 2, 3, 4, 1).reshape(depth, SSD_HEADS * SSD_N * SSD_P, ns).astype(F32)
    hgs = state_hgrn.transpose(0, 2, 3, 4, 1).reshape(depth, HG_HEADS * HG_K * HG_V, ns).astype(F32)
    glas = state_gla.transpose(0, 2, 3, 4, 1).reshape(depth, GLA_HEADS * GLA_K * GLA_V, ns).astype(F32)

    outs_p = [[] for _ in range(6)]
    outs_s = [[] for _ in range(5)]
    for l in range(depth):
        final = l == depth - 1
        mix, s5, conv, st_ssd, st_hg, st_gla, *x_tm = _mixer_prompt(xp, nb, t_steps, p, l)
        if x_tm:
            xp, = x_tm
        xp = _mlp(mix, xp, p, l, final, nb * t_steps, n_seq=nb if final else None)
        for dst, val in zip(outs_p, (s5[0], s5[1], conv, st_ssd, st_hg, st_gla)):
            dst.append(val)

        res = _mixer_sample(xs, p, l, s5s, convs, ssds, hgs, glas)
        xs = _mlp(res[0], xs, p, l, final, 512)
        for dst, val in zip(outs_s, res[1:]):
            dst.append(val)

    y_prompt = xp
    y_sample = xs.reshape(ns, 1, D_MODEL)
    ps5r, ps5i, pconv, pssd, phg, pgla = [jnp.stack(v, axis=0) for v in outs_p]
    ss5, sconv, sssd, shg, sgla = [jnp.stack(v, axis=0) for v in outs_s]
    diag = lambda st, dk: _diag_state(st.reshape((depth * nb,) + st.shape[2:]), dk).reshape(
        depth, nb, N_HEADS, dk, 64)
    return (
        y_prompt, y_sample,
        ps5r.reshape(depth, nb, S5_GROUPS, S5_P), ps5i.reshape(depth, nb, S5_GROUPS, S5_P),
        pconv.reshape(depth, SSD_CONV - 1, nb, SSD_XBC).transpose(0, 2, 1, 3),
        diag(pssd, SSD_N), diag(phg, HG_K), diag(pgla, GLA_K),
        ss5[:, 0].reshape(depth, ns, S5_GROUPS, S5_P), ss5[:, 1].reshape(depth, ns, S5_GROUPS, S5_P),
        sconv.reshape(depth, ns, SSD_CONV - 1, SSD_XBC),
        sssd.reshape(depth, SSD_HEADS, SSD_N, SSD_P, ns).transpose(0, 4, 1, 2, 3),
        shg.reshape(depth, HG_HEADS, HG_K, HG_V, ns).transpose(0, 4, 1, 2, 3),
        sgla.reshape(depth, GLA_HEADS, GLA_K, GLA_V, ns).transpose(0, 4, 1, 2, 3),
    )
```

```python
import functools
import math

import numpy as np
import jax
import jax.numpy as jnp
from jax import lax
from jax.experimental import pallas as pl
from jax.experimental.pallas import tpu as pltpu

F32 = jnp.float32
BF16 = jnp.bfloat16

D_MODEL = 1024
GROUP_W = 256
S5_GROUPS, S5_CH, S5_P = 16, 16, 64
S5_STATE = S5_GROUPS * S5_P
SSD_HEADS, SSD_N, SSD_P, SSD_NGROUPS, SSD_CONV, SSD_XBC = 4, 64, 64, 2, 4, 512
HG_HEADS, HG_K, HG_V = 4, 64, 64
GLA_HEADS, GLA_K, GLA_V, GLA_RANK, GLA_TAU = 4, 32, 64, 16, 16.0
N_HEADS = 4
D_FF = 4096
EPS = 1e-6
LB_FLOOR = 1e-30

O_U, O_Z, O_XBC = 0, 256, 512
O_HQ, O_HF, O_HI, O_HGATE = 1024, 1280, 1536, 1792
O_GQ, O_GK, O_GV, O_GGATE = 2048, 2176, 2304, 2560
O_MISC = 2816
N_PROJ = 2944
LR_LO = SSD_HEADS

SUB = 16
NEG = -1e30
LOG2E = 1.4426950408889634
VMEM_LIMIT = 56 * 1024 * 1024


def _sigmoid(x):
    return 0.5 * (1.0 + jnp.tanh(0.5 * x))


def _sigmoid_rel(x):
    return 1.0 / (1.0 + jnp.exp(-x))


def _silu(x):
    return x * _sigmoid(x)


def _softplus(x):
    return jnp.maximum(x, 0.0) + jnp.log1p(jnp.exp(-jnp.abs(x)))


def _log_sigmoid(x):
    return -_softplus(-x)


def _gelu_tanh(x):
    c = math.sqrt(2.0 / math.pi)
    return x * (0.5 * (1.0 + jnp.tanh(c * (x + 0.044715 * (x * x * x)))))


def _rms_scale(x, axis):
    return lax.rsqrt(jnp.mean(x * x, axis=axis, keepdims=True) + EPS)


def _dot(a, b):
    return jnp.dot(a, b, preferred_element_type=F32)


def _dot_nt(a, b):
    return lax.dot_general(a, b, (((1,), (1,)), ((), ())), preferred_element_type=F32)


def _dot_tn(a, b):
    return lax.dot_general(a, b, (((0,), (0,)), ((), ())), preferred_element_type=F32)


def _dot_split(a, b16, terms):
    acc, rest = None, a
    for _ in range(terms):
        piece = rest.astype(BF16)
        part = _dot(piece, b16)
        acc = part if acc is None else acc + part
        rest = rest - piece.astype(F32)
    return acc


def _idiv(x, n):
    shift = int(math.log2(n))
    assert 1 << shift == n
    return x >> shift


def _store_tiles(ref, row0, val):
    rows = val.shape[0]
    for j in range(val.shape[1] // 128):
        ref[j, row0:row0 + rows, :] = val[:, j * 128:(j + 1) * 128]


def _load_tiles(ref, row0, rows, n_tiles):
    return jnp.concatenate([ref[j, row0:row0 + rows, :] for j in range(n_tiles)], axis=1)


def _load_seq(ref, start, t_steps, nb, n_tiles):
    tiles = [ref[j, pl.ds(start, t_steps, stride=nb), :] for j in range(n_tiles)]
    return tiles[0] if n_tiles == 1 else jnp.concatenate(tiles, axis=1)


def _load_rows(ref, start, rows, n_tiles):
    tiles = [ref[j, pl.ds(start, rows), :] for j in range(n_tiles)]
    return tiles[0] if n_tiles == 1 else jnp.concatenate(tiles, axis=1)


def _gated_scan_tile(kd_lanes, dk, nb, t_steps, qf_ref, kp_ref, cp_ref, vp_ref, ob_ref, os_ref, st_ref,
                     s16_ref, j_ref, ko_ref, vo_ref):
    rows = nb * t_steps
    sub_rows = SUB * nb
    n_sub = rows // sub_rows
    n_kt = kd_lanes // 128

    def sub_body(sb):
        r0 = pl.multiple_of(sb * sub_rows, sub_rows)
        accs = [[None] * SUB for _ in range(2)]
        for j in range(SUB):
            reps = SUB - j
            rj = pl.multiple_of(r0 + nb * j, nb)
            q = _load_rows(qf_ref, rj, reps * nb, n_kt)
            c = _load_rows(cp_ref, rj, reps * nb, n_kt)
            k_j = jnp.tile(_load_rows(kp_ref, rj, nb, n_kt), (reps, 1))
            c_j = jnp.tile(_load_rows(cp_ref, rj, nb, n_kt), (reps, 1))
            w = _dot((q * k_j * jnp.exp2(c - c_j)).astype(BF16), j_ref[...])
            for n in range(2):
                v_j = vp_ref[n, pl.ds(rj, nb), :]
                for i in range(j, SUB):
                    term = w[(i - j) * nb:(i - j + 1) * nb, n * 128:(n + 1) * 128] * v_j
                    accs[n][i] = term if accs[n][i] is None else accs[n][i] + term
            yield
        for n in range(2):
            ob_ref[n, pl.ds(r0, sub_rows), :] = jnp.concatenate(accs[n], axis=0)

    t_idx = lax.broadcasted_iota(jnp.int32, (t_steps, kd_lanes), 0)
    ri = lax.broadcasted_iota(jnp.int32, (t_steps, N_HEADS * t_steps), 0)
    ci = lax.broadcasted_iota(jnp.int32, (t_steps, N_HEADS * t_steps), 1) & (t_steps - 1)
    lane_tile_w = min(kd_lanes, 128)
    heads_per_tile = lane_tile_w // dk
    tile_head = _idiv(lax.broadcasted_iota(jnp.int32, (64, lane_tile_w), 1), dk)
    levels = []
    c_sz = SUB
    while c_sz < t_steps:
        levels.append(c_sz)
        c_sz *= 2

    def seq_body(b):
        qb = _load_seq(qf_ref, b, t_steps, nb, n_kt)
        kb = _load_seq(kp_ref, b, t_steps, nb, n_kt)
        cb = _load_seq(cp_ref, b, t_steps, nb, n_kt)
        vb16 = _load_seq(vp_ref, b, t_steps, nb, 2).astype(BF16)
        c_last = cb[t_steps - 1:t_steps, :]

        o = _dot_nt((qb * jnp.exp2(cb)).astype(BF16), s16_ref[b])
        yield
        s = None
        for c_sz in levels:
            pieces = []
            for m in range(t_steps // (2 * c_sz)):
                a_row = m * 2 * c_sz + c_sz - 1
                pieces.append(jnp.broadcast_to(cb[a_row:a_row + 1, :], (2 * c_sz, kd_lanes)))
            anchor = pieces[0] if len(pieces) == 1 else jnp.concatenate(pieces, axis=0)
            upper = (t_idx & c_sz) != 0
            q_l = (qb * jnp.exp2(jnp.where(upper, cb - anchor, NEG))).astype(BF16)
            k_l = (kb * jnp.exp2(jnp.where(upper, NEG, anchor - cb))).astype(BF16)
            s_l = _dot_nt(q_l, jnp.tile(k_l, (N_HEADS, 1)) * ko_ref[...])
            if 2 * c_sz < t_steps:
                shift = int(math.log2(2 * c_sz))
                s_l = jnp.where((ri >> shift) == (ci >> shift), s_l, 0.0)
            s = s_l if s is None else s + s_l
            yield
        if s is not None:
            o = o + _dot(s.astype(BF16), jnp.tile(vb16, (N_HEADS, 1)) * vo_ref[...])
        for j in range(2):
            os_ref[j, pl.ds(b, t_steps, stride=nb), :] = o[:, j * 128:(j + 1) * 128]
        yield

        upd = _dot_tn(vb16, (kb * jnp.exp2(c_last - cb)).astype(BF16))
        decay = jnp.exp2(c_last)
        for h in range(N_HEADS):
            r0, l0 = h * 64, (h // heads_per_tile) * lane_tile_w
            own = tile_head == (h % heads_per_tile)
            blk = (decay[:, l0:l0 + lane_tile_w] * st_ref[b, r0:r0 + 64, l0:l0 + lane_tile_w]
                   + jnp.where(own, upd[r0:r0 + 64, l0:l0 + lane_tile_w], 0.0))
            st_ref[b, r0:r0 + 64, l0:l0 + lane_tile_w] = blk
            s16_ref[b, r0:r0 + 64, l0:l0 + lane_tile_w] = blk.astype(BF16)

    assert nb % n_sub == 0
    seq_per_trip = nb // n_sub

    def trip(i, carry):
        sub = sub_body(i)
        seqs = [seq_body(i * seq_per_trip + s) for s in range(seq_per_trip)]
        turn = 0
        while sub is not None or seqs:
            if seqs:
                turn %= len(seqs)
                if next(seqs[turn], "done") == "done":
                    seqs.pop(turn)
                else:
                    turn += 1
            if sub is not None and next(sub, "done") == "done":
                sub = None
        return carry

    lax.fori_loop(0, n_sub, trip, 0, unroll=2)


def _cumsum_time(nb, t_steps, src_ref, cp_ref, kd_lanes):
    def body(t, c):
        r = pl.multiple_of(t * nb, nb)
        c = c + src_ref[pl.ds(r, nb), :kd_lanes] * LOG2E
        for j in range(kd_lanes // 128):
            cp_ref[j, pl.ds(r, nb), :] = c[:, j * 128:(j + 1) * 128]
        return c

    lax.fori_loop(0, t_steps, body, jnp.zeros((nb, kd_lanes), F32), unroll=4)


def _head_rms_scale(o, jn_ref):
    return lax.rsqrt(_dot_split(o * o, jn_ref[...], 2) + EPS)


def _mixer_prompt_kernel(nb, t_steps, batch_major_in,
                         x_ref, gmix_ref, win_ref, bblk_ref, lam_ref, cblk_ref, wglu_ref,
                         v256_ref, v128_ref, v512_ref, w2p_ref, ehd_ref, j256_ref, j128_ref, jn_ref,
                         ko256_ref, ko128_ref, wout_ref, wup_ref, wdn_ref,
                         mix_ref, s5_ref, conv_ref, st_ssd_ref, st_hg_ref, st_gla_ref,
                         wout16_ref, wup16_ref, wdn16_ref, *rest):
    wout16_ref[...] = wout_ref[...].astype(BF16)
    wup16_ref[...] = wup_ref[...].astype(BF16)
    wdn16_ref[...] = wdn_ref[...].astype(BF16)
    if batch_major_in:
        xtm_ref, xt_ref = rest[0], rest[-1]
        rest = rest[1:-1]
    (proj_ref, bu_ref, xe_ref, qf_ref, kp_ref, cp_ref, vp_ref, ob_ref, os_ref, la_ref,
     s16_ssd_ref, s16_hg_ref, s16_gla_ref, hn_ref) = rest
    rows = nb * t_steps
    conv_rows = (SSD_CONV - 1) * nb

    @pl.when(pl.program_id(0) == 0)
    def _():
        for ref in (s5_ref, st_ssd_ref, st_hg_ref, st_gla_ref, s16_ssd_ref, s16_hg_ref, s16_gla_ref):
            ref[...] = jnp.zeros_like(ref)
        xe_ref[0:conv_rows, :] = jnp.zeros((conv_rows, SSD_XBC), F32)

    if batch_major_in:
        for b in range(nb):
            for j in range(D_MODEL // 128):
                xt_ref[j, pl.ds(b, t_steps, stride=nb), :] = x_ref[b, :, j * 128:(j + 1) * 128]
        x = _load_tiles(xt_ref, 0, rows, D_MODEL // 128)
        xtm_ref[...] = x
    else:
        x = x_ref[...]
    hn_ref[...] = (x * _rms_scale(x, -1) * gmix_ref[...]).astype(BF16)

    def project(lo, hi):
        proj_ref[:, lo:hi] = _dot(hn_ref[...], win_ref[:, lo:hi])

    project(O_U, O_HQ)
    project(O_MISC, N_PROJ)

    u = proj_ref[:, O_U:O_U + GROUP_W]
    bu_ref[...] = _dot(u.astype(BF16), bblk_ref[...])
    lam_r = jnp.broadcast_to(lam_ref[0:1, :], (nb, S5_STATE))
    lam_i = jnp.broadcast_to(lam_ref[1:2, :], (nb, S5_STATE))

    def s5_step(t, h):
        h_r, h_i = h
        r = pl.multiple_of(t * nb, nb)
        n_r = lam_r * h_r - lam_i * h_i + bu_ref[pl.ds(r, nb), 0:S5_STATE]
        n_i = lam_r * h_i + lam_i * h_r + bu_ref[pl.ds(r, nb), S5_STATE:2 * S5_STATE]
        bu_ref[pl.ds(r, nb), 0:S5_STATE] = n_r
        bu_ref[pl.ds(r, nb), S5_STATE:2 * S5_STATE] = n_i
        return n_r, n_i

    h_r, h_i = lax.fori_loop(0, t_steps, s5_step, (s5_ref[0], s5_ref[1]), unroll=2)
    s5_ref[0] = h_r
    s5_ref[1] = h_i
    project(O_HQ, O_GQ)
    y = _dot(bu_ref[...].astype(BF16), cblk_ref[...]) + v256_ref[0:1, :] * u
    z = _gelu_tanh(y)
    o_s5 = z * _sigmoid(_dot(z.astype(BF16), wglu_ref[...]) + v256_ref[1:2, :])
    mix_ref[:, 0:GROUP_W] = o_s5.astype(mix_ref.dtype)

    xe_ref[conv_rows:conv_rows + rows, :] = proj_ref[:, O_XBC:O_XBC + SSD_XBC]
    conv = v512_ref[SSD_CONV:SSD_CONV + 1, :]
    for j in range(SSD_CONV):
        conv = conv + xe_ref[j * nb:j * nb + rows, :] * v512_ref[j:j + 1, :]
    tail = xe_ref[rows:rows + conv_rows, :]
    xe_ref[0:conv_rows, :] = tail
    conv_ref[...] = tail
    act = _silu(conv)
    xs = act[:, 0:GROUP_W]
    b_g = act[:, GROUP_W:GROUP_W + 128]
    c_g = act[:, GROUP_W + 128:GROUP_W + 256]
    misc = proj_ref[:, O_MISC:O_MISC + 128]
    dt = _softplus(misc + v128_ref[0:1, :])
    dt_hd = _dot_split(dt, ehd_ref[...], 3)
    la_ref[...] = dt_hd * v256_ref[7:8, :]
    _store_tiles(qf_ref, 0, jnp.concatenate(
        [c_g[:, 0:64], c_g[:, 0:64], c_g[:, 64:128], c_g[:, 64:128]], axis=1))
    _store_tiles(kp_ref, 0, jnp.concatenate(
        [b_g[:, 0:64], b_g[:, 0:64], b_g[:, 64:128], b_g[:, 64:128]], axis=1))
    _store_tiles(vp_ref, 0, xs * dt_hd)
    _cumsum_time(nb, t_steps, la_ref, cp_ref, 256)
    _gated_scan_tile(256, 64, nb, t_steps, qf_ref, kp_ref, cp_ref, vp_ref, ob_ref, os_ref, st_ssd_ref,
                     s16_ssd_ref, j256_ref, ko256_ref, ko256_ref)
    project(O_GQ, O_MISC)
    y = _load_tiles(ob_ref, 0, rows, 2) + _load_tiles(os_ref, 0, rows, 2) + v256_ref[2:3, :] * xs
    y = y * _silu(proj_ref[:, O_Z:O_Z + GROUP_W])
    y = y * _rms_scale(y, -1) * v256_ref[3:4, :]
    mix_ref[:, GROUP_W:2 * GROUP_W] = y.astype(mix_ref.dtype)

    sig_f = _sigmoid_rel(proj_ref[:, O_HF:O_HF + GROUP_W])
    nsig_f = 1.0 - sig_f
    la_ref[...] = jnp.log(sig_f + v256_ref[4:5, :] * nsig_f)
    _store_tiles(qf_ref, 0, _silu(proj_ref[:, O_HQ:O_HQ + GROUP_W]))
    _store_tiles(kp_ref, 0, v256_ref[5:6, :] * nsig_f)
    _store_tiles(vp_ref, 0, proj_ref[:, O_HI:O_HI + GROUP_W])
    _cumsum_time(nb, t_steps, la_ref, cp_ref, 256)
    _gated_scan_tile(256, 64, nb, t_steps, qf_ref, kp_ref, cp_ref, vp_ref, ob_ref, os_ref, st_hg_ref,
                     s16_hg_ref, j256_ref, ko256_ref, ko256_ref)
    o = _load_tiles(ob_ref, 0, rows, 2) + _load_tiles(os_ref, 0, rows, 2)
    o = o * _head_rms_scale(o, jn_ref) * v256_ref[6:7, :]
    o = o * _silu(proj_ref[:, O_HGATE:O_HGATE + GROUP_W])
    mix_ref[:, 2 * GROUP_W:3 * GROUP_W] = o.astype(mix_ref.dtype)

    gk = _dot(misc.astype(BF16), w2p_ref[...]) + v128_ref[2:3, :]
    la_ref[:, 0:128] = _log_sigmoid(gk) * (1.0 / GLA_TAU)
    _store_tiles(qf_ref, 0, proj_ref[:, O_GQ:O_GQ + 128] * (GLA_K ** -0.5))
    _store_tiles(kp_ref, 0, proj_ref[:, O_GK:O_GK + 128])
    _store_tiles(vp_ref, 0, proj_ref[:, O_GV:O_GV + GROUP_W])
    _cumsum_time(nb, t_steps, la_ref, cp_ref, 128)
    _gated_scan_tile(128, 32, nb, t_steps, qf_ref, kp_ref, cp_ref, vp_ref, ob_ref, os_ref, st_gla_ref,
                     s16_gla_ref, j128_ref, ko128_ref, ko256_ref)
    o = _load_tiles(ob_ref, 0, rows, 2) + _load_tiles(os_ref, 0, rows, 2)
    o = o * _head_rms_scale(o, jn_ref) * v256_ref[8:9, :]
    o = o * _silu(proj_ref[:, O_GGATE:O_GGATE + GROUP_W])
    mix_ref[:, 3 * GROUP_W:4 * GROUP_W] = o.astype(mix_ref.dtype)


def _const_spec(shape):
    nd = len(shape)
    return pl.BlockSpec(shape, lambda i, _nd=nd: (0,) * _nd, pipeline_mode=pl.Buffered(1))


def _layer_spec(arr, layer):
    nd = arr.ndim - 1
    return pl.BlockSpec((None,) + arr.shape[1:], lambda i, _nd=nd: (layer,) + (0,) * _nd,
                        pipeline_mode=pl.Buffered(1))


PROMPT_LAYER_PARAMS = ('gmix', 'win', 'bblk', 'lam', 'cblk', 'wglu', 'v256', 'v128', 'v512', 'w2p')
PROMPT_SHARED_PARAMS = ('ehd', 'j256', 'j128', 'jn')


def _mixer_prompt(x, nb, t_steps, p, layer):
    batch_major_in = x.ndim == 3
    rows_total = x.shape[0] * x.shape[1] if batch_major_in else x.shape[0]
    rows = nb * t_steps
    n_tiles = rows_total // rows
    layered = [p[k] for k in PROMPT_LAYER_PARAMS]
    shared = [p[k] for k in PROMPT_SHARED_PARAMS] + [
        jnp.asarray(_block_ones(N_HEADS * t_steps, t_steps, 256, 64), BF16),
        jnp.asarray(_block_ones(N_HEADS * t_steps, t_steps, 128, 32), BF16)]
    mlp_w = [p['wout32'], p['wup32'], p['wdn32']]
    assert all(w.shape[1] % (16 * n_tiles) == 0 for w in mlp_w)
    slab = lambda w: (w.shape[1] // n_tiles, w.shape[2])
    consts = layered + shared + mlp_w
    x_spec = (pl.BlockSpec((nb, t_steps, D_MODEL), lambda i: (0, i, 0)) if batch_major_in
              else pl.BlockSpec((rows, D_MODEL), lambda i: (i, 0)))
    in_specs = ([x_spec] + [_layer_spec(c, layer) for c in layered] + [_const_spec(c.shape) for c in shared]
                + [pl.BlockSpec((None,) + slab(w), lambda i: (layer, i, 0)) for w in mlp_w])
    conv_rows = (SSD_CONV - 1) * nb
    out_shape = (
        jax.ShapeDtypeStruct((rows_total, D_MODEL), BF16),
        jax.ShapeDtypeStruct((2, nb, S5_STATE), F32),
        jax.ShapeDtypeStruct((conv_rows, SSD_XBC), F32),
        jax.ShapeDtypeStruct((nb, 256, 256), F32),
        jax.ShapeDtypeStruct((nb, 256, 256), F32),
        jax.ShapeDtypeStruct((nb, 256, 128), F32),
    ) + tuple(jax.ShapeDtypeStruct(w.shape[1:], BF16) for w in mlp_w)
    out_specs = (
        pl.BlockSpec((rows, D_MODEL), lambda i: (i, 0)),
        pl.BlockSpec((2, nb, S5_STATE), lambda i: (0, 0, 0)),
        pl.BlockSpec((conv_rows, SSD_XBC), lambda i: (0, 0)),
        pl.BlockSpec((nb, 256, 256), lambda i: (0, 0, 0)),
        pl.BlockSpec((nb, 256, 256), lambda i: (0, 0, 0)),
        pl.BlockSpec((nb, 256, 128), lambda i: (0, 0, 0)),
    ) + tuple(pl.BlockSpec(slab(w), lambda i: (i, 0)) for w in mlp_w)
    scratch = [
        pltpu.VMEM((rows, N_PROJ), F32),
        pltpu.VMEM((rows, 2 * S5_STATE), F32),
        pltpu.VMEM((conv_rows + rows, SSD_XBC), F32),
        pltpu.VMEM((2, rows, 128), F32),
        pltpu.VMEM((2, rows, 128), F32),
        pltpu.VMEM((2, rows, 128), F32),
        pltpu.VMEM((2, rows, 128), F32),
        pltpu.VMEM((2, rows, 128), F32),
        pltpu.VMEM((2, rows, 128), F32),
        pltpu.VMEM((rows, 256), F32),
        pltpu.VMEM((nb, 256, 256), BF16),
        pltpu.VMEM((nb, 256, 256), BF16),
        pltpu.VMEM((nb, 256, 128), BF16),
        pltpu.VMEM((rows, D_MODEL), BF16),
    ]
    if batch_major_in:
        out_shape += (jax.ShapeDtypeStruct((rows_total, D_MODEL), F32),)
        out_specs += (pl.BlockSpec((rows, D_MODEL), lambda i: (i, 0)),)
        scratch.append(pltpu.VMEM((D_MODEL // 128, rows, 128), F32))
    return pl.pallas_call(
        functools.partial(_mixer_prompt_kernel, nb, t_steps, batch_major_in),
        grid=(n_tiles,),
        in_specs=in_specs,
        out_specs=out_specs,
        out_shape=out_shape,
        scratch_shapes=scratch,
        compiler_params=pltpu.CompilerParams(dimension_semantics=("arbitrary",),
                                             vmem_limit_bytes=VMEM_LIMIT),
        name="mixer_prompt",
    )(x, *consts)


SAMPLE_LAYER_PARAMS = ('gmix', 'win', 'bblk', 'lam', 'cblk', 'wglu', 'v256', 'v128', 'v512', 'w2p')


def _mixer_sample_kernel(x_ref, gmix_ref, win_ref, bblk_ref, lam_ref, cblk_ref, wglu_ref,
                         v256_ref, v128_ref, v512_ref, w2p_ref, jn_ref,
                         s5_in, conv_in, ssd_in, hg_in, gla_in,
                         mix_ref, s5_out, conv_out, ssd_out, hg_out, gla_out,
                         proj_ref, xs_ref, act_ref, dt_ref, es_ref, qh_ref, kh_ref, eh_ref, vh_ref,
                         qg_ref, kg_ref, eg_ref, vg_ref, os_ref, oh_ref, og_ref):
    h = pl.program_id(0)

    @pl.when(h == 0)
    def _():
        x = x_ref[...]
        hn = (x * _rms_scale(x, -1) * gmix_ref[...]).astype(BF16)
        proj_ref[...] = _dot(hn, win_ref[...])

        u = proj_ref[:, O_U:O_U + GROUP_W]
        bu = _dot(u.astype(BF16), bblk_ref[...])
        lam_r, lam_i = lam_ref[0:1, :], lam_ref[1:2, :]
        h0_r, h0_i = s5_in[0], s5_in[1]
        h_r = lam_r * h0_r - lam_i * h0_i + bu[:, 0:S5_STATE]
        h_i = lam_r * h0_i + lam_i * h0_r + bu[:, S5_STATE:2 * S5_STATE]
        s5_out[0] = h_r
        s5_out[1] = h_i
        y = (_dot(jnp.concatenate([h_r, h_i], axis=1).astype(BF16), cblk_ref[...])
             + v256_ref[0:1, :] * u)
        z = _gelu_tanh(y)
        o_s5 = z * _sigmoid(_dot(z.astype(BF16), wglu_ref[...]) + v256_ref[1:2, :])
        mix_ref[:, 0:GROUP_W] = o_s5.astype(mix_ref.dtype)

        xbc = proj_ref[:, O_XBC:O_XBC + SSD_XBC]
        conv = v512_ref[SSD_CONV:SSD_CONV + 1, :] + v512_ref[SSD_CONV - 1:SSD_CONV, :] * xbc
        for j in range(SSD_CONV - 1):
            conv = conv + v512_ref[j:j + 1, :] * conv_in[:, j * SSD_XBC:(j + 1) * SSD_XBC]
        conv_out[:, 0:2 * SSD_XBC] = conv_in[:, SSD_XBC:3 * SSD_XBC]
        conv_out[:, 2 * SSD_XBC:3 * SSD_XBC] = xbc
        act = _silu(conv)
        xs_ref[...] = act[:, 0:GROUP_W]
        act_ref[...] = act.T
        misc = proj_ref[:, O_MISC:O_MISC + 128]
        dt = _softplus(misc + v128_ref[0:1, :])
        dt_ref[...] = dt.T
        es_ref[...] = jnp.exp(dt * v128_ref[1:2, :]).T

        sig_f = _sigmoid_rel(proj_ref[:, O_HF:O_HF + GROUP_W])
        nsig_f = 1.0 - sig_f
        eh_ref[...] = (sig_f + v256_ref[4:5, :] * nsig_f).T
        kh_ref[...] = (v256_ref[5:6, :] * nsig_f).T
        qh_ref[...] = _silu(proj_ref[:, O_HQ:O_HQ + GROUP_W]).T
        vh_ref[...] = proj_ref[:, O_HI:O_HI + GROUP_W].T

        gk = _dot(misc.astype(BF16), w2p_ref[...]) + v128_ref[2:3, :]
        eg_ref[...] = jnp.exp(_log_sigmoid(gk) * (1.0 / GLA_TAU)).T
        kg_ref[...] = proj_ref[:, O_GK:O_GK + 128].T
        qg_ref[...] = (proj_ref[:, O_GQ:O_GQ + 128] * (GLA_K ** -0.5)).T
        vg_ref[...] = proj_ref[:, O_GV:O_GV + GROUP_W].T

    def state_step(st_in, st_out, e_row, k_row, q_row, v, n_keys):
        def body(i, acc):
            r = pl.multiple_of(i * 64, 64)
            s_new = e_row(i) * st_in[pl.ds(r, 64), :] + k_row(i) * v
            st_out[pl.ds(r, 64), :] = s_new
            return acc + q_row(i) * s_new
        return lax.fori_loop(0, n_keys, body, jnp.zeros_like(v), unroll=4)

    def head_rows():
        return pl.ds(pl.multiple_of(h * 64, 64), 64)

    g = _idiv(h, SSD_HEADS // SSD_NGROUPS)
    e_h = es_ref[pl.ds(h, 1), :]
    os_ref[head_rows(), :] = state_step(
        ssd_in, ssd_out,
        lambda i: e_h,
        lambda i: act_ref[pl.ds(GROUP_W + g * SSD_N + i, 1), :],
        lambda i: act_ref[pl.ds(GROUP_W + 128 + g * SSD_N + i, 1), :],
        act_ref[head_rows(), :] * dt_ref[pl.ds(h, 1), :], SSD_N)

    oh_ref[head_rows(), :] = state_step(
        hg_in, hg_out,
        lambda i: eh_ref[pl.ds(h * HG_K + i, 1), :],
        lambda i: kh_ref[pl.ds(h * HG_K + i, 1), :],
        lambda i: qh_ref[pl.ds(h * HG_K + i, 1), :],
        vh_ref[head_rows(), :], HG_K)

    og_ref[head_rows(), :] = state_step(
        gla_in, gla_out,
        lambda i: eg_ref[pl.ds(h * GLA_K + i, 1), :],
        lambda i: kg_ref[pl.ds(h * GLA_K + i, 1), :],
        lambda i: qg_ref[pl.ds(h * GLA_K + i, 1), :],
        vg_ref[head_rows(), :], GLA_K)

    @pl.when(h == N_HEADS - 1)
    def _():
        y = os_ref[...].T + v256_ref[2:3, :] * xs_ref[...]
        y = y * _silu(proj_ref[:, O_Z:O_Z + GROUP_W])
        mix_ref[:, GROUP_W:2 * GROUP_W] = (y * _rms_scale(y, -1) * v256_ref[3:4, :]).astype(mix_ref.dtype)
        o = oh_ref[...].T
        o = o * _head_rms_scale(o, jn_ref) * v256_ref[6:7, :] * _silu(proj_ref[:, O_HGATE:O_HGATE + GROUP_W])
        mix_ref[:, 2 * GROUP_W:3 * GROUP_W] = o.astype(mix_ref.dtype)
        o = og_ref[...].T
        o = o * _head_rms_scale(o, jn_ref) * v256_ref[8:9, :] * _silu(proj_ref[:, O_GGATE:O_GGATE + GROUP_W])
        mix_ref[:, 3 * GROUP_W:4 * GROUP_W] = o.astype(mix_ref.dtype)


def _mixer_sample(x, p, layer, s5, conv, ssd_t, hg_t, gla_t):
    nbatch = x.shape[0]
    layered = [p[k] for k in SAMPLE_LAYER_PARAMS]
    head_in = lambda a: pl.BlockSpec((None, a.shape[1] // N_HEADS, nbatch), lambda i: (layer, i, 0))
    head_blk = lambda a: pl.BlockSpec((a.shape[1] // N_HEADS, nbatch), lambda i: (i, 0))
    res_blk = lambda shape: pl.BlockSpec(shape, lambda i, _nd=len(shape): (0,) * _nd)
    out_shape = (
        jax.ShapeDtypeStruct((nbatch, D_MODEL), BF16),
        jax.ShapeDtypeStruct(s5.shape[1:], F32),
        jax.ShapeDtypeStruct(conv.shape[1:], F32),
        jax.ShapeDtypeStruct(ssd_t.shape[1:], F32),
        jax.ShapeDtypeStruct(hg_t.shape[1:], F32),
        jax.ShapeDtypeStruct(gla_t.shape[1:], F32),
    )
    fm = lambda n: pltpu.VMEM((n, nbatch), F32)
    scratch = [
        pltpu.VMEM((nbatch, N_PROJ), F32),
        pltpu.VMEM((nbatch, GROUP_W), F32),
        fm(SSD_XBC),
        fm(128), fm(128),
        fm(GROUP_W), fm(GROUP_W), fm(GROUP_W), fm(GROUP_W),
        fm(128), fm(128), fm(128), fm(GROUP_W),
        fm(GROUP_W), fm(GROUP_W), fm(GROUP_W),
    ]
    return pl.pallas_call(
        _mixer_sample_kernel,
        grid=(N_HEADS,),
        in_specs=([_const_spec(x.shape)] + [_layer_spec(c, layer) for c in layered]
                  + [_const_spec(p['jn'].shape), _layer_spec(s5, layer), _layer_spec(conv, layer),
                     head_in(ssd_t), head_in(hg_t), head_in(gla_t)]),
        out_specs=(res_blk((nbatch, D_MODEL)), res_blk(s5.shape[1:]), res_blk(conv.shape[1:]),
                   head_blk(ssd_t), head_blk(hg_t), head_blk(gla_t)),
        out_shape=out_shape,
        scratch_shapes=scratch,
        compiler_params=pltpu.CompilerParams(dimension_semantics=("arbitrary",),
                                             vmem_limit_bytes=VMEM_LIMIT),
        name="mixer_sample",
    )(x, *layered, p['jn'], s5, conv, ssd_t, hg_t, gla_t)


FF_TILE = 512


def _mlp_kernel(final, n_seq, mix_ref, x_ref, wout_ref, g_ref, wup_ref, wdn_ref, gfin_ref, o_ref, *scratch):
    x1 = x_ref[...] + _dot(mix_ref[...], wout_ref[...])
    hn = (x1 * _rms_scale(x1, -1) * g_ref[...]).astype(BF16)
    acc = x1
    for f in range(D_FF // FF_TILE):
        up = jnp.maximum(_dot(hn, wup_ref[:, f * FF_TILE:(f + 1) * FF_TILE]), 0.0)
        acc = acc + _dot((up * up).astype(BF16), wdn_ref[f * FF_TILE:(f + 1) * FF_TILE, :])
    if final:
        acc = acc * _rms_scale(acc, -1) * gfin_ref[...]
    if n_seq is None:
        o_ref[...] = acc
    else:
        tm_ref, = scratch
        _store_tiles(tm_ref, 0, acc)
        steps = acc.shape[0] // n_seq
        for b in range(n_seq):
            for j in range(D_MODEL // 128):
                o_ref[b, :, j * 128:(j + 1) * 128] = tm_ref[j, pl.ds(b, steps, stride=n_seq), :]


def _mlp(mix, x, weights16, p, layer, final, tm, n_seq=None):
    rows_total = x.shape[0]
    tm = min(tm, rows_total)
    wout, wup, wdn = weights16
    consts = [wout, p['gmlp'], wup, wdn, p['gfin']]
    specs = [_const_spec(wout.shape), _layer_spec(p['gmlp'], layer), _const_spec(wup.shape),
             _const_spec(wdn.shape), _const_spec(p['gfin'].shape)]
    if n_seq is None:
        out_spec = pl.BlockSpec((tm, D_MODEL), lambda i: (i, 0))
        out_shape = jax.ShapeDtypeStruct((rows_total, D_MODEL), F32)
        scratch = []
    else:
        out_spec = pl.BlockSpec((n_seq, tm // n_seq, D_MODEL), lambda i: (0, i, 0))
        out_shape = jax.ShapeDtypeStruct((n_seq, rows_total // n_seq, D_MODEL), F32)
        scratch = [pltpu.VMEM((D_MODEL // 128, tm, 128), F32)]
    return pl.pallas_call(
        functools.partial(_mlp_kernel, final, n_seq),
        grid=(rows_total // tm,),
        in_specs=[pl.BlockSpec((tm, D_MODEL), lambda i: (i, 0)),
                  pl.BlockSpec((tm, D_MODEL), lambda i: (i, 0))] + specs,
        out_specs=out_spec,
        out_shape=out_shape,
        scratch_shapes=scratch,
        compiler_params=pltpu.CompilerParams(dimension_semantics=("parallel",),
                                             vmem_limit_bytes=VMEM_LIMIT),
        name="out_mlp",
    )(mix, x, *consts)


N_IN = 2836
IN_MISC_LO, IN_LR_LO = 1024, 2820
PACK_ROWS = 256


def _pack_win_kernel(w_ref, o_ref):
    w = w_ref[...]
    n_dt, n_lr = SSD_HEADS, GLA_RANK
    o_ref[:, 0:IN_MISC_LO] = w[:, 0:IN_MISC_LO].astype(BF16)
    o_ref[:, IN_MISC_LO:O_MISC] = w[:, IN_MISC_LO + n_dt:IN_LR_LO].astype(BF16)
    tail = jnp.concatenate([w[:, IN_MISC_LO:IN_MISC_LO + n_dt], w[:, IN_LR_LO:IN_LR_LO + n_lr],
                            jnp.zeros((w.shape[0], 128 - n_dt - n_lr), w.dtype)], axis=1)
    o_ref[:, O_MISC:N_PROJ] = tail.astype(BF16)


def _pack_win(w_in):
    depth, d_in, n_in = w_in.shape
    assert n_in == N_IN and d_in % PACK_ROWS == 0
    return pl.pallas_call(
        _pack_win_kernel,
        grid=(depth, d_in // PACK_ROWS),
        in_specs=[pl.BlockSpec((None, PACK_ROWS, n_in), lambda l, i: (l, i, 0))],
        out_specs=pl.BlockSpec((None, PACK_ROWS, N_PROJ), lambda l, i: (l, i, 0)),
        out_shape=jax.ShapeDtypeStruct((depth, d_in, N_PROJ), BF16),
        compiler_params=pltpu.CompilerParams(dimension_semantics=("parallel", "parallel")),
        name="pack_win",
    )(w_in.astype(F32))


def _block_ones(n_rows, row_blk, n_cols, col_blk, scale=1.0):
    r = np.arange(n_rows)[:, None] // row_blk
    c = np.arange(n_cols)[None, :] // col_blk
    return (r == c).astype(np.float32) * scale


def _prepare(norm_mix_g, w_in, s5_lam_re, s5_lam_im, s5_log_dt, s5_b_re, s5_b_im, s5_c_re, s5_c_im,
             s5_d, s5_w_glu, s5_b_glu, ssd_conv_w, ssd_conv_b, ssd_dt_bias, ssd_a_log, ssd_d, ssd_norm_g,
             hg_lb_logits, hg_norm_g, gla_w_gk2, gla_b_gk, gla_norm_g):
    depth = w_in.shape[0]
    win = _pack_win(w_in)

    lam_re, lam_im = s5_lam_re.astype(F32), s5_lam_im.astype(F32)
    dt = jnp.exp(s5_log_dt.astype(F32))[..., None]
    ea = jnp.exp(lam_re * dt)
    lb_r, lb_i = ea * jnp.cos(lam_im * dt), ea * jnp.sin(lam_im * dt)
    den = lam_re * lam_re + lam_im * lam_im
    cr = ((lb_r - 1.0) * lam_re + lb_i * lam_im) / den
    ci = (lb_i * lam_re - (lb_r - 1.0) * lam_im) / den
    bb_r = cr[..., None] * s5_b_re - ci[..., None] * s5_b_im
    bb_i = cr[..., None] * s5_b_im + ci[..., None] * s5_b_re
    eye_g = jnp.eye(S5_GROUPS, dtype=F32)
    blk_b = lambda t: jnp.einsum('lgpc,gh->lgchp', t, eye_g).reshape(depth, GROUP_W, S5_STATE)
    blk_c = lambda t: jnp.einsum('lgcp,gh->lgphc', t, eye_g).reshape(depth, S5_STATE, GROUP_W)
    bblk = jnp.concatenate([blk_b(bb_r), blk_b(bb_i)], axis=-1)
    cblk = jnp.concatenate([blk_c(s5_c_re.astype(F32)), -blk_c(s5_c_im.astype(F32))], axis=1)
    lam = jnp.stack([lb_r.reshape(depth, S5_STATE), lb_i.reshape(depth, S5_STATE)], axis=1)

    sm = jax.nn.softmax(hg_lb_logits.astype(F32), axis=0)
    lb = jnp.cumsum(sm, axis=0) - sm[0:1]
    lb_floor = jnp.maximum(lb, LB_FLOOR)
    a = -jnp.exp(ssd_a_log.astype(F32))
    rep64 = lambda t: jnp.repeat(t, 64, axis=-1)
    zeros256 = jnp.zeros((depth, GROUP_W), F32)
    v256 = jnp.stack([s5_d, s5_b_glu, rep64(ssd_d), ssd_norm_g, lb_floor, 1.0 - lb, hg_norm_g, rep64(a),
                      jnp.tile(gla_norm_g, (1, GLA_HEADS))] + [zeros256] * 7, axis=1).astype(F32)
    pad128 = lambda t: jnp.pad(t, ((0, 0), (0, 128 - t.shape[-1])))
    zeros128 = jnp.zeros((depth, 128), F32)
    v128 = jnp.stack([pad128(ssd_dt_bias), pad128(a), gla_b_gk] + [zeros128] * 5, axis=1).astype(F32)
    v512 = jnp.concatenate([ssd_conv_w, ssd_conv_b[:, None, :], jnp.zeros((depth, 3, SSD_XBC), F32)],
                           axis=1).astype(F32)
    w2p = jnp.pad(gla_w_gk2, ((0, 0), (LR_LO, 128 - LR_LO - GLA_RANK), (0, 0))).astype(BF16)

    return dict(
        gmix=norm_mix_g[:, None, :].astype(F32), win=win, bblk=bblk.astype(BF16), lam=lam,
        cblk=cblk.astype(BF16), wglu=s5_w_glu.astype(BF16), v256=v256, v128=v128, v512=v512, w2p=w2p,
        ehd=jnp.asarray(_block_ones(128, 1, 256, 64) * (np.arange(128)[:, None] < SSD_HEADS), BF16),
        j256=jnp.asarray(_block_ones(256, 64, 256, 64), BF16),
        j128=jnp.asarray(_block_ones(128, 32, 256, 64), BF16),
        jn=jnp.asarray(_block_ones(256, 64, 256, 64, 1.0 / 64), BF16),
    )


def _diag_state(st, dk):
    nb = st.shape[0]
    s = st.reshape(nb, N_HEADS, 64, N_HEADS, dk)
    idx = jnp.arange(N_HEADS)
    s = s[:, idx, :, idx, :]
    return s.transpose(1, 0, 3, 2)


def kernel(x_prompt, x_sample, state_s5_re, state_s5_im, state_ssd_conv, state_ssd, state_hgrn, state_gla,
           norm_mix_g, w_in, s5_lam_re, s5_lam_im, s5_log_dt, s5_b_re, s5_b_im, s5_c_re, s5_c_im,
           s5_d, s5_w_glu, s5_b_glu, ssd_conv_w, ssd_conv_b, ssd_dt_bias, ssd_a_log, ssd_d, ssd_norm_g,
           hg_lb_logits, hg_norm_g, gla_w_gk2, gla_b_gk, gla_norm_g, w_out, norm_mlp_g, w_up, w_down,
           norm_final_g):
    nb, seq, _ = x_prompt.shape
    ns = x_sample.shape[0]
    depth = w_in.shape[0]
    t_steps = min(64, seq)

    p = _prepare(norm_mix_g, w_in, s5_lam_re, s5_lam_im, s5_log_dt, s5_b_re, s5_b_im, s5_c_re, s5_c_im,
                 s5_d, s5_w_glu, s5_b_glu, ssd_conv_w, ssd_conv_b, ssd_dt_bias, ssd_a_log, ssd_d,
                 ssd_norm_g, hg_lb_logits, hg_norm_g, gla_w_gk2, gla_b_gk, gla_norm_g)
    p.update(wout32=w_out.astype(F32), wup32=w_up.astype(F32), wdn32=w_down.astype(F32),
             gmlp=norm_mlp_g[:, None, :].astype(F32), gfin=norm_final_g[None, :].astype(F32))

    xp = x_prompt.astype(F32)
    xs = x_sample.astype(F32).reshape(ns, D_MODEL)

    s5s = jnp.stack([state_s5_re.reshape(depth, ns, S5_STATE), state_s5_im.reshape(depth, ns, S5_STATE)],
                    axis=1).astype(F32)
    convs = state_ssd_conv.reshape(depth, ns, (SSD_CONV - 1) * SSD_XBC).astype(F32)
    ssds = state_ssd.transpose(0, 2, 3, 4, 1).reshape(depth, SSD_HEADS * SSD_N * SSD_P, ns).astype(F32)
    hgs = state_hgrn.transpose(0, 2, 3, 4, 1).reshape(depth, HG_HEADS * HG_K * HG_V, ns).astype(F32)
    glas = state_gla.transpose(0, 2, 3, 4, 1).reshape(depth, GLA_HEADS * GLA_K * GLA_V, ns).astype(F32)

    outs_p = [[] for _ in range(6)]
    outs_s = [[] for _ in range(5)]
    for l in range(depth):
        final = l == depth - 1
        mix, s5, conv, st_ssd, st_hg, st_gla, *w16, = _mixer_prompt(xp, nb, t_steps, p, l)
        if len(w16) > 3:
            xp = w16.pop()
        xp = _mlp(mix, xp, w16, p, l, final, nb * t_steps, n_seq=nb if final else None)
        for dst, val in zip(outs_p, (s5[0], s5[1], conv, st_ssd, st_hg, st_gla)):
            dst.append(val)

        res = _mixer_sample(xs, p, l, s5s, convs, ssds, hgs, glas)
        xs = _mlp(res[0], xs, w16, p, l, final, 512)
        for dst, val in zip(outs_s, res[1:]):
            dst.append(val)

    y_prompt = xp
    y_sample = xs.reshape(ns, 1, D_MODEL)
    ps5r, ps5i, pconv, pssd, phg, pgla = [jnp.stack(v, axis=0) for v in outs_p]
    ss5, sconv, sssd, shg, sgla = [jnp.stack(v, axis=0) for v in outs_s]
    diag = lambda st, dk: _diag_state(st.reshape((depth * nb,) + st.shape[2:]), dk).reshape(
        depth, nb, N_HEADS, dk, 64)
    return (
        y_prompt, y_sample,
        ps5r.reshape(depth, nb, S5_GROUPS, S5_P), ps5i.reshape(depth, nb, S5_GROUPS, S5_P),
        pconv.reshape(depth, SSD_CONV - 1, nb, SSD_XBC).transpose(0, 2, 1, 3),
        diag(pssd, SSD_N), diag(phg, HG_K), diag(pgla, GLA_K),
        ss5[:, 0].reshape(depth, ns, S5_GROUPS, S5_P), ss5[:, 1].reshape(depth, ns, S5_GROUPS, S5_P),
        sconv.reshape(depth, ns, SSD_CONV - 1, SSD_XBC),
        sssd.reshape(depth, SSD_HEADS, SSD_N, SSD_P, ns).transpose(0, 4, 1, 2, 3),
        shg.reshape(depth, HG_HEADS, HG_K, HG_V, ns).transpose(0, 4, 1, 2, 3),
        sgla.reshape(depth, GLA_HEADS, GLA_K, GLA_V, ns).transpose(0, 4, 1, 2, 3),
    )
```

```python
import functools
import math

import numpy as np
import jax
import jax.numpy as jnp
from jax import lax
from jax.experimental import pallas as pl
from jax.experimental.pallas import tpu as pltpu

F32 = jnp.float32
BF16 = jnp.bfloat16

D_MODEL = 1024
GROUP_W = 256
S5_GROUPS, S5_CH, S5_P = 16, 16, 64
S5_STATE = S5_GROUPS * S5_P
SSD_HEADS, SSD_N, SSD_P, SSD_NGROUPS, SSD_CONV, SSD_XBC = 4, 64, 64, 2, 4, 512
HG_HEADS, HG_K, HG_V = 4, 64, 64
GLA_HEADS, GLA_K, GLA_V, GLA_RANK, GLA_TAU = 4, 32, 64, 16, 16.0
N_HEADS = 4
D_FF = 4096
EPS = 1e-6
LB_FLOOR = 1e-30

O_U, O_Z, O_XBC = 0, 256, 512
O_HQ, O_HF, O_HI, O_HGATE = 1024, 1280, 1536, 1792
O_GQ, O_GK, O_GV, O_GGATE = 2048, 2176, 2304, 2560
O_MISC = 2816
N_PROJ = 2944
LR_LO = SSD_HEADS

SUB = 16
NEG = -1e30
LOG2E = 1.4426950408889634
VMEM_LIMIT = 56 * 1024 * 1024


def _sigmoid(x):
    return 0.5 * (1.0 + jnp.tanh(0.5 * x))


def _sigmoid_rel(x):
    return 1.0 / (1.0 + jnp.exp(-x))


def _silu(x):
    return x * _sigmoid(x)


def _softplus(x):
    return jnp.maximum(x, 0.0) + jnp.log1p(jnp.exp(-jnp.abs(x)))


def _log_sigmoid(x):
    return -_softplus(-x)


def _gelu_tanh(x):
    c = math.sqrt(2.0 / math.pi)
    return x * (0.5 * (1.0 + jnp.tanh(c * (x + 0.044715 * (x * x * x)))))


def _rms_scale(x, axis):
    return lax.rsqrt(jnp.mean(x * x, axis=axis, keepdims=True) + EPS)


def _dot(a, b):
    return jnp.dot(a, b, preferred_element_type=F32)


def _dot_nt(a, b):
    return lax.dot_general(a, b, (((1,), (1,)), ((), ())), preferred_element_type=F32)


def _dot_tn(a, b):
    return lax.dot_general(a, b, (((0,), (0,)), ((), ())), preferred_element_type=F32)


def _dot_split(a, b16, terms):
    acc, rest = None, a
    for _ in range(terms):
        piece = rest.astype(BF16)
        part = _dot(piece, b16)
        acc = part if acc is None else acc + part
        rest = rest - piece.astype(F32)
    return acc


def _idiv(x, n):
    shift = int(math.log2(n))
    assert 1 << shift == n
    return x >> shift


def _store_tiles(ref, row0, val):
    rows = val.shape[0]
    for j in range(val.shape[1] // 128):
        ref[j, row0:row0 + rows, :] = val[:, j * 128:(j + 1) * 128]


def _load_tiles(ref, row0, rows, n_tiles):
    return jnp.concatenate([ref[j, row0:row0 + rows, :] for j in range(n_tiles)], axis=1)


def _load_seq(ref, start, t_steps, nb, n_tiles):
    tiles = [ref[j, pl.ds(start, t_steps, stride=nb), :] for j in range(n_tiles)]
    return tiles[0] if n_tiles == 1 else jnp.concatenate(tiles, axis=1)


def _load_rows(ref, start, rows, n_tiles):
    tiles = [ref[j, pl.ds(start, rows), :] for j in range(n_tiles)]
    return tiles[0] if n_tiles == 1 else jnp.concatenate(tiles, axis=1)


def _gated_scan_tile(kd_lanes, dk, nb, t_steps, qf_ref, kp_ref, cp_ref, vp_ref, ob_ref, os_ref, st_ref,
                     s16_ref, j_ref, ko_ref, vo_ref, head_decay=False):
    rows = nb * t_steps
    sub_rows = SUB * nb
    n_sub = rows // sub_rows
    n_kt = kd_lanes // 128
    assert not head_decay or dk == t_steps

    def sub_body(sb):
        r0 = pl.multiple_of(sb * sub_rows, sub_rows)
        accs = [[None] * SUB for _ in range(2)]
        for j in range(SUB):
            reps = SUB - j
            rj = pl.multiple_of(r0 + nb * j, nb)
            q = _load_rows(qf_ref, rj, reps * nb, n_kt)
            c = _load_rows(cp_ref, rj, reps * nb, n_kt)
            k_j = jnp.tile(_load_rows(kp_ref, rj, nb, n_kt), (reps, 1))
            c_j = jnp.tile(_load_rows(cp_ref, rj, nb, n_kt), (reps, 1))
            w = _dot((q * k_j * jnp.exp2(c - c_j)).astype(BF16), j_ref[...])
            for n in range(2):
                v_j = vp_ref[n, pl.ds(rj, nb), :]
                for i in range(j, SUB):
                    term = w[(i - j) * nb:(i - j + 1) * nb, n * 128:(n + 1) * 128] * v_j
                    accs[n][i] = term if accs[n][i] is None else accs[n][i] + term
            yield
        for n in range(2):
            ob_ref[n, pl.ds(r0, sub_rows), :] = jnp.concatenate(accs[n], axis=0)

    t_idx = lax.broadcasted_iota(jnp.int32, (t_steps, kd_lanes), 0)
    ri = lax.broadcasted_iota(jnp.int32, (t_steps, N_HEADS * t_steps), 0)
    ci = lax.broadcasted_iota(jnp.int32, (t_steps, N_HEADS * t_steps), 1) & (t_steps - 1)
    lane_tile_w = min(kd_lanes, 128)
    heads_per_tile = lane_tile_w // dk
    tile_head = _idiv(lax.broadcasted_iota(jnp.int32, (64, lane_tile_w), 1), dk)
    levels = []
    c_sz = SUB
    while c_sz < t_steps:
        levels.append(c_sz)
        c_sz *= 2

    def seq_body(b):
        qb = _load_seq(qf_ref, b, t_steps, nb, n_kt)
        kb = _load_seq(kp_ref, b, t_steps, nb, n_kt)
        cb = _load_seq(cp_ref, b, t_steps, nb, n_kt)
        vb16 = _load_seq(vp_ref, b, t_steps, nb, 2).astype(BF16)
        c_last = cb[t_steps - 1:t_steps, :]

        o = _dot_nt((qb * jnp.exp2(cb)).astype(BF16), s16_ref[b])
        yield
        s = None
        if head_decay:
            cum_rows = [cb[:, t * 128:(t + 1) * 128].T for t in range(n_kt)]
            cum_j = jnp.concatenate(
                [jnp.broadcast_to(cum_rows[(h * dk) // 128][(h * dk) % 128:(h * dk) % 128 + 1, :],
                                  (t_steps, t_steps)) for h in range(N_HEADS)], axis=1)
            dmat = jnp.exp2(jnp.where(ri >= ci, cb - cum_j, NEG))
            s = _dot_nt(qb.astype(BF16), jnp.tile(kb.astype(BF16), (N_HEADS, 1)) * ko_ref[...]) * dmat
            yield
        for c_sz in ([] if head_decay else levels):
            pieces = []
            for m in range(t_steps // (2 * c_sz)):
                a_row = m * 2 * c_sz + c_sz - 1
                pieces.append(jnp.broadcast_to(cb[a_row:a_row + 1, :], (2 * c_sz, kd_lanes)))
            anchor = pieces[0] if len(pieces) == 1 else jnp.concatenate(pieces, axis=0)
            upper = (t_idx & c_sz) != 0
            q_l = (qb * jnp.exp2(jnp.where(upper, cb - anchor, NEG))).astype(BF16)
            k_l = (kb * jnp.exp2(jnp.where(upper, NEG, anchor - cb))).astype(BF16)
            s_l = _dot_nt(q_l, jnp.tile(k_l, (N_HEADS, 1)) * ko_ref[...])
            if 2 * c_sz < t_steps:
                shift = int(math.log2(2 * c_sz))
                s_l = jnp.where((ri >> shift) == (ci >> shift), s_l, 0.0)
            s = s_l if s is None else s + s_l
            yield
        if s is not None:
            o = o + _dot(s.astype(BF16), jnp.tile(vb16, (N_HEADS, 1)) * vo_ref[...])
        for j in range(2):
            os_ref[j, pl.ds(b, t_steps, stride=nb), :] = o[:, j * 128:(j + 1) * 128]
        yield

        upd = _dot_tn(vb16, (kb * jnp.exp2(c_last - cb)).astype(BF16))
        decay = jnp.exp2(c_last)
        for h in range(N_HEADS):
            r0, l0 = h * 64, (h // heads_per_tile) * lane_tile_w
            own = tile_head == (h % heads_per_tile)
            blk = (decay[:, l0:l0 + lane_tile_w] * st_ref[b, r0:r0 + 64, l0:l0 + lane_tile_w]
                   + jnp.where(own, upd[r0:r0 + 64, l0:l0 + lane_tile_w], 0.0))
            st_ref[b, r0:r0 + 64, l0:l0 + lane_tile_w] = blk
            s16_ref[b, r0:r0 + 64, l0:l0 + lane_tile_w] = blk.astype(BF16)

    assert nb % n_sub == 0
    seq_per_trip = nb // n_sub

    def trip(i, carry):
        sub = None if head_decay else sub_body(i)
        seqs = [seq_body(i * seq_per_trip + s) for s in range(seq_per_trip)]
        turn = 0
        while sub is not None or seqs:
            if seqs:
                turn %= len(seqs)
                if next(seqs[turn], "done") == "done":
                    seqs.pop(turn)
                else:
                    turn += 1
            if sub is not None and next(sub, "done") == "done":
                sub = None
        return carry

    lax.fori_loop(0, n_sub, trip, 0, unroll=4 if head_decay else 2)


def _cumsum_time(nb, t_steps, src_ref, cp_ref, kd_lanes):
    def body(t, c):
        r = pl.multiple_of(t * nb, nb)
        c = c + src_ref[pl.ds(r, nb), :kd_lanes] * LOG2E
        for j in range(kd_lanes // 128):
            cp_ref[j, pl.ds(r, nb), :] = c[:, j * 128:(j + 1) * 128]
        return c

    lax.fori_loop(0, t_steps, body, jnp.zeros((nb, kd_lanes), F32), unroll=4)


def _head_rms_scale(o, jn_ref):
    return lax.rsqrt(_dot_split(o * o, jn_ref[...], 2) + EPS)


def _mixer_prompt_kernel(nb, t_steps, batch_major_in,
                         x_ref, gmix_ref, win_ref, bblk_ref, lam_ref, cblk_ref, wglu_ref,
                         v256_ref, v128_ref, v512_ref, w2p_ref, ehd_ref, j256_ref, j128_ref, jn_ref,
                         ko256_ref, ko128_ref, wout_ref, wup_ref, wdn_ref,
                         mix_ref, s5_ref, conv_ref, st_ssd_ref, st_hg_ref, st_gla_ref,
                         wout16_ref, wup16_ref, wdn16_ref, *rest):
    wout16_ref[...] = wout_ref[...].astype(BF16)
    wup16_ref[...] = wup_ref[...].astype(BF16)
    wdn16_ref[...] = wdn_ref[...].astype(BF16)
    if batch_major_in:
        xtm_ref, xt_ref = rest[0], rest[-1]
        rest = rest[1:-1]
    (proj_ref, bu_ref, xe_ref, qf_ref, kp_ref, cp_ref, vp_ref, ob_ref, os_ref, la_ref,
     s16_ssd_ref, s16_hg_ref, s16_gla_ref, hn_ref) = rest
    rows = nb * t_steps
    conv_rows = (SSD_CONV - 1) * nb

    @pl.when(pl.program_id(0) == 0)
    def _():
        for ref in (s5_ref, st_ssd_ref, st_hg_ref, st_gla_ref, s16_ssd_ref, s16_hg_ref, s16_gla_ref):
            ref[...] = jnp.zeros_like(ref)
        xe_ref[0:conv_rows, :] = jnp.zeros((conv_rows, SSD_XBC), F32)

    if batch_major_in:
        for b in range(nb):
            for j in range(D_MODEL // 128):
                xt_ref[j, pl.ds(b, t_steps, stride=nb), :] = x_ref[b, :, j * 128:(j + 1) * 128]
        x = _load_tiles(xt_ref, 0, rows, D_MODEL // 128)
        xtm_ref[...] = x
    else:
        x = x_ref[...]
    hn_ref[...] = (x * _rms_scale(x, -1) * gmix_ref[...]).astype(BF16)

    def project(lo, hi):
        proj_ref[:, lo:hi] = _dot(hn_ref[...], win_ref[:, lo:hi])

    project(O_U, O_HQ)
    project(O_MISC, N_PROJ)

    u = proj_ref[:, O_U:O_U + GROUP_W]
    bu_ref[...] = _dot(u.astype(BF16), bblk_ref[...])
    lam_r = jnp.broadcast_to(lam_ref[0:1, :], (nb, S5_STATE))
    lam_i = jnp.broadcast_to(lam_ref[1:2, :], (nb, S5_STATE))

    def s5_step(t, h):
        h_r, h_i = h
        r = pl.multiple_of(t * nb, nb)
        n_r = lam_r * h_r - lam_i * h_i + bu_ref[pl.ds(r, nb), 0:S5_STATE]
        n_i = lam_r * h_i + lam_i * h_r + bu_ref[pl.ds(r, nb), S5_STATE:2 * S5_STATE]
        bu_ref[pl.ds(r, nb), 0:S5_STATE] = n_r
        bu_ref[pl.ds(r, nb), S5_STATE:2 * S5_STATE] = n_i
        return n_r, n_i

    h_r, h_i = lax.fori_loop(0, t_steps, s5_step, (s5_ref[0], s5_ref[1]), unroll=2)
    s5_ref[0] = h_r
    s5_ref[1] = h_i
    project(O_HQ, O_GQ)
    y = _dot(bu_ref[...].astype(BF16), cblk_ref[...]) + v256_ref[0:1, :] * u
    z = _gelu_tanh(y)
    o_s5 = z * _sigmoid(_dot(z.astype(BF16), wglu_ref[...]) + v256_ref[1:2, :])
    mix_ref[:, 0:GROUP_W] = o_s5.astype(mix_ref.dtype)

    xe_ref[conv_rows:conv_rows + rows, :] = proj_ref[:, O_XBC:O_XBC + SSD_XBC]
    conv = v512_ref[SSD_CONV:SSD_CONV + 1, :]
    for j in range(SSD_CONV):
        conv = conv + xe_ref[j * nb:j * nb + rows, :] * v512_ref[j:j + 1, :]
    tail = xe_ref[rows:rows + conv_rows, :]
    xe_ref[0:conv_rows, :] = tail
    conv_ref[...] = tail
    act = _silu(conv)
    xs = act[:, 0:GROUP_W]
    b_g = act[:, GROUP_W:GROUP_W + 128]
    c_g = act[:, GROUP_W + 128:GROUP_W + 256]
    misc = proj_ref[:, O_MISC:O_MISC + 128]
    dt = _softplus(misc + v128_ref[0:1, :])
    dt_hd = _dot_split(dt, ehd_ref[...], 3)
    la_ref[...] = dt_hd * v256_ref[7:8, :]
    _store_tiles(qf_ref, 0, jnp.concatenate(
        [c_g[:, 0:64], c_g[:, 0:64], c_g[:, 64:128], c_g[:, 64:128]], axis=1))
    _store_tiles(kp_ref, 0, jnp.concatenate(
        [b_g[:, 0:64], b_g[:, 0:64], b_g[:, 64:128], b_g[:, 64:128]], axis=1))
    _store_tiles(vp_ref, 0, xs * dt_hd)
    _cumsum_time(nb, t_steps, la_ref, cp_ref, 256)
    _gated_scan_tile(256, 64, nb, t_steps, qf_ref, kp_ref, cp_ref, vp_ref, ob_ref, os_ref, st_ssd_ref,
                     s16_ssd_ref, j256_ref, ko256_ref, ko256_ref, head_decay=True)
    project(O_GQ, O_MISC)
    y = _load_tiles(os_ref, 0, rows, 2) + v256_ref[2:3, :] * xs
    y = y * _silu(proj_ref[:, O_Z:O_Z + GROUP_W])
    y = y * _rms_scale(y, -1) * v256_ref[3:4, :]
    mix_ref[:, GROUP_W:2 * GROUP_W] = y.astype(mix_ref.dtype)

    sig_f = _sigmoid_rel(proj_ref[:, O_HF:O_HF + GROUP_W])
    nsig_f = 1.0 - sig_f
    la_ref[...] = jnp.log(sig_f + v256_ref[4:5, :] * nsig_f)
    _store_tiles(qf_ref, 0, _silu(proj_ref[:, O_HQ:O_HQ + GROUP_W]))
    _store_tiles(kp_ref, 0, v256_ref[5:6, :] * nsig_f)
    _store_tiles(vp_ref, 0, proj_ref[:, O_HI:O_HI + GROUP_W])
    _cumsum_time(nb, t_steps, la_ref, cp_ref, 256)
    _gated_scan_tile(256, 64, nb, t_steps, qf_ref, kp_ref, cp_ref, vp_ref, ob_ref, os_ref, st_hg_ref,
                     s16_hg_ref, j256_ref, ko256_ref, ko256_ref)
    o = _load_tiles(ob_ref, 0, rows, 2) + _load_tiles(os_ref, 0, rows, 2)
    o = o * _head_rms_scale(o, jn_ref) * v256_ref[6:7, :]
    o = o * _silu(proj_ref[:, O_HGATE:O_HGATE + GROUP_W])
    mix_ref[:, 2 * GROUP_W:3 * GROUP_W] = o.astype(mix_ref.dtype)

    gk = _dot(misc.astype(BF16), w2p_ref[...]) + v128_ref[2:3, :]
    la_ref[:, 0:128] = _log_sigmoid(gk) * (1.0 / GLA_TAU)
    _store_tiles(qf_ref, 0, proj_ref[:, O_GQ:O_GQ + 128] * (GLA_K ** -0.5))
    _store_tiles(kp_ref, 0, proj_ref[:, O_GK:O_GK + 128])
    _store_tiles(vp_ref, 0, proj_ref[:, O_GV:O_GV + GROUP_W])
    _cumsum_time(nb, t_steps, la_ref, cp_ref, 128)
    _gated_scan_tile(128, 32, nb, t_steps, qf_ref, kp_ref, cp_ref, vp_ref, ob_ref, os_ref, st_gla_ref,
                     s16_gla_ref, j128_ref, ko128_ref, ko256_ref)
    o = _load_tiles(ob_ref, 0, rows, 2) + _load_tiles(os_ref, 0, rows, 2)
    o = o * _head_rms_scale(o, jn_ref) * v256_ref[8:9, :]
    o = o * _silu(proj_ref[:, O_GGATE:O_GGATE + GROUP_W])
    mix_ref[:, 3 * GROUP_W:4 * GROUP_W] = o.astype(mix_ref.dtype)


def _const_spec(shape):
    nd = len(shape)
    return pl.BlockSpec(shape, lambda i, _nd=nd: (0,) * _nd, pipeline_mode=pl.Buffered(1))


def _layer_spec(arr, layer):
    nd = arr.ndim - 1
    return pl.BlockSpec((None,) + arr.shape[1:], lambda i, _nd=nd: (layer,) + (0,) * _nd,
                        pipeline_mode=pl.Buffered(1))


PROMPT_LAYER_PARAMS = ('gmix', 'win', 'bblk', 'lam', 'cblk', 'wglu', 'v256', 'v128', 'v512', 'w2p')
PROMPT_SHARED_PARAMS = ('ehd', 'j256', 'j128', 'jn')


def _mixer_prompt(x, nb, t_steps, p, layer):
    batch_major_in = x.ndim == 3
    rows_total = x.shape[0] * x.shape[1] if batch_major_in else x.shape[0]
    rows = nb * t_steps
    n_tiles = rows_total // rows
    layered = [p[k] for k in PROMPT_LAYER_PARAMS]
    shared = [p[k] for k in PROMPT_SHARED_PARAMS] + [
        jnp.asarray(_block_ones(N_HEADS * t_steps, t_steps, 256, 64), BF16),
        jnp.asarray(_block_ones(N_HEADS * t_steps, t_steps, 128, 32), BF16)]
    mlp_w = [p['wout32'], p['wup32'], p['wdn32']]
    assert all(w.shape[1] % (16 * n_tiles) == 0 for w in mlp_w)
    slab = lambda w: (w.shape[1] // n_tiles, w.shape[2])
    consts = layered + shared + mlp_w
    x_spec = (pl.BlockSpec((nb, t_steps, D_MODEL), lambda i: (0, i, 0)) if batch_major_in
              else pl.BlockSpec((rows, D_MODEL), lambda i: (i, 0)))
    in_specs = ([x_spec] + [_layer_spec(c, layer) for c in layered] + [_const_spec(c.shape) for c in shared]
                + [pl.BlockSpec((None,) + slab(w), lambda i: (layer, i, 0)) for w in mlp_w])
    conv_rows = (SSD_CONV - 1) * nb
    out_shape = (
        jax.ShapeDtypeStruct((rows_total, D_MODEL), BF16),
        jax.ShapeDtypeStruct((2, nb, S5_STATE), F32),
        jax.ShapeDtypeStruct((conv_rows, SSD_XBC), F32),
        jax.ShapeDtypeStruct((nb, 256, 256), F32),
        jax.ShapeDtypeStruct((nb, 256, 256), F32),
        jax.ShapeDtypeStruct((nb, 256, 128), F32),
    ) + tuple(jax.ShapeDtypeStruct(w.shape[1:], BF16) for w in mlp_w)
    out_specs = (
        pl.BlockSpec((rows, D_MODEL), lambda i: (i, 0)),
        pl.BlockSpec((2, nb, S5_STATE), lambda i: (0, 0, 0)),
        pl.BlockSpec((conv_rows, SSD_XBC), lambda i: (0, 0)),
        pl.BlockSpec((nb, 256, 256), lambda i: (0, 0, 0)),
        pl.BlockSpec((nb, 256, 256), lambda i: (0, 0, 0)),
        pl.BlockSpec((nb, 256, 128), lambda i: (0, 0, 0)),
    ) + tuple(pl.BlockSpec(slab(w), lambda i: (i, 0)) for w in mlp_w)
    scratch = [
        pltpu.VMEM((rows, N_PROJ), F32),
        pltpu.VMEM((rows, 2 * S5_STATE), F32),
        pltpu.VMEM((conv_rows + rows, SSD_XBC), F32),
        pltpu.VMEM((2, rows, 128), F32),
        pltpu.VMEM((2, rows, 128), F32),
        pltpu.VMEM((2, rows, 128), F32),
        pltpu.VMEM((2, rows, 128), F32),
        pltpu.VMEM((2, rows, 128), F32),
        pltpu.VMEM((2, rows, 128), F32),
        pltpu.VMEM((rows, 256), F32),
        pltpu.VMEM((nb, 256, 256), BF16),
        pltpu.VMEM((nb, 256, 256), BF16),
        pltpu.VMEM((nb, 256, 128), BF16),
        pltpu.VMEM((rows, D_MODEL), BF16),
    ]
    if batch_major_in:
        out_shape += (jax.ShapeDtypeStruct((rows_total, D_MODEL), F32),)
        out_specs += (pl.BlockSpec((rows, D_MODEL), lambda i: (i, 0)),)
        scratch.append(pltpu.VMEM((D_MODEL // 128, rows, 128), F32))
    return pl.pallas_call(
        functools.partial(_mixer_prompt_kernel, nb, t_steps, batch_major_in),
        grid=(n_tiles,),
        in_specs=in_specs,
        out_specs=out_specs,
        out_shape=out_shape,
        scratch_shapes=scratch,
        compiler_params=pltpu.CompilerParams(dimension_semantics=("arbitrary",),
                                             vmem_limit_bytes=VMEM_LIMIT),
        name="mixer_prompt",
    )(x, *consts)


SAMPLE_LAYER_PARAMS = ('gmix', 'win', 'bblk', 'lam', 'cblk', 'wglu', 'v256', 'v128', 'v512', 'w2p')


def _mixer_sample_kernel(x_ref, gmix_ref, win_ref, bblk_ref, lam_ref, cblk_ref, wglu_ref,
                         v256_ref, v128_ref, v512_ref, w2p_ref, jn_ref,
                         s5_in, conv_in, ssd_in, hg_in, gla_in,
                         mix_ref, s5_out, conv_out, ssd_out, hg_out, gla_out,
                         proj_ref, xs_ref, act_ref, dt_ref, es_ref, qh_ref, kh_ref, eh_ref, vh_ref,
                         qg_ref, kg_ref, eg_ref, vg_ref, os_ref, oh_ref, og_ref):
    h = pl.program_id(0)

    @pl.when(h == 0)
    def _():
        x = x_ref[...]
        hn = (x * _rms_scale(x, -1) * gmix_ref[...]).astype(BF16)
        proj_ref[...] = _dot(hn, win_ref[...])

        u = proj_ref[:, O_U:O_U + GROUP_W]
        bu = _dot(u.astype(BF16), bblk_ref[...])
        lam_r, lam_i = lam_ref[0:1, :], lam_ref[1:2, :]
        h0_r, h0_i = s5_in[0], s5_in[1]
        h_r = lam_r * h0_r - lam_i * h0_i + bu[:, 0:S5_STATE]
        h_i = lam_r * h0_i + lam_i * h0_r + bu[:, S5_STATE:2 * S5_STATE]
        s5_out[0] = h_r
        s5_out[1] = h_i
        y = (_dot(jnp.concatenate([h_r, h_i], axis=1).astype(BF16), cblk_ref[...])
             + v256_ref[0:1, :] * u)
        z = _gelu_tanh(y)
        o_s5 = z * _sigmoid(_dot(z.astype(BF16), wglu_ref[...]) + v256_ref[1:2, :])
        mix_ref[:, 0:GROUP_W] = o_s5.astype(mix_ref.dtype)

        xbc = proj_ref[:, O_XBC:O_XBC + SSD_XBC]
        conv = v512_ref[SSD_CONV:SSD_CONV + 1, :] + v512_ref[SSD_CONV - 1:SSD_CONV, :] * xbc
        for j in range(SSD_CONV - 1):
            conv = conv + v512_ref[j:j + 1, :] * conv_in[:, j * SSD_XBC:(j + 1) * SSD_XBC]
        conv_out[:, 0:2 * SSD_XBC] = conv_in[:, SSD_XBC:3 * SSD_XBC]
        conv_out[:, 2 * SSD_XBC:3 * SSD_XBC] = xbc
        act = _silu(conv)
        xs_ref[...] = act[:, 0:GROUP_W]
        act_ref[...] = act.T
        misc = proj_ref[:, O_MISC:O_MISC + 128]
        dt = _softplus(misc + v128_ref[0:1, :])
        dt_ref[...] = dt.T
        es_ref[...] = jnp.exp(dt * v128_ref[1:2, :]).T

        sig_f = _sigmoid_rel(proj_ref[:, O_HF:O_HF + GROUP_W])
        nsig_f = 1.0 - sig_f
        eh_ref[...] = (sig_f + v256_ref[4:5, :] * nsig_f).T
        kh_ref[...] = (v256_ref[5:6, :] * nsig_f).T
        qh_ref[...] = _silu(proj_ref[:, O_HQ:O_HQ + GROUP_W]).T
        vh_ref[...] = proj_ref[:, O_HI:O_HI + GROUP_W].T

        gk = _dot(misc.astype(BF16), w2p_ref[...]) + v128_ref[2:3, :]
        eg_ref[...] = jnp.exp(_log_sigmoid(gk) * (1.0 / GLA_TAU)).T
        kg_ref[...] = proj_ref[:, O_GK:O_GK + 128].T
        qg_ref[...] = (proj_ref[:, O_GQ:O_GQ + 128] * (GLA_K ** -0.5)).T
        vg_ref[...] = proj_ref[:, O_GV:O_GV + GROUP_W].T

    def state_step(st_in, st_out, e_row, k_row, q_row, v, n_keys):
        def body(i, acc):
            r = pl.multiple_of(i * 64, 64)
            s_new = e_row(i) * st_in[pl.ds(r, 64), :] + k_row(i) * v
            st_out[pl.ds(r, 64), :] = s_new
            return acc + q_row(i) * s_new
        return lax.fori_loop(0, n_keys, body, jnp.zeros_like(v), unroll=4)

    def head_rows():
        return pl.ds(pl.multiple_of(h * 64, 64), 64)

    g = _idiv(h, SSD_HEADS // SSD_NGROUPS)
    e_h = es_ref[pl.ds(h, 1), :]
    os_ref[head_rows(), :] = state_step(
        ssd_in, ssd_out,
        lambda i: e_h,
        lambda i: act_ref[pl.ds(GROUP_W + g * SSD_N + i, 1), :],
        lambda i: act_ref[pl.ds(GROUP_W + 128 + g * SSD_N + i, 1), :],
        act_ref[head_rows(), :] * dt_ref[pl.ds(h, 1), :], SSD_N)

    oh_ref[head_rows(), :] = state_step(
        hg_in, hg_out,
        lambda i: eh_ref[pl.ds(h * HG_K + i, 1), :],
        lambda i: kh_ref[pl.ds(h * HG_K + i, 1), :],
        lambda i: qh_ref[pl.ds(h * HG_K + i, 1), :],
        vh_ref[head_rows(), :], HG_K)

    og_ref[head_rows(), :] = state_step(
        gla_in, gla_out,
        lambda i: eg_ref[pl.ds(h * GLA_K + i, 1), :],
        lambda i: kg_ref[pl.ds(h * GLA_K + i, 1), :],
        lambda i: qg_ref[pl.ds(h * GLA_K + i, 1), :],
        vg_ref[head_rows(), :], GLA_K)

    @pl.when(h == N_HEADS - 1)
    def _():
        y = os_ref[...].T + v256_ref[2:3, :] * xs_ref[...]
        y = y * _silu(proj_ref[:, O_Z:O_Z + GROUP_W])
        mix_ref[:, GROUP_W:2 * GROUP_W] = (y * _rms_scale(y, -1) * v256_ref[3:4, :]).astype(mix_ref.dtype)
        o = oh_ref[...].T
        o = o * _head_rms_scale(o, jn_ref) * v256_ref[6:7, :] * _silu(proj_ref[:, O_HGATE:O_HGATE + GROUP_W])
        mix_ref[:, 2 * GROUP_W:3 * GROUP_W] = o.astype(mix_ref.dtype)
        o = og_ref[...].T
        o = o * _head_rms_scale(o, jn_ref) * v256_ref[8:9, :] * _silu(proj_ref[:, O_GGATE:O_GGATE + GROUP_W])
        mix_ref[:, 3 * GROUP_W:4 * GROUP_W] = o.astype(mix_ref.dtype)


def _mixer_sample(x, p, layer, s5, conv, ssd_t, hg_t, gla_t):
    nbatch = x.shape[0]
    layered = [p[k] for k in SAMPLE_LAYER_PARAMS]
    head_in = lambda a: pl.BlockSpec((None, a.shape[1] // N_HEADS, nbatch), lambda i: (layer, i, 0))
    head_blk = lambda a: pl.BlockSpec((a.shape[1] // N_HEADS, nbatch), lambda i: (i, 0))
    res_blk = lambda shape: pl.BlockSpec(shape, lambda i, _nd=len(shape): (0,) * _nd)
    out_shape = (
        jax.ShapeDtypeStruct((nbatch, D_MODEL), BF16),
        jax.ShapeDtypeStruct(s5.shape[1:], F32),
        jax.ShapeDtypeStruct(conv.shape[1:], F32),
        jax.ShapeDtypeStruct(ssd_t.shape[1:], F32),
        jax.ShapeDtypeStruct(hg_t.shape[1:], F32),
        jax.ShapeDtypeStruct(gla_t.shape[1:], F32),
    )
    fm = lambda n: pltpu.VMEM((n, nbatch), F32)
    scratch = [
        pltpu.VMEM((nbatch, N_PROJ), F32),
        pltpu.VMEM((nbatch, GROUP_W), F32),
        fm(SSD_XBC),
        fm(128), fm(128),
        fm(GROUP_W), fm(GROUP_W), fm(GROUP_W), fm(GROUP_W),
        fm(128), fm(128), fm(128), fm(GROUP_W),
        fm(GROUP_W), fm(GROUP_W), fm(GROUP_W),
    ]
    return pl.pallas_call(
        _mixer_sample_kernel,
        grid=(N_HEADS,),
        in_specs=([_const_spec(x.shape)] + [_layer_spec(c, layer) for c in layered]
                  + [_const_spec(p['jn'].shape), _layer_spec(s5, layer), _layer_spec(conv, layer),
                     head_in(ssd_t), head_in(hg_t), head_in(gla_t)]),
        out_specs=(res_blk((nbatch, D_MODEL)), res_blk(s5.shape[1:]), res_blk(conv.shape[1:]),
                   head_blk(ssd_t), head_blk(hg_t), head_blk(gla_t)),
        out_shape=out_shape,
        scratch_shapes=scratch,
        compiler_params=pltpu.CompilerParams(dimension_semantics=("arbitrary",),
                                             vmem_limit_bytes=VMEM_LIMIT),
        name="mixer_sample",
    )(x, *layered, p['jn'], s5, conv, ssd_t, hg_t, gla_t)


FF_TILE = 512


def _mlp_kernel(final, n_seq, mix_ref, x_ref, wout_ref, g_ref, wup_ref, wdn_ref, gfin_ref, o_ref, *scratch):
    x1 = x_ref[...] + _dot(mix_ref[...], wout_ref[...])
    hn = (x1 * _rms_scale(x1, -1) * g_ref[...]).astype(BF16)
    acc = x1
    for f in range(D_FF // FF_TILE):
        up = jnp.maximum(_dot(hn, wup_ref[:, f * FF_TILE:(f + 1) * FF_TILE]), 0.0)
        acc = acc + _dot((up * up).astype(BF16), wdn_ref[f * FF_TILE:(f + 1) * FF_TILE, :])
    if final:
        acc = acc * _rms_scale(acc, -1) * gfin_ref[...]
    if n_seq is None:
        o_ref[...] = acc
    else:
        tm_ref, = scratch
        _store_tiles(tm_ref, 0, acc)
        steps = acc.shape[0] // n_seq
        for b in range(n_seq):
            for j in range(D_MODEL // 128):
                o_ref[b, :, j * 128:(j + 1) * 128] = tm_ref[j, pl.ds(b, steps, stride=n_seq), :]


def _mlp(mix, x, weights16, p, layer, final, tm, n_seq=None):
    rows_total = x.shape[0]
    tm = min(tm, rows_total)
    wout, wup, wdn = weights16
    consts = [wout, p['gmlp'], wup, wdn, p['gfin']]
    specs = [_const_spec(wout.shape), _layer_spec(p['gmlp'], layer), _const_spec(wup.shape),
             _const_spec(wdn.shape), _const_spec(p['gfin'].shape)]
    if n_seq is None:
        out_spec = pl.BlockSpec((tm, D_MODEL), lambda i: (i, 0))
        out_shape = jax.ShapeDtypeStruct((rows_total, D_MODEL), F32)
        scratch = []
    else:
        out_spec = pl.BlockSpec((n_seq, tm // n_seq, D_MODEL), lambda i: (0, i, 0))
        out_shape = jax.ShapeDtypeStruct((n_seq, rows_total // n_seq, D_MODEL), F32)
        scratch = [pltpu.VMEM((D_MODEL // 128, tm, 128), F32)]
    return pl.pallas_call(
        functools.partial(_mlp_kernel, final, n_seq),
        grid=(rows_total // tm,),
        in_specs=[pl.BlockSpec((tm, D_MODEL), lambda i: (i, 0)),
                  pl.BlockSpec((tm, D_MODEL), lambda i: (i, 0))] + specs,
        out_specs=out_spec,
        out_shape=out_shape,
        scratch_shapes=scratch,
        compiler_params=pltpu.CompilerParams(dimension_semantics=("parallel",),
                                             vmem_limit_bytes=VMEM_LIMIT),
        name="out_mlp",
    )(mix, x, *consts)


N_IN = 2836
IN_MISC_LO, IN_LR_LO = 1024, 2820
PACK_ROWS = 256


def _pack_win_kernel(w_ref, o_ref):
    w = w_ref[...]
    n_dt, n_lr = SSD_HEADS, GLA_RANK
    o_ref[:, 0:IN_MISC_LO] = w[:, 0:IN_MISC_LO].astype(BF16)
    o_ref[:, IN_MISC_LO:O_MISC] = w[:, IN_MISC_LO + n_dt:IN_LR_LO].astype(BF16)
    tail = jnp.concatenate([w[:, IN_MISC_LO:IN_MISC_LO + n_dt], w[:, IN_LR_LO:IN_LR_LO + n_lr],
                            jnp.zeros((w.shape[0], 128 - n_dt - n_lr), w.dtype)], axis=1)
    o_ref[:, O_MISC:N_PROJ] = tail.astype(BF16)


def _pack_win(w_in):
    depth, d_in, n_in = w_in.shape
    assert n_in == N_IN and d_in % PACK_ROWS == 0
    return pl.pallas_call(
        _pack_win_kernel,
        grid=(depth, d_in // PACK_ROWS),
        in_specs=[pl.BlockSpec((None, PACK_ROWS, n_in), lambda l, i: (l, i, 0))],
        out_specs=pl.BlockSpec((None, PACK_ROWS, N_PROJ), lambda l, i: (l, i, 0)),
        out_shape=jax.ShapeDtypeStruct((depth, d_in, N_PROJ), BF16),
        compiler_params=pltpu.CompilerParams(dimension_semantics=("parallel", "parallel")),
        name="pack_win",
    )(w_in.astype(F32))


def _block_ones(n_rows, row_blk, n_cols, col_blk, scale=1.0):
    r = np.arange(n_rows)[:, None] // row_blk
    c = np.arange(n_cols)[None, :] // col_blk
    return (r == c).astype(np.float32) * scale


def _prepare(norm_mix_g, w_in, s5_lam_re, s5_lam_im, s5_log_dt, s5_b_re, s5_b_im, s5_c_re, s5_c_im,
             s5_d, s5_w_glu, s5_b_glu, ssd_conv_w, ssd_conv_b, ssd_dt_bias, ssd_a_log, ssd_d, ssd_norm_g,
             hg_lb_logits, hg_norm_g, gla_w_gk2, gla_b_gk, gla_norm_g):
    depth = w_in.shape[0]
    win = _pack_win(w_in)

    lam_re, lam_im = s5_lam_re.astype(F32), s5_lam_im.astype(F32)
    dt = jnp.exp(s5_log_dt.astype(F32))[..., None]
    ea = jnp.exp(lam_re * dt)
    lb_r, lb_i = ea * jnp.cos(lam_im * dt), ea * jnp.sin(lam_im * dt)
    den = lam_re * lam_re + lam_im * lam_im
    cr = ((lb_r - 1.0) * lam_re + lb_i * lam_im) / den
    ci = (lb_i * lam_re - (lb_r - 1.0) * lam_im) / den
    bb_r = cr[..., None] * s5_b_re - ci[..., None] * s5_b_im
    bb_i = cr[..., None] * s5_b_im + ci[..., None] * s5_b_re
    eye_g = jnp.eye(S5_GROUPS, dtype=F32)
    blk_b = lambda t: jnp.einsum('lgpc,gh->lgchp', t, eye_g).reshape(depth, GROUP_W, S5_STATE)
    blk_c = lambda t: jnp.einsum('lgcp,gh->lgphc', t, eye_g).reshape(depth, S5_STATE, GROUP_W)
    bblk = jnp.concatenate([blk_b(bb_r), blk_b(bb_i)], axis=-1)
    cblk = jnp.concatenate([blk_c(s5_c_re.astype(F32)), -blk_c(s5_c_im.astype(F32))], axis=1)
    lam = jnp.stack([lb_r.reshape(depth, S5_STATE), lb_i.reshape(depth, S5_STATE)], axis=1)

    sm = jax.nn.softmax(hg_lb_logits.astype(F32), axis=0)
    lb = jnp.cumsum(sm, axis=0) - sm[0:1]
    lb_floor = jnp.maximum(lb, LB_FLOOR)
    a = -jnp.exp(ssd_a_log.astype(F32))
    rep64 = lambda t: jnp.repeat(t, 64, axis=-1)
    zeros256 = jnp.zeros((depth, GROUP_W), F32)
    v256 = jnp.stack([s5_d, s5_b_glu, rep64(ssd_d), ssd_norm_g, lb_floor, 1.0 - lb, hg_norm_g, rep64(a),
                      jnp.tile(gla_norm_g, (1, GLA_HEADS))] + [zeros256] * 7, axis=1).astype(F32)
    pad128 = lambda t: jnp.pad(t, ((0, 0), (0, 128 - t.shape[-1])))
    zeros128 = jnp.zeros((depth, 128), F32)
    v128 = jnp.stack([pad128(ssd_dt_bias), pad128(a), gla_b_gk] + [zeros128] * 5, axis=1).astype(F32)
    v512 = jnp.concatenate([ssd_conv_w, ssd_conv_b[:, None, :], jnp.zeros((depth, 3, SSD_XBC), F32)],
                           axis=1).astype(F32)
    w2p = jnp.pad(gla_w_gk2, ((0, 0), (LR_LO, 128 - LR_LO - GLA_RANK), (0, 0))).astype(BF16)

    return dict(
        gmix=norm_mix_g[:, None, :].astype(F32), win=win, bblk=bblk.astype(BF16), lam=lam,
        cblk=cblk.astype(BF16), wglu=s5_w_glu.astype(BF16), v256=v256, v128=v128, v512=v512, w2p=w2p,
        ehd=jnp.asarray(_block_ones(128, 1, 256, 64) * (np.arange(128)[:, None] < SSD_HEADS), BF16),
        j256=jnp.asarray(_block_ones(256, 64, 256, 64), BF16),
        j128=jnp.asarray(_block_ones(128, 32, 256, 64), BF16),
        jn=jnp.asarray(_block_ones(256, 64, 256, 64, 1.0 / 64), BF16),
    )


def _diag_state(st, dk):
    nb = st.shape[0]
    s = st.reshape(nb, N_HEADS, 64, N_HEADS, dk)
    idx = jnp.arange(N_HEADS)
    s = s[:, idx, :, idx, :]
    return s.transpose(1, 0, 3, 2)


def kernel(x_prompt, x_sample, state_s5_re, state_s5_im, state_ssd_conv, state_ssd, state_hgrn, state_gla,
           norm_mix_g, w_in, s5_lam_re, s5_lam_im, s5_log_dt, s5_b_re, s5_b_im, s5_c_re, s5_c_im,
           s5_d, s5_w_glu, s5_b_glu, ssd_conv_w, ssd_conv_b, ssd_dt_bias, ssd_a_log, ssd_d, ssd_norm_g,
           hg_lb_logits, hg_norm_g, gla_w_gk2, gla_b_gk, gla_norm_g, w_out, norm_mlp_g, w_up, w_down,
           norm_final_g):
    nb, seq, _ = x_prompt.shape
    ns = x_sample.shape[0]
    depth = w_in.shape[0]
    t_steps = min(64, seq)

    p = _prepare(norm_mix_g, w_in, s5_lam_re, s5_lam_im, s5_log_dt, s5_b_re, s5_b_im, s5_c_re, s5_c_im,
                 s5_d, s5_w_glu, s5_b_glu, ssd_conv_w, ssd_conv_b, ssd_dt_bias, ssd_a_log, ssd_d,
                 ssd_norm_g, hg_lb_logits, hg_norm_g, gla_w_gk2, gla_b_gk, gla_norm_g)
    p.update(wout32=w_out.astype(F32), wup32=w_up.astype(F32), wdn32=w_down.astype(F32),
             gmlp=norm_mlp_g[:, None, :].astype(F32), gfin=norm_final_g[None, :].astype(F32))

    xp = x_prompt.astype(F32)
    xs = x_sample.astype(F32).reshape(ns, D_MODEL)

    s5s = jnp.stack([state_s5_re.reshape(depth, ns, S5_STATE), state_s5_im.reshape(depth, ns, S5_STATE)],
                    axis=1).astype(F32)
    convs = state_ssd_conv.reshape(depth, ns, (SSD_CONV - 1) * SSD_XBC).astype(F32)
    ssds = state_ssd.transpose(0, 2, 3, 4, 1).reshape(depth, SSD_HEADS * SSD_N * SSD_P, ns).astype(F32)
    hgs = state_hgrn.transpose(0, 2, 3, 4, 1).reshape(depth, HG_HEADS * HG_K * HG_V, ns).astype(F32)
    glas = state_gla.transpose(0, 2, 3, 4, 1).reshape(depth, GLA_HEADS * GLA_K * GLA_V, ns).astype(F32)

    outs_p = [[] for _ in range(6)]
    outs_s = [[] for _ in range(5)]
    for l in range(depth):
        final = l == depth - 1
        mix, s5, conv, st_ssd, st_hg, st_gla, *w16, = _mixer_prompt(xp, nb, t_steps, p, l)
        if len(w16) > 3:
            xp = w16.pop()
        xp = _mlp(mix, xp, w16, p, l, final, nb * t_steps, n_seq=nb if final else None)
        for dst, val in zip(outs_p, (s5[0], s5[1], conv, st_ssd, st_hg, st_gla)):
            dst.append(val)

        res = _mixer_sample(xs, p, l, s5s, convs, ssds, hgs, glas)
        xs = _mlp(res[0], xs, w16, p, l, final, 512)
        for dst, val in zip(outs_s, res[1:]):
            dst.append(val)

    y_prompt = xp
    y_sample = xs.reshape(ns, 1, D_MODEL)
    ps5r, ps5i, pconv, pssd, phg, pgla = [jnp.stack(v, axis=0) for v in outs_p]
    ss5, sconv, sssd, shg, sgla = [jnp.stack(v, axis=0) for v in outs_s]
    diag = lambda st, dk: _diag_state(st.reshape((depth * nb,) + st.shape[2:]), dk).reshape(
        depth, nb, N_HEADS, dk, 64)
    return (
        y_prompt, y_sample,
        ps5r.reshape(depth, nb, S5_GROUPS, S5_P), ps5i.reshape(depth, nb, S5_GROUPS, S5_P),
        pconv.reshape(depth, SSD_CONV - 1, nb, SSD_XBC).transpose(0, 2, 1, 3),
        diag(pssd, SSD_N), diag(phg, HG_K), diag(pgla, GLA_K),
        ss5[:, 0].reshape(depth, ns, S5_GROUPS, S5_P), ss5[:, 1].reshape(depth, ns, S5_GROUPS, S5_P),
        sconv.reshape(depth, ns, SSD_CONV - 1, SSD_XBC),
        sssd.reshape(depth, SSD_HEADS, SSD_N, SSD_P, ns).transpose(0, 4, 1, 2, 3),
        shg.reshape(depth, HG_HEADS, HG_K, HG_V, ns).transpose(0, 4, 1, 2, 3),
        sgla.reshape(depth, GLA_HEADS, GLA_K, GLA_V, ns).transpose(0, 4, 1, 2, 3),
    )
```

```python
import functools
import math

import numpy as np
import jax
import jax.numpy as jnp
from jax import lax
from jax.experimental import pallas as pl
from jax.experimental.pallas import tpu as pltpu

F32 = jnp.float32
BF16 = jnp.bfloat16

D_MODEL = 1024
GROUP_W = 256
S5_GROUPS, S5_CH, S5_P = 16, 16, 64
S5_STATE = S5_GROUPS * S5_P
SSD_HEADS, SSD_N, SSD_P, SSD_NGROUPS, SSD_CONV, SSD_XBC = 4, 64, 64, 2, 4, 512
HG_HEADS, HG_K, HG_V = 4, 64, 64
GLA_HEADS, GLA_K, GLA_V, GLA_RANK, GLA_TAU = 4, 32, 64, 16, 16.0
N_HEADS = 4
D_FF = 4096
EPS = 1e-6
LB_FLOOR = 1e-30

O_U, O_Z, O_XBC = 0, 256, 512
O_HQ, O_HF, O_HI, O_HGATE = 1024, 1280, 1536, 1792
O_GQ, O_GK, O_GV, O_GGATE = 2048, 2176, 2304, 2560
O_MISC = 2816
N_PROJ = 2944
LR_LO = SSD_HEADS

SUB = 16
NEG = -1e30
LOG2E = 1.4426950408889634
V7X_VMEM_BYTES = 64 * 1024 * 1024
VMEM_LIMIT = V7X_VMEM_BYTES * 7 // 8


def _sigmoid(x):
    return 0.5 * (1.0 + jnp.tanh(0.5 * x))


def _sigmoid_rel(x):
    return 1.0 / (1.0 + jnp.exp(-x))


def _silu(x):
    return x * _sigmoid(x)


def _softplus(x):
    return jnp.maximum(x, 0.0) + jnp.log1p(jnp.exp(-jnp.abs(x)))


def _log_sigmoid(x):
    return -_softplus(-x)


def _gelu_tanh(x):
    c = math.sqrt(2.0 / math.pi)
    return x * (0.5 * (1.0 + jnp.tanh(c * (x + 0.044715 * (x * x * x)))))


def _rms_scale(x, axis):
    return lax.rsqrt(jnp.mean(x * x, axis=axis, keepdims=True) + EPS)


def _dot(a, b):
    return jnp.dot(a, b, preferred_element_type=F32)


def _dot_nt(a, b):
    return lax.dot_general(a, b, (((1,), (1,)), ((), ())), preferred_element_type=F32)


def _dot_tn(a, b):
    return lax.dot_general(a, b, (((0,), (0,)), ((), ())), preferred_element_type=F32)


def _dot_split(a, b16, terms):
    acc, rest = None, a
    for _ in range(terms):
        piece = rest.astype(BF16)
        part = _dot(piece, b16)
        acc = part if acc is None else acc + part
        rest = rest - piece.astype(F32)
    return acc


def _idiv(x, n):
    shift = int(math.log2(n))
    assert 1 << shift == n
    return x >> shift


def _store_tiles(ref, row0, val):
    rows = val.shape[0]
    for j in range(val.shape[1] // 128):
        ref[j, row0:row0 + rows, :] = val[:, j * 128:(j + 1) * 128]


def _load_tiles(ref, row0, rows, n_tiles):
    return jnp.concatenate([ref[j, row0:row0 + rows, :] for j in range(n_tiles)], axis=1)


def _load_seq(ref, start, t_steps, nb, n_tiles):
    tiles = [ref[j, pl.ds(start, t_steps, stride=nb), :] for j in range(n_tiles)]
    return tiles[0] if n_tiles == 1 else jnp.concatenate(tiles, axis=1)


def _load_rows(ref, start, rows, n_tiles):
    tiles = [ref[j, pl.ds(start, rows), :] for j in range(n_tiles)]
    return tiles[0] if n_tiles == 1 else jnp.concatenate(tiles, axis=1)


def _gated_scan_tile(kd_lanes, dk, nb, t_steps, qf_ref, kp_ref, cp_ref, vp_ref, ob_ref, os_ref, st_ref,
                     s16_ref, j_ref, ko_ref, vo_ref, head_decay=False):
    rows = nb * t_steps
    sub_rows = SUB * nb
    n_sub = rows // sub_rows
    n_kt = kd_lanes // 128
    assert not head_decay or dk == t_steps

    def sub_body(sb):
        r0 = pl.multiple_of(sb * sub_rows, sub_rows)
        accs = [[None] * SUB for _ in range(2)]
        for j in range(SUB):
            reps = SUB - j
            rj = pl.multiple_of(r0 + nb * j, nb)
            q = _load_rows(qf_ref, rj, reps * nb, n_kt)
            c = _load_rows(cp_ref, rj, reps * nb, n_kt)
            k_j = jnp.tile(_load_rows(kp_ref, rj, nb, n_kt), (reps, 1))
            c_j = jnp.tile(_load_rows(cp_ref, rj, nb, n_kt), (reps, 1))
            w = _dot((q * k_j * jnp.exp2(c - c_j)).astype(BF16), j_ref[...])
            for n in range(2):
                v_j = vp_ref[n, pl.ds(rj, nb), :]
                for i in range(j, SUB):
                    term = w[(i - j) * nb:(i - j + 1) * nb, n * 128:(n + 1) * 128] * v_j
                    accs[n][i] = term if accs[n][i] is None else accs[n][i] + term
            yield
        for n in range(2):
            ob_ref[n, pl.ds(r0, sub_rows), :] = jnp.concatenate(accs[n], axis=0)

    t_idx = lax.broadcasted_iota(jnp.int32, (t_steps, kd_lanes), 0)
    ri = lax.broadcasted_iota(jnp.int32, (t_steps, N_HEADS * t_steps), 0)
    ci = lax.broadcasted_iota(jnp.int32, (t_steps, N_HEADS * t_steps), 1) & (t_steps - 1)
    lane_tile_w = min(kd_lanes, 128)
    heads_per_tile = lane_tile_w // dk
    tile_head = _idiv(lax.broadcasted_iota(jnp.int32, (64, lane_tile_w), 1), dk)
    levels = []
    c_sz = SUB
    while c_sz < t_steps:
        levels.append(c_sz)
        c_sz *= 2

    def seq_body(b):
        qb = _load_seq(qf_ref, b, t_steps, nb, n_kt)
        kb = _load_seq(kp_ref, b, t_steps, nb, n_kt)
        cb = _load_seq(cp_ref, b, t_steps, nb, n_kt)
        vb16 = _load_seq(vp_ref, b, t_steps, nb, 2).astype(BF16)
        c_last = cb[t_steps - 1:t_steps, :]

        o = _dot_nt((qb * jnp.exp2(cb)).astype(BF16), s16_ref[b])
        yield
        s = None
        if head_decay:
            cum_rows = [cb[:, t * 128:(t + 1) * 128].T for t in range(n_kt)]
            cum_j = jnp.concatenate(
                [jnp.broadcast_to(cum_rows[(h * dk) // 128][(h * dk) % 128:(h * dk) % 128 + 1, :],
                                  (t_steps, t_steps)) for h in range(N_HEADS)], axis=1)
            dmat = jnp.exp2(jnp.where(ri >= ci, cb - cum_j, NEG))
            s = _dot_nt(qb.astype(BF16), jnp.tile(kb.astype(BF16), (N_HEADS, 1)) * ko_ref[...]) * dmat
            yield
        for c_sz in ([] if head_decay else levels):
            pieces = []
            for m in range(t_steps // (2 * c_sz)):
                a_row = m * 2 * c_sz + c_sz - 1
                pieces.append(jnp.broadcast_to(cb[a_row:a_row + 1, :], (2 * c_sz, kd_lanes)))
            anchor = pieces[0] if len(pieces) == 1 else jnp.concatenate(pieces, axis=0)
            upper = (t_idx & c_sz) != 0
            q_l = (qb * jnp.exp2(jnp.where(upper, cb - anchor, NEG))).astype(BF16)
            k_l = (kb * jnp.exp2(jnp.where(upper, NEG, anchor - cb))).astype(BF16)
            s_l = _dot_nt(q_l, jnp.tile(k_l, (N_HEADS, 1)) * ko_ref[...])
            if 2 * c_sz < t_steps:
                shift = int(math.log2(2 * c_sz))
                s_l = jnp.where((ri >> shift) == (ci >> shift), s_l, 0.0)
            s = s_l if s is None else s + s_l
            yield
        if s is not None:
            o = o + _dot(s.astype(BF16), jnp.tile(vb16, (N_HEADS, 1)) * vo_ref[...])
        for j in range(2):
            os_ref[j, pl.ds(b, t_steps, stride=nb), :] = o[:, j * 128:(j + 1) * 128]
        yield

        upd = _dot_tn(vb16, (kb * jnp.exp2(c_last - cb)).astype(BF16))
        decay = jnp.exp2(c_last)
        for h in range(N_HEADS):
            r0, l0 = h * 64, (h // heads_per_tile) * lane_tile_w
            own = tile_head == (h % heads_per_tile)
            blk = (decay[:, l0:l0 + lane_tile_w] * st_ref[b, r0:r0 + 64, l0:l0 + lane_tile_w]
                   + jnp.where(own, upd[r0:r0 + 64, l0:l0 + lane_tile_w], 0.0))
            st_ref[b, r0:r0 + 64, l0:l0 + lane_tile_w] = blk
            s16_ref[b, r0:r0 + 64, l0:l0 + lane_tile_w] = blk.astype(BF16)

    assert nb % n_sub == 0
    seq_per_trip = nb // n_sub

    def trip(i, carry):
        sub = None if head_decay else sub_body(i)
        seqs = [seq_body(i * seq_per_trip + s) for s in range(seq_per_trip)]
        turn = 0
        while sub is not None or seqs:
            if seqs:
                turn %= len(seqs)
                if next(seqs[turn], "done") == "done":
                    seqs.pop(turn)
                else:
                    turn += 1
            if sub is not None and next(sub, "done") == "done":
                sub = None
        return carry

    lax.fori_loop(0, n_sub, trip, 0, unroll=4 if head_decay else 2)


def _cumsum_time(nb, t_steps, src_ref, cp_ref, kd_lanes):
    def body(t, c):
        r = pl.multiple_of(t * nb, nb)
        c = c + src_ref[pl.ds(r, nb), :kd_lanes] * LOG2E
        for j in range(kd_lanes // 128):
            cp_ref[j, pl.ds(r, nb), :] = c[:, j * 128:(j + 1) * 128]
        return c

    lax.fori_loop(0, t_steps, body, jnp.zeros((nb, kd_lanes), F32), unroll=4)


def _head_rms_scale(o, jn_ref):
    return lax.rsqrt(_dot_split(o * o, jn_ref[...], 2) + EPS)


def _mixer_prompt_kernel(nb, t_steps, batch_major_in,
                         x_ref, gmix_ref, win_ref, bblk_ref, lam_ref, cblk_ref, wglu_ref,
                         v256_ref, v128_ref, v512_ref, w2p_ref, ehd_ref, j256_ref, j128_ref, jn_ref,
                         ko256_ref, ko128_ref, wout_ref, wup_ref, wdn_ref,
                         mix_ref, s5_ref, conv_ref, st_ssd_ref, st_hg_ref, st_gla_ref,
                         wout16_ref, wup16_ref, wdn16_ref, *rest):
    wout16_ref[...] = wout_ref[...].astype(BF16)
    wup16_ref[...] = wup_ref[...].astype(BF16)
    wdn16_ref[...] = wdn_ref[...].astype(BF16)
    if batch_major_in:
        xtm_ref, xt_ref = rest[0], rest[-1]
        rest = rest[1:-1]
    (proj_ref, bu_ref, xe_ref, qf_ref, kp_ref, cp_ref, vp_ref, ob_ref, os_ref, la_ref,
     s16_ssd_ref, s16_hg_ref, s16_gla_ref, hn_ref) = rest
    rows = nb * t_steps
    conv_rows = (SSD_CONV - 1) * nb

    @pl.when(pl.program_id(0) == 0)
    def _():
        for ref in (s5_ref, st_ssd_ref, st_hg_ref, st_gla_ref, s16_ssd_ref, s16_hg_ref, s16_gla_ref):
            ref[...] = jnp.zeros_like(ref)
        xe_ref[0:conv_rows, :] = jnp.zeros((conv_rows, SSD_XBC), F32)

    if batch_major_in:
        for b in range(nb):
            for j in range(D_MODEL // 128):
                xt_ref[j, pl.ds(b, t_steps, stride=nb), :] = x_ref[b, :, j * 128:(j + 1) * 128]
        x = _load_tiles(xt_ref, 0, rows, D_MODEL // 128)
        xtm_ref[...] = x
    else:
        x = x_ref[...]
    hn_ref[...] = (x * _rms_scale(x, -1) * gmix_ref[...]).astype(BF16)

    def project(lo, hi):
        proj_ref[:, lo:hi] = _dot(hn_ref[...], win_ref[:, lo:hi])

    project(O_U, O_HQ)
    project(O_MISC, N_PROJ)

    u = proj_ref[:, O_U:O_U + GROUP_W]
    bu_ref[...] = _dot(u.astype(BF16), bblk_ref[...])
    lam_r = jnp.broadcast_to(lam_ref[0:1, :], (nb, S5_STATE))
    lam_i = jnp.broadcast_to(lam_ref[1:2, :], (nb, S5_STATE))

    def s5_step(t, h):
        h_r, h_i = h
        r = pl.multiple_of(t * nb, nb)
        n_r = lam_r * h_r - lam_i * h_i + bu_ref[pl.ds(r, nb), 0:S5_STATE]
        n_i = lam_r * h_i + lam_i * h_r + bu_ref[pl.ds(r, nb), S5_STATE:2 * S5_STATE]
        bu_ref[pl.ds(r, nb), 0:S5_STATE] = n_r
        bu_ref[pl.ds(r, nb), S5_STATE:2 * S5_STATE] = n_i
        return n_r, n_i

    h_r, h_i = lax.fori_loop(0, t_steps, s5_step, (s5_ref[0], s5_ref[1]), unroll=2)
    s5_ref[0] = h_r
    s5_ref[1] = h_i
    y = _dot(bu_ref[...].astype(BF16), cblk_ref[...]) + v256_ref[0:1, :] * u
    z = _gelu_tanh(y)
    o_s5 = z * _sigmoid(_dot(z.astype(BF16), wglu_ref[...]) + v256_ref[1:2, :])
    mix_ref[:, 0:GROUP_W] = o_s5.astype(mix_ref.dtype)

    xe_ref[conv_rows:conv_rows + rows, :] = proj_ref[:, O_XBC:O_XBC + SSD_XBC]
    conv = v512_ref[SSD_CONV:SSD_CONV + 1, :]
    for j in range(SSD_CONV):
        conv = conv + xe_ref[j * nb:j * nb + rows, :] * v512_ref[j:j + 1, :]
    tail = xe_ref[rows:rows + conv_rows, :]
    xe_ref[0:conv_rows, :] = tail
    conv_ref[...] = tail
    act = _silu(conv)
    xs = act[:, 0:GROUP_W]
    b_g = act[:, GROUP_W:GROUP_W + 128]
    c_g = act[:, GROUP_W + 128:GROUP_W + 256]
    misc = proj_ref[:, O_MISC:O_MISC + 128]
    dt = _softplus(misc + v128_ref[0:1, :])
    dt_hd = _dot_split(dt, ehd_ref[...], 3)
    la_ref[...] = dt_hd * v256_ref[7:8, :]
    _store_tiles(qf_ref, 0, jnp.concatenate(
        [c_g[:, 0:64], c_g[:, 0:64], c_g[:, 64:128], c_g[:, 64:128]], axis=1))
    _store_tiles(kp_ref, 0, jnp.concatenate(
        [b_g[:, 0:64], b_g[:, 0:64], b_g[:, 64:128], b_g[:, 64:128]], axis=1))
    _store_tiles(vp_ref, 0, xs * dt_hd)
    _cumsum_time(nb, t_steps, la_ref, cp_ref, 256)
    _gated_scan_tile(256, 64, nb, t_steps, qf_ref, kp_ref, cp_ref, vp_ref, ob_ref, os_ref, st_ssd_ref,
                     s16_ssd_ref, j256_ref, ko256_ref, ko256_ref, head_decay=True)
    project(O_HQ, O_MISC)
    y = _load_tiles(os_ref, 0, rows, 2) + v256_ref[2:3, :] * xs
    y = y * _silu(proj_ref[:, O_Z:O_Z + GROUP_W])
    y = y * _rms_scale(y, -1) * v256_ref[3:4, :]
    mix_ref[:, GROUP_W:2 * GROUP_W] = y.astype(mix_ref.dtype)

    sig_f = _sigmoid_rel(proj_ref[:, O_HF:O_HF + GROUP_W])
    nsig_f = 1.0 - sig_f
    la_ref[...] = jnp.log(sig_f + v256_ref[4:5, :] * nsig_f)
    _store_tiles(qf_ref, 0, _silu(proj_ref[:, O_HQ:O_HQ + GROUP_W]))
    _store_tiles(kp_ref, 0, v256_ref[5:6, :] * nsig_f)
    _store_tiles(vp_ref, 0, proj_ref[:, O_HI:O_HI + GROUP_W])
    _cumsum_time(nb, t_steps, la_ref, cp_ref, 256)
    _gated_scan_tile(256, 64, nb, t_steps, qf_ref, kp_ref, cp_ref, vp_ref, ob_ref, os_ref, st_hg_ref,
                     s16_hg_ref, j256_ref, ko256_ref, ko256_ref)
    o = _load_tiles(ob_ref, 0, rows, 2) + _load_tiles(os_ref, 0, rows, 2)
    o = o * _head_rms_scale(o, jn_ref) * v256_ref[6:7, :]
    o = o * _silu(proj_ref[:, O_HGATE:O_HGATE + GROUP_W])
    mix_ref[:, 2 * GROUP_W:3 * GROUP_W] = o.astype(mix_ref.dtype)

    gk = _dot(misc.astype(BF16), w2p_ref[...]) + v128_ref[2:3, :]
    la_ref[:, 0:128] = _log_sigmoid(gk) * (1.0 / GLA_TAU)
    _store_tiles(qf_ref, 0, proj_ref[:, O_GQ:O_GQ + 128] * (GLA_K ** -0.5))
    _store_tiles(kp_ref, 0, proj_ref[:, O_GK:O_GK + 128])
    _store_tiles(vp_ref, 0, proj_ref[:, O_GV:O_GV + GROUP_W])
    _cumsum_time(nb, t_steps, la_ref, cp_ref, 128)
    _gated_scan_tile(128, 32, nb, t_steps, qf_ref, kp_ref, cp_ref, vp_ref, ob_ref, os_ref, st_gla_ref,
                     s16_gla_ref, j128_ref, ko128_ref, ko256_ref)
    o = _load_tiles(ob_ref, 0, rows, 2) + _load_tiles(os_ref, 0, rows, 2)
    o = o * _head_rms_scale(o, jn_ref) * v256_ref[8:9, :]
    o = o * _silu(proj_ref[:, O_GGATE:O_GGATE + GROUP_W])
    mix_ref[:, 3 * GROUP_W:4 * GROUP_W] = o.astype(mix_ref.dtype)


def _const_spec(shape):
    nd = len(shape)
    return pl.BlockSpec(shape, lambda i, _nd=nd: (0,) * _nd, pipeline_mode=pl.Buffered(1))


def _layer_spec(arr, layer):
    nd = arr.ndim - 1
    return pl.BlockSpec((None,) + arr.shape[1:], lambda i, _nd=nd: (layer,) + (0,) * _nd,
                        pipeline_mode=pl.Buffered(1))


PROMPT_LAYER_PARAMS = ('gmix', 'win', 'bblk', 'lam', 'cblk', 'wglu', 'v256', 'v128', 'v512', 'w2p')
PROMPT_SHARED_PARAMS = ('ehd', 'j256', 'j128', 'jn')


def _mixer_prompt(x, nb, t_steps, p, layer):
    batch_major_in = x.ndim == 3
    rows_total = x.shape[0] * x.shape[1] if batch_major_in else x.shape[0]
    rows = nb * t_steps
    n_tiles = rows_total // rows
    layered = [p[k] for k in PROMPT_LAYER_PARAMS]
    shared = [p[k] for k in PROMPT_SHARED_PARAMS] + [
        jnp.asarray(_block_ones(N_HEADS * t_steps, t_steps, 256, 64), BF16),
        jnp.asarray(_block_ones(N_HEADS * t_steps, t_steps, 128, 32), BF16)]
    mlp_w = [p['wout32'], p['wup32'], p['wdn32']]
    assert all(w.shape[1] % (16 * n_tiles) == 0 for w in mlp_w)
    slab = lambda w: (w.shape[1] // n_tiles, w.shape[2])
    consts = layered + shared + mlp_w
    x_spec = (pl.BlockSpec((nb, t_steps, D_MODEL), lambda i: (0, i, 0)) if batch_major_in
              else pl.BlockSpec((rows, D_MODEL), lambda i: (i, 0)))
    in_specs = ([x_spec] + [_layer_spec(c, layer) for c in layered] + [_const_spec(c.shape) for c in shared]
                + [pl.BlockSpec((None,) + slab(w), lambda i: (layer, i, 0)) for w in mlp_w])
    conv_rows = (SSD_CONV - 1) * nb
    out_shape = (
        jax.ShapeDtypeStruct((rows_total, D_MODEL), BF16),
        jax.ShapeDtypeStruct((2, nb, S5_STATE), F32),
        jax.ShapeDtypeStruct((conv_rows, SSD_XBC), F32),
        jax.ShapeDtypeStruct((nb, 256, 256), F32),
        jax.ShapeDtypeStruct((nb, 256, 256), F32),
        jax.ShapeDtypeStruct((nb, 256, 128), F32),
    ) + tuple(jax.ShapeDtypeStruct(w.shape[1:], BF16) for w in mlp_w)
    out_specs = (
        pl.BlockSpec((rows, D_MODEL), lambda i: (i, 0)),
        pl.BlockSpec((2, nb, S5_STATE), lambda i: (0, 0, 0)),
        pl.BlockSpec((conv_rows, SSD_XBC), lambda i: (0, 0)),
        pl.BlockSpec((nb, 256, 256), lambda i: (0, 0, 0)),
        pl.BlockSpec((nb, 256, 256), lambda i: (0, 0, 0)),
        pl.BlockSpec((nb, 256, 128), lambda i: (0, 0, 0)),
    ) + tuple(pl.BlockSpec(slab(w), lambda i: (i, 0)) for w in mlp_w)
    scratch = [
        pltpu.VMEM((rows, N_PROJ), F32),
        pltpu.VMEM((rows, 2 * S5_STATE), F32),
        pltpu.VMEM((conv_rows + rows, SSD_XBC), F32),
        pltpu.VMEM((2, rows, 128), F32),
        pltpu.VMEM((2, rows, 128), F32),
        pltpu.VMEM((2, rows, 128), F32),
        pltpu.VMEM((2, rows, 128), F32),
        pltpu.VMEM((2, rows, 128), F32),
        pltpu.VMEM((2, rows, 128), F32),
        pltpu.VMEM((rows, 256), F32),
        pltpu.VMEM((nb, 256, 256), BF16),
        pltpu.VMEM((nb, 256, 256), BF16),
        pltpu.VMEM((nb, 256, 128), BF16),
        pltpu.VMEM((rows, D_MODEL), BF16),
    ]
    if batch_major_in:
        out_shape += (jax.ShapeDtypeStruct((rows_total, D_MODEL), F32),)
        out_specs += (pl.BlockSpec((rows, D_MODEL), lambda i: (i, 0)),)
        scratch.append(pltpu.VMEM((D_MODEL // 128, rows, 128), F32))
    return pl.pallas_call(
        functools.partial(_mixer_prompt_kernel, nb, t_steps, batch_major_in),
        grid=(n_tiles,),
        in_specs=in_specs,
        out_specs=out_specs,
        out_shape=out_shape,
        scratch_shapes=scratch,
        compiler_params=pltpu.CompilerParams(dimension_semantics=("arbitrary",),
                                             vmem_limit_bytes=VMEM_LIMIT),
        name="mixer_prompt",
    )(x, *consts)


SAMPLE_LAYER_PARAMS = ('gmix', 'win', 'bblk', 'lam', 'cblk', 'wglu', 'v256', 'v128', 'v512', 'w2p')


def _mixer_sample_kernel(x_ref, gmix_ref, win_ref, bblk_ref, lam_ref, cblk_ref, wglu_ref,
                         v256_ref, v128_ref, v512_ref, w2p_ref, jn_ref,
                         s5_in, conv_in, ssd_in, hg_in, gla_in,
                         mix_ref, s5_out, conv_out, ssd_out, hg_out, gla_out,
                         proj_ref, xs_ref, act_ref, dt_ref, es_ref, qh_ref, kh_ref, eh_ref, vh_ref,
                         qg_ref, kg_ref, eg_ref, vg_ref, os_ref, oh_ref, og_ref):
    h = pl.program_id(0)

    @pl.when(h == 0)
    def _():
        x = x_ref[...]
        hn = (x * _rms_scale(x, -1) * gmix_ref[...]).astype(BF16)
        proj_ref[...] = _dot(hn, win_ref[...])

        u = proj_ref[:, O_U:O_U + GROUP_W]
        bu = _dot(u.astype(BF16), bblk_ref[...])
        lam_r, lam_i = lam_ref[0:1, :], lam_ref[1:2, :]
        h0_r, h0_i = s5_in[0], s5_in[1]
        h_r = lam_r * h0_r - lam_i * h0_i + bu[:, 0:S5_STATE]
        h_i = lam_r * h0_i + lam_i * h0_r + bu[:, S5_STATE:2 * S5_STATE]
        s5_out[0] = h_r
        s5_out[1] = h_i
        y = (_dot(jnp.concatenate([h_r, h_i], axis=1).astype(BF16), cblk_ref[...])
             + v256_ref[0:1, :] * u)
        z = _gelu_tanh(y)
        o_s5 = z * _sigmoid(_dot(z.astype(BF16), wglu_ref[...]) + v256_ref[1:2, :])
        mix_ref[:, 0:GROUP_W] = o_s5.astype(mix_ref.dtype)

        xbc = proj_ref[:, O_XBC:O_XBC + SSD_XBC]
        conv = v512_ref[SSD_CONV:SSD_CONV + 1, :] + v512_ref[SSD_CONV - 1:SSD_CONV, :] * xbc
        for j in range(SSD_CONV - 1):
            conv = conv + v512_ref[j:j + 1, :] * conv_in[:, j * SSD_XBC:(j + 1) * SSD_XBC]
        conv_out[:, 0:2 * SSD_XBC] = conv_in[:, SSD_XBC:3 * SSD_XBC]
        conv_out[:, 2 * SSD_XBC:3 * SSD_XBC] = xbc
        act = _silu(conv)
        xs_ref[...] = act[:, 0:GROUP_W]
        act_ref[...] = act.T
        misc = proj_ref[:, O_MISC:O_MISC + 128]
        dt = _softplus(misc + v128_ref[0:1, :])
        dt_ref[...] = dt.T
        es_ref[...] = jnp.exp(dt * v128_ref[1:2, :]).T

        sig_f = _sigmoid_rel(proj_ref[:, O_HF:O_HF + GROUP_W])
        nsig_f = 1.0 - sig_f
        eh_ref[...] = (sig_f + v256_ref[4:5, :] * nsig_f).T
        kh_ref[...] = (v256_ref[5:6, :] * nsig_f).T
        qh_ref[...] = _silu(proj_ref[:, O_HQ:O_HQ + GROUP_W]).T
        vh_ref[...] = proj_ref[:, O_HI:O_HI + GROUP_W].T

        gk = _dot(misc.astype(BF16), w2p_ref[...]) + v128_ref[2:3, :]
        eg_ref[...] = jnp.exp(_log_sigmoid(gk) * (1.0 / GLA_TAU)).T
        kg_ref[...] = proj_ref[:, O_GK:O_GK + 128].T
        qg_ref[...] = (proj_ref[:, O_GQ:O_GQ + 128] * (GLA_K ** -0.5)).T
        vg_ref[...] = proj_ref[:, O_GV:O_GV + GROUP_W].T

    def state_step(st_in, st_out, e_row, k_row, q_row, v, n_keys):
        def body(i, acc):
            r = pl.multiple_of(i * 64, 64)
            s_new = e_row(i) * st_in[pl.ds(r, 64), :] + k_row(i) * v
            st_out[pl.ds(r, 64), :] = s_new
            return acc + q_row(i) * s_new
        return lax.fori_loop(0, n_keys, body, jnp.zeros_like(v), unroll=4)

    def head_rows():
        return pl.ds(pl.multiple_of(h * 64, 64), 64)

    g = _idiv(h, SSD_HEADS // SSD_NGROUPS)
    e_h = es_ref[pl.ds(h, 1), :]
    os_ref[head_rows(), :] = state_step(
        ssd_in, ssd_out,
        lambda i: e_h,
        lambda i: act_ref[pl.ds(GROUP_W + g * SSD_N + i, 1), :],
        lambda i: act_ref[pl.ds(GROUP_W + 128 + g * SSD_N + i, 1), :],
        act_ref[head_rows(), :] * dt_ref[pl.ds(h, 1), :], SSD_N)

    oh_ref[head_rows(), :] = state_step(
        hg_in, hg_out,
        lambda i: eh_ref[pl.ds(h * HG_K + i, 1), :],
        lambda i: kh_ref[pl.ds(h * HG_K + i, 1), :],
        lambda i: qh_ref[pl.ds(h * HG_K + i, 1), :],
        vh_ref[head_rows(), :], HG_K)

    og_ref[head_rows(), :] = state_step(
        gla_in, gla_out,
        lambda i: eg_ref[pl.ds(h * GLA_K + i, 1), :],
        lambda i: kg_ref[pl.ds(h * GLA_K + i, 1), :],
        lambda i: qg_ref[pl.ds(h * GLA_K + i, 1), :],
        vg_ref[head_rows(), :], GLA_K)

    @pl.when(h == N_HEADS - 1)
    def _():
        y = os_ref[...].T + v256_ref[2:3, :] * xs_ref[...]
        y = y * _silu(proj_ref[:, O_Z:O_Z + GROUP_W])
        mix_ref[:, GROUP_W:2 * GROUP_W] = (y * _rms_scale(y, -1) * v256_ref[3:4, :]).astype(mix_ref.dtype)
        o = oh_ref[...].T
        o = o * _head_rms_scale(o, jn_ref) * v256_ref[6:7, :] * _silu(proj_ref[:, O_HGATE:O_HGATE + GROUP_W])
        mix_ref[:, 2 * GROUP_W:3 * GROUP_W] = o.astype(mix_ref.dtype)
        o = og_ref[...].T
        o = o * _head_rms_scale(o, jn_ref) * v256_ref[8:9, :] * _silu(proj_ref[:, O_GGATE:O_GGATE + GROUP_W])
        mix_ref[:, 3 * GROUP_W:4 * GROUP_W] = o.astype(mix_ref.dtype)


def _mixer_sample(x, p, layer, s5, conv, ssd_t, hg_t, gla_t):
    nbatch = x.shape[0]
    layered = [p[k] for k in SAMPLE_LAYER_PARAMS]
    head_in = lambda a: pl.BlockSpec((None, a.shape[1] // N_HEADS, nbatch), lambda i: (layer, i, 0))
    head_blk = lambda a: pl.BlockSpec((a.shape[1] // N_HEADS, nbatch), lambda i: (i, 0))
    res_blk = lambda shape: pl.BlockSpec(shape, lambda i, _nd=len(shape): (0,) * _nd)
    out_shape = (
        jax.ShapeDtypeStruct((nbatch, D_MODEL), BF16),
        jax.ShapeDtypeStruct(s5.shape[1:], F32),
        jax.ShapeDtypeStruct(conv.shape[1:], F32),
        jax.ShapeDtypeStruct(ssd_t.shape[1:], F32),
        jax.ShapeDtypeStruct(hg_t.shape[1:], F32),
        jax.ShapeDtypeStruct(gla_t.shape[1:], F32),
    )
    fm = lambda n: pltpu.VMEM((n, nbatch), F32)
    scratch = [
        pltpu.VMEM((nbatch, N_PROJ), F32),
        pltpu.VMEM((nbatch, GROUP_W), F32),
        fm(SSD_XBC),
        fm(128), fm(128),
        fm(GROUP_W), fm(GROUP_W), fm(GROUP_W), fm(GROUP_W),
        fm(128), fm(128), fm(128), fm(GROUP_W),
        fm(GROUP_W), fm(GROUP_W), fm(GROUP_W),
    ]
    return pl.pallas_call(
        _mixer_sample_kernel,
        grid=(N_HEADS,),
        in_specs=([_const_spec(x.shape)] + [_layer_spec(c, layer) for c in layered]
                  + [_const_spec(p['jn'].shape), _layer_spec(s5, layer), _layer_spec(conv, layer),
                     head_in(ssd_t), head_in(hg_t), head_in(gla_t)]),
        out_specs=(res_blk((nbatch, D_MODEL)), res_blk(s5.shape[1:]), res_blk(conv.shape[1:]),
                   head_blk(ssd_t), head_blk(hg_t), head_blk(gla_t)),
        out_shape=out_shape,
        scratch_shapes=scratch,
        compiler_params=pltpu.CompilerParams(dimension_semantics=("arbitrary",),
                                             vmem_limit_bytes=VMEM_LIMIT),
        name="mixer_sample",
    )(x, *layered, p['jn'], s5, conv, ssd_t, hg_t, gla_t)


FF_TILE = 512


def _mlp_kernel(final, n_seq, mix_ref, x_ref, wout_ref, g_ref, wup_ref, wdn_ref, gfin_ref, o_ref, *scratch):
    x1 = x_ref[...] + _dot(mix_ref[...], wout_ref[...])
    hn = (x1 * _rms_scale(x1, -1) * g_ref[...]).astype(BF16)
    acc = x1
    for f in range(D_FF // FF_TILE):
        up = jnp.maximum(_dot(hn, wup_ref[:, f * FF_TILE:(f + 1) * FF_TILE]), 0.0)
        acc = acc + _dot((up * up).astype(BF16), wdn_ref[f * FF_TILE:(f + 1) * FF_TILE, :])
    if final:
        acc = acc * _rms_scale(acc, -1) * gfin_ref[...]
    if n_seq is None:
        o_ref[...] = acc
    else:
        tm_ref, = scratch
        _store_tiles(tm_ref, 0, acc)
        steps = acc.shape[0] // n_seq
        for b in range(n_seq):
            for j in range(D_MODEL // 128):
                o_ref[b, :, j * 128:(j + 1) * 128] = tm_ref[j, pl.ds(b, steps, stride=n_seq), :]


def _mlp(mix, x, weights16, p, layer, final, tm, n_seq=None):
    rows_total = x.shape[0]
    tm = min(tm, rows_total)
    wout, wup, wdn = weights16
    consts = [wout, p['gmlp'], wup, wdn, p['gfin']]
    specs = [_const_spec(wout.shape), _layer_spec(p['gmlp'], layer), _const_spec(wup.shape),
             _const_spec(wdn.shape), _const_spec(p['gfin'].shape)]
    if n_seq is None:
        out_spec = pl.BlockSpec((tm, D_MODEL), lambda i: (i, 0))
        out_shape = jax.ShapeDtypeStruct((rows_total, D_MODEL), F32)
        scratch = []
    else:
        out_spec = pl.BlockSpec((n_seq, tm // n_seq, D_MODEL), lambda i: (0, i, 0))
        out_shape = jax.ShapeDtypeStruct((n_seq, rows_total // n_seq, D_MODEL), F32)
        scratch = [pltpu.VMEM((D_MODEL // 128, tm, 128), F32)]
    return pl.pallas_call(
        functools.partial(_mlp_kernel, final, n_seq),
        grid=(rows_total // tm,),
        in_specs=[pl.BlockSpec((tm, D_MODEL), lambda i: (i, 0)),
                  pl.BlockSpec((tm, D_MODEL), lambda i: (i, 0))] + specs,
        out_specs=out_spec,
        out_shape=out_shape,
        scratch_shapes=scratch,
        compiler_params=pltpu.CompilerParams(dimension_semantics=("parallel",),
                                             vmem_limit_bytes=VMEM_LIMIT),
        name="out_mlp",
    )(mix, x, *consts)


N_IN = 2836
IN_MISC_LO, IN_LR_LO = 1024, 2820
PACK_ROWS = 256


def _pack_win_kernel(w_ref, o_ref):
    w = w_ref[...]
    n_dt, n_lr = SSD_HEADS, GLA_RANK
    o_ref[:, 0:IN_MISC_LO] = w[:, 0:IN_MISC_LO].astype(BF16)
    o_ref[:, IN_MISC_LO:O_MISC] = w[:, IN_MISC_LO + n_dt:IN_LR_LO].astype(BF16)
    tail = jnp.concatenate([w[:, IN_MISC_LO:IN_MISC_LO + n_dt], w[:, IN_LR_LO:IN_LR_LO + n_lr],
                            jnp.zeros((w.shape[0], 128 - n_dt - n_lr), w.dtype)], axis=1)
    o_ref[:, O_MISC:N_PROJ] = tail.astype(BF16)


def _pack_win(w_in):
    depth, d_in, n_in = w_in.shape
    assert n_in == N_IN and d_in % PACK_ROWS == 0
    return pl.pallas_call(
        _pack_win_kernel,
        grid=(depth, d_in // PACK_ROWS),
        in_specs=[pl.BlockSpec((None, PACK_ROWS, n_in), lambda l, i: (l, i, 0))],
        out_specs=pl.BlockSpec((None, PACK_ROWS, N_PROJ), lambda l, i: (l, i, 0)),
        out_shape=jax.ShapeDtypeStruct((depth, d_in, N_PROJ), BF16),
        compiler_params=pltpu.CompilerParams(dimension_semantics=("parallel", "parallel")),
        name="pack_win",
    )(w_in.astype(F32))


def _block_ones(n_rows, row_blk, n_cols, col_blk, scale=1.0):
    r = np.arange(n_rows)[:, None] // row_blk
    c = np.arange(n_cols)[None, :] // col_blk
    return (r == c).astype(np.float32) * scale


def _prepare(norm_mix_g, w_in, s5_lam_re, s5_lam_im, s5_log_dt, s5_b_re, s5_b_im, s5_c_re, s5_c_im,
             s5_d, s5_w_glu, s5_b_glu, ssd_conv_w, ssd_conv_b, ssd_dt_bias, ssd_a_log, ssd_d, ssd_norm_g,
             hg_lb_logits, hg_norm_g, gla_w_gk2, gla_b_gk, gla_norm_g):
    depth = w_in.shape[0]
    win = _pack_win(w_in)

    lam_re, lam_im = s5_lam_re.astype(F32), s5_lam_im.astype(F32)
    dt = jnp.exp(s5_log_dt.astype(F32))[..., None]
    ea = jnp.exp(lam_re * dt)
    lb_r, lb_i = ea * jnp.cos(lam_im * dt), ea * jnp.sin(lam_im * dt)
    den = lam_re * lam_re + lam_im * lam_im
    cr = ((lb_r - 1.0) * lam_re + lb_i * lam_im) / den
    ci = (lb_i * lam_re - (lb_r - 1.0) * lam_im) / den
    bb_r = cr[..., None] * s5_b_re - ci[..., None] * s5_b_im
    bb_i = cr[..., None] * s5_b_im + ci[..., None] * s5_b_re
    eye_g = jnp.eye(S5_GROUPS, dtype=F32)
    blk_b = lambda t: jnp.einsum('lgpc,gh->lgchp', t, eye_g).reshape(depth, GROUP_W, S5_STATE)
    blk_c = lambda t: jnp.einsum('lgcp,gh->lgphc', t, eye_g).reshape(depth, S5_STATE, GROUP_W)
    bblk = jnp.concatenate([blk_b(bb_r), blk_b(bb_i)], axis=-1)
    cblk = jnp.concatenate([blk_c(s5_c_re.astype(F32)), -blk_c(s5_c_im.astype(F32))], axis=1)
    lam = jnp.stack([lb_r.reshape(depth, S5_STATE), lb_i.reshape(depth, S5_STATE)], axis=1)

    sm = jax.nn.softmax(hg_lb_logits.astype(F32), axis=0)
    lb = jnp.cumsum(sm, axis=0) - sm[0:1]
    lb_floor = jnp.maximum(lb, LB_FLOOR)
    a = -jnp.exp(ssd_a_log.astype(F32))
    rep64 = lambda t: jnp.repeat(t, 64, axis=-1)
    zeros256 = jnp.zeros((depth, GROUP_W), F32)
    v256 = jnp.stack([s5_d, s5_b_glu, rep64(ssd_d), ssd_norm_g, lb_floor, 1.0 - lb, hg_norm_g, rep64(a),
                      jnp.tile(gla_norm_g, (1, GLA_HEADS))] + [zeros256] * 7, axis=1).astype(F32)
    pad128 = lambda t: jnp.pad(t, ((0, 0), (0, 128 - t.shape[-1])))
    zeros128 = jnp.zeros((depth, 128), F32)
    v128 = jnp.stack([pad128(ssd_dt_bias), pad128(a), gla_b_gk] + [zeros128] * 5, axis=1).astype(F32)
    v512 = jnp.concatenate([ssd_conv_w, ssd_conv_b[:, None, :], jnp.zeros((depth, 3, SSD_XBC), F32)],
                           axis=1).astype(F32)
    w2p = jnp.pad(gla_w_gk2, ((0, 0), (LR_LO, 128 - LR_LO - GLA_RANK), (0, 0))).astype(BF16)

    return dict(
        gmix=norm_mix_g[:, None, :].astype(F32), win=win, bblk=bblk.astype(BF16), lam=lam,
        cblk=cblk.astype(BF16), wglu=s5_w_glu.astype(BF16), v256=v256, v128=v128, v512=v512, w2p=w2p,
        ehd=jnp.asarray(_block_ones(128, 1, 256, 64) * (np.arange(128)[:, None] < SSD_HEADS), BF16),
        j256=jnp.asarray(_block_ones(256, 64, 256, 64), BF16),
        j128=jnp.asarray(_block_ones(128, 32, 256, 64), BF16),
        jn=jnp.asarray(_block_ones(256, 64, 256, 64, 1.0 / 64), BF16),
    )


def _diag_state(st, dk):
    nb = st.shape[0]
    s = st.reshape(nb, N_HEADS, 64, N_HEADS, dk)
    idx = jnp.arange(N_HEADS)
    s = s[:, idx, :, idx, :]
    return s.transpose(1, 0, 3, 2)


def kernel(x_prompt, x_sample, state_s5_re, state_s5_im, state_ssd_conv, state_ssd, state_hgrn, state_gla,
           norm_mix_g, w_in, s5_lam_re, s5_lam_im, s5_log_dt, s5_b_re, s5_b_im, s5_c_re, s5_c_im,
           s5_d, s5_w_glu, s5_b_glu, ssd_conv_w, ssd_conv_b, ssd_dt_bias, ssd_a_log, ssd_d, ssd_norm_g,
           hg_lb_logits, hg_norm_g, gla_w_gk2, gla_b_gk, gla_norm_g, w_out, norm_mlp_g, w_up, w_down,
           norm_final_g):
    nb, seq, _ = x_prompt.shape
    ns = x_sample.shape[0]
    depth = w_in.shape[0]
    t_steps = min(64, seq)

    p = _prepare(norm_mix_g, w_in, s5_lam_re, s5_lam_im, s5_log_dt, s5_b_re, s5_b_im, s5_c_re, s5_c_im,
                 s5_d, s5_w_glu, s5_b_glu, ssd_conv_w, ssd_conv_b, ssd_dt_bias, ssd_a_log, ssd_d,
                 ssd_norm_g, hg_lb_logits, hg_norm_g, gla_w_gk2, gla_b_gk, gla_norm_g)
    p.update(wout32=w_out.astype(F32), wup32=w_up.astype(F32), wdn32=w_down.astype(F32),
             gmlp=norm_mlp_g[:, None, :].astype(F32), gfin=norm_final_g[None, :].astype(F32))

    xp = x_prompt.astype(F32)
    xs = x_sample.astype(F32).reshape(ns, D_MODEL)

    s5s = jnp.stack([state_s5_re.reshape(depth, ns, S5_STATE), state_s5_im.reshape(depth, ns, S5_STATE)],
                    axis=1).astype(F32)
    convs = state_ssd_conv.reshape(depth, ns, (SSD_CONV - 1) * SSD_XBC).astype(F32)
    ssds = state_ssd.transpose(0, 2, 3, 4, 1).reshape(depth, SSD_HEADS * SSD_N * SSD_P, ns).astype(F32)
    hgs = state_hgrn.transpose(0, 2, 3, 4, 1).reshape(depth, HG_HEADS * HG_K * HG_V, ns).astype(F32)
    glas = state_gla.transpose(0, 2, 3, 4, 1).reshape(depth, GLA_HEADS * GLA_K * GLA_V, ns).astype(F32)

    outs_p = [[] for _ in range(6)]
    outs_s = [[] for _ in range(5)]
    for l in range(depth):
        final = l == depth - 1
        mix, s5, conv, st_ssd, st_hg, st_gla, *w16, = _mixer_prompt(xp, nb, t_steps, p, l)
        if len(w16) > 3:
            xp = w16.pop()
        xp = _mlp(mix, xp, w16, p, l, final, nb * t_steps, n_seq=nb if final else None)
        for dst, val in zip(outs_p, (s5[0], s5[1], conv, st_ssd, st_hg, st_gla)):
            dst.append(val)

        res = _mixer_sample(xs, p, l, s5s, convs, ssds, hgs, glas)
        xs = _mlp(res[0], xs, w16, p, l, final, 512)
        for dst, val in zip(outs_s, res[1:]):
            dst.append(val)

    y_prompt = xp
    y_sample = xs.reshape(ns, 1, D_MODEL)
    ps5r, ps5i, pconv, pssd, phg, pgla = [jnp.stack(v, axis=0) for v in outs_p]
    ss5, sconv, sssd, shg, sgla = [jnp.stack(v, axis=0) for v in outs_s]
    diag = lambda st, dk: _diag_state(st.reshape((depth * nb,) + st.shape[2:]), dk).reshape(
        depth, nb, N_HEADS, dk, 64)
    return (
        y_prompt, y_sample,
        ps5r.reshape(depth, nb, S5_GROUPS, S5_P), ps5i.reshape(depth, nb, S5_GROUPS, S5_P),
        pconv.reshape(depth, SSD_CONV - 1, nb, SSD_XBC).transpose(0, 2, 1, 3),
        diag(pssd, SSD_N), diag(phg, HG_K), diag(pgla, GLA_K),
        ss5[:, 0].reshape(depth, ns, S5_GROUPS, S5_P), ss5[:, 1].reshape(depth, ns, S5_GROUPS, S5_P),
        sconv.reshape(depth, ns, SSD_CONV - 1, SSD_XBC),
        sssd.reshape(depth, SSD_HEADS, SSD_N, SSD_P, ns).transpose(0, 4, 1, 2, 3),
        shg.reshape(depth, HG_HEADS, HG_K, HG_V, ns).transpose(0, 4, 1, 2, 3),
        sgla.reshape(depth, GLA_HEADS, GLA_K, GLA_V, ns).transpose(0, 4, 1, 2, 3),
    )
```

```python
import functools
import math

import numpy as np
import jax
import jax.numpy as jnp
from jax import lax
from jax.experimental import pallas as pl
from jax.experimental.pallas import tpu as pltpu

F32 = jnp.float32
BF16 = jnp.bfloat16

D_MODEL = 1024
GROUP_W = 256
S5_GROUPS, S5_CH, S5_P = 16, 16, 64
S5_STATE = S5_GROUPS * S5_P
SSD_HEADS, SSD_N, SSD_P, SSD_NGROUPS, SSD_CONV, SSD_XBC = 4, 64, 64, 2, 4, 512
HG_HEADS, HG_K, HG_V = 4, 64, 64
GLA_HEADS, GLA_K, GLA_V, GLA_RANK, GLA_TAU = 4, 32, 64, 16, 16.0
N_HEADS = 4
D_FF = 4096
EPS = 1e-6
LB_FLOOR = 1e-30

O_U, O_Z, O_XBC = 0, 256, 512
O_HQ, O_HF, O_HI, O_HGATE = 1024, 1280, 1536, 1792
O_GQ, O_GK, O_GV, O_GGATE = 2048, 2176, 2304, 2560
O_MISC = 2816
N_PROJ = 2944
LR_LO = SSD_HEADS

SUB = 16
NEG = -1e30
LOG2E = 1.4426950408889634
V7X_VMEM_BYTES = 64 * 1024 * 1024
VMEM_LIMIT = V7X_VMEM_BYTES * 7 // 8


def _sigmoid(x):
    return 0.5 * (1.0 + jnp.tanh(0.5 * x))


def _sigmoid_rel(x):
    return 1.0 / (1.0 + jnp.exp(-x))


def _silu(x):
    return x * _sigmoid(x)


def _softplus(x):
    return jnp.maximum(x, 0.0) + jnp.log1p(jnp.exp(-jnp.abs(x)))


def _log_sigmoid(x):
    return -_softplus(-x)


def _gelu_tanh(x):
    c = math.sqrt(2.0 / math.pi)
    return x * (0.5 * (1.0 + jnp.tanh(c * (x + 0.044715 * (x * x * x)))))


def _rms_scale(x, axis):
    return lax.rsqrt(jnp.mean(x * x, axis=axis, keepdims=True) + EPS)


def _dot(a, b):
    return jnp.dot(a, b, preferred_element_type=F32)


def _dot_nt(a, b):
    return lax.dot_general(a, b, (((1,), (1,)), ((), ())), preferred_element_type=F32)


def _dot_tn(a, b):
    return lax.dot_general(a, b, (((0,), (0,)), ((), ())), preferred_element_type=F32)


def _dot_split(a, b16, terms):
    acc, rest = None, a
    for _ in range(terms):
        piece = rest.astype(BF16)
        part = _dot(piece, b16)
        acc = part if acc is None else acc + part
        rest = rest - piece.astype(F32)
    return acc


def _idiv(x, n):
    shift = int(math.log2(n))
    assert 1 << shift == n
    return x >> shift


def _store_tiles(ref, row0, val):
    rows = val.shape[0]
    for j in range(val.shape[1] // 128):
        ref[j, row0:row0 + rows, :] = val[:, j * 128:(j + 1) * 128]


def _load_tiles(ref, row0, rows, n_tiles):
    return jnp.concatenate([ref[j, row0:row0 + rows, :] for j in range(n_tiles)], axis=1)


def _load_seq(ref, start, t_steps, nb, n_tiles):
    tiles = [ref[j, pl.ds(start, t_steps, stride=nb), :] for j in range(n_tiles)]
    return tiles[0] if n_tiles == 1 else jnp.concatenate(tiles, axis=1)


def _load_rows(ref, start, rows, n_tiles):
    tiles = [ref[j, pl.ds(start, rows), :] for j in range(n_tiles)]
    return tiles[0] if n_tiles == 1 else jnp.concatenate(tiles, axis=1)


def _gated_scan_tile(kd_lanes, dk, nb, t_steps, qf_ref, kp_ref, cp_ref, vp_ref, ob_ref, os_ref, st_ref,
                     s16_ref, j_ref, ko_ref, vo_ref, head_decay=False):
    rows = nb * t_steps
    sub_rows = SUB * nb
    n_sub = rows // sub_rows
    n_kt = kd_lanes // 128
    assert not head_decay or dk == t_steps

    def sub_body(sb):
        r0 = pl.multiple_of(sb * sub_rows, sub_rows)
        accs = [[None] * SUB for _ in range(2)]
        for j in range(SUB):
            reps = SUB - j
            rj = pl.multiple_of(r0 + nb * j, nb)
            q = _load_rows(qf_ref, rj, reps * nb, n_kt)
            c = _load_rows(cp_ref, rj, reps * nb, n_kt)
            k_j = jnp.tile(_load_rows(kp_ref, rj, nb, n_kt), (reps, 1))
            c_j = jnp.tile(_load_rows(cp_ref, rj, nb, n_kt), (reps, 1))
            w = _dot((q * k_j * jnp.exp2(c - c_j)).astype(BF16), j_ref[...])
            for n in range(2):
                v_j = vp_ref[n, pl.ds(rj, nb), :]
                for i in range(j, SUB):
                    term = w[(i - j) * nb:(i - j + 1) * nb, n * 128:(n + 1) * 128] * v_j
                    accs[n][i] = term if accs[n][i] is None else accs[n][i] + term
            yield
        for n in range(2):
            ob_ref[n, pl.ds(r0, sub_rows), :] = jnp.concatenate(accs[n], axis=0)

    t_idx = lax.broadcasted_iota(jnp.int32, (t_steps, kd_lanes), 0)
    ri = lax.broadcasted_iota(jnp.int32, (t_steps, N_HEADS * t_steps), 0)
    ci = lax.broadcasted_iota(jnp.int32, (t_steps, N_HEADS * t_steps), 1) & (t_steps - 1)
    lane_tile_w = min(kd_lanes, 128)
    heads_per_tile = lane_tile_w // dk
    tile_head = _idiv(lax.broadcasted_iota(jnp.int32, (64, lane_tile_w), 1), dk)
    levels = []
    c_sz = SUB
    while c_sz < t_steps:
        levels.append(c_sz)
        c_sz *= 2

    def seq_body(b):
        qb = _load_seq(qf_ref, b, t_steps, nb, n_kt)
        kb = _load_seq(kp_ref, b, t_steps, nb, n_kt)
        cb = _load_seq(cp_ref, b, t_steps, nb, n_kt)
        vb16 = _load_seq(vp_ref, b, t_steps, nb, 2).astype(BF16)
        c_last = cb[t_steps - 1:t_steps, :]

        o = _dot_nt((qb * jnp.exp2(cb)).astype(BF16), s16_ref[b])
        yield
        s = None
        if head_decay:
            cum_rows = [cb[:, t * 128:(t + 1) * 128].T for t in range(n_kt)]
            cum_j = jnp.concatenate(
                [jnp.broadcast_to(cum_rows[(h * dk) // 128][(h * dk) % 128:(h * dk) % 128 + 1, :],
                                  (t_steps, t_steps)) for h in range(N_HEADS)], axis=1)
            dmat = jnp.exp2(jnp.where(ri >= ci, cb - cum_j, NEG))
            s = _dot_nt(qb.astype(BF16), jnp.tile(kb.astype(BF16), (N_HEADS, 1)) * ko_ref[...]) * dmat
            yield
        for c_sz in ([] if head_decay else levels):
            pieces = []
            for m in range(t_steps // (2 * c_sz)):
                a_row = m * 2 * c_sz + c_sz - 1
                pieces.append(jnp.broadcast_to(cb[a_row:a_row + 1, :], (2 * c_sz, kd_lanes)))
            anchor = pieces[0] if len(pieces) == 1 else jnp.concatenate(pieces, axis=0)
            upper = (t_idx & c_sz) != 0
            q_l = (qb * jnp.exp2(jnp.where(upper, cb - anchor, NEG))).astype(BF16)
            k_l = (kb * jnp.exp2(jnp.where(upper, NEG, anchor - cb))).astype(BF16)
            s_l = _dot_nt(q_l, jnp.tile(k_l, (N_HEADS, 1)) * ko_ref[...])
            if 2 * c_sz < t_steps:
                shift = int(math.log2(2 * c_sz))
                s_l = jnp.where((ri >> shift) == (ci >> shift), s_l, 0.0)
            s = s_l if s is None else s + s_l
            yield
        if s is not None:
            o = o + _dot(s.astype(BF16), jnp.tile(vb16, (N_HEADS, 1)) * vo_ref[...])
        for j in range(2):
            os_ref[j, pl.ds(b, t_steps, stride=nb), :] = o[:, j * 128:(j + 1) * 128]
        yield

        upd = _dot_tn(vb16, (kb * jnp.exp2(c_last - cb)).astype(BF16))
        decay = jnp.exp2(c_last)
        for h in range(N_HEADS):
            r0, l0 = h * 64, (h // heads_per_tile) * lane_tile_w
            own = tile_head == (h % heads_per_tile)
            blk = (decay[:, l0:l0 + lane_tile_w] * st_ref[b, r0:r0 + 64, l0:l0 + lane_tile_w]
                   + jnp.where(own, upd[r0:r0 + 64, l0:l0 + lane_tile_w], 0.0))
            st_ref[b, r0:r0 + 64, l0:l0 + lane_tile_w] = blk
            s16_ref[b, r0:r0 + 64, l0:l0 + lane_tile_w] = blk.astype(BF16)

    assert nb % n_sub == 0
    seq_per_trip = nb // n_sub

    def trip(i, carry):
        sub = None if head_decay else sub_body(i)
        seqs = [seq_body(i * seq_per_trip + s) for s in range(seq_per_trip)]
        turn = 0
        while sub is not None or seqs:
            if seqs:
                turn %= len(seqs)
                if next(seqs[turn], "done") == "done":
                    seqs.pop(turn)
                else:
                    turn += 1
            if sub is not None and next(sub, "done") == "done":
                sub = None
        return carry

    lax.fori_loop(0, n_sub, trip, 0, unroll=4 if head_decay else 2)


def _cumsum_time(nb, t_steps, src_ref, cp_ref, kd_lanes):
    def body(t, c):
        r = pl.multiple_of(t * nb, nb)
        c = c + src_ref[pl.ds(r, nb), :kd_lanes] * LOG2E
        for j in range(kd_lanes // 128):
            cp_ref[j, pl.ds(r, nb), :] = c[:, j * 128:(j + 1) * 128]
        return c

    lax.fori_loop(0, t_steps, body, jnp.zeros((nb, kd_lanes), F32), unroll=4)


def _head_rms_scale(o, jn_ref):
    return lax.rsqrt(_dot_split(o * o, jn_ref[...], 2) + EPS)


def _mixer_prompt_kernel(nb, t_steps, batch_major_in,
                         x_ref, gmix_ref, win_ref, bblk_ref, lam_ref, cblk_ref, wglu_ref,
                         v256_ref, v128_ref, v512_ref, w2p_ref, ehd_ref, j256_ref, j128_ref, jn_ref,
                         ko256_ref, ko128_ref, wout_ref, wup_ref, wdn_ref,
                         mix_ref, s5_ref, conv_ref, st_ssd_ref, st_hg_ref, st_gla_ref,
                         wout16_ref, wup16_ref, wdn16_ref, *rest):
    wout16_ref[...] = wout_ref[...].astype(BF16)
    wup16_ref[...] = wup_ref[...].astype(BF16)
    wdn16_ref[...] = wdn_ref[...].astype(BF16)
    if batch_major_in:
        xtm_ref, xt_ref = rest[0], rest[-1]
        rest = rest[1:-1]
    (proj_ref, bu_ref, xe_ref, qf_ref, kp_ref, cp_ref, vp_ref, ob_ref, os_ref, la_ref,
     s16_ssd_ref, s16_hg_ref, s16_gla_ref, hn_ref) = rest
    rows = nb * t_steps
    conv_rows = (SSD_CONV - 1) * nb

    @pl.when(pl.program_id(0) == 0)
    def _():
        for ref in (s5_ref, st_ssd_ref, st_hg_ref, st_gla_ref, s16_ssd_ref, s16_hg_ref, s16_gla_ref):
            ref[...] = jnp.zeros_like(ref)
        xe_ref[0:conv_rows, :] = jnp.zeros((conv_rows, SSD_XBC), F32)

    if batch_major_in:
        for b in range(nb):
            for j in range(D_MODEL // 128):
                xt_ref[j, pl.ds(b, t_steps, stride=nb), :] = x_ref[b, :, j * 128:(j + 1) * 128]
        x = _load_tiles(xt_ref, 0, rows, D_MODEL // 128)
        xtm_ref[...] = x
    else:
        x = x_ref[...]
    hn_ref[...] = (x * _rms_scale(x, -1) * gmix_ref[...]).astype(BF16)

    def project(lo, hi):
        proj_ref[:, lo:hi] = _dot(hn_ref[...], win_ref[:, lo:hi])

    project(O_U, O_HQ)
    project(O_MISC, N_PROJ)

    u = proj_ref[:, O_U:O_U + GROUP_W]
    bu_ref[...] = _dot(u.astype(BF16), bblk_ref[...])
    lam_r = jnp.broadcast_to(lam_ref[0:1, :], (nb, S5_STATE))
    lam_i = jnp.broadcast_to(lam_ref[1:2, :], (nb, S5_STATE))

    def s5_step(t, h):
        h_r, h_i = h
        r = pl.multiple_of(t * nb, nb)
        n_r = lam_r * h_r - lam_i * h_i + bu_ref[pl.ds(r, nb), 0:S5_STATE]
        n_i = lam_r * h_i + lam_i * h_r + bu_ref[pl.ds(r, nb), S5_STATE:2 * S5_STATE]
        bu_ref[pl.ds(r, nb), 0:S5_STATE] = n_r
        bu_ref[pl.ds(r, nb), S5_STATE:2 * S5_STATE] = n_i
        return n_r, n_i

    h_r, h_i = lax.fori_loop(0, t_steps, s5_step, (s5_ref[0], s5_ref[1]), unroll=2)
    s5_ref[0] = h_r
    s5_ref[1] = h_i
    y = _dot(bu_ref[...].astype(BF16), cblk_ref[...]) + v256_ref[0:1, :] * u
    z = _gelu_tanh(y)
    o_s5 = z * _sigmoid(_dot(z.astype(BF16), wglu_ref[...]) + v256_ref[1:2, :])
    mix_ref[:, 0:GROUP_W] = o_s5.astype(mix_ref.dtype)

    xe_ref[conv_rows:conv_rows + rows, :] = proj_ref[:, O_XBC:O_XBC + SSD_XBC]
    conv = v512_ref[SSD_CONV:SSD_CONV + 1, :]
    for j in range(SSD_CONV):
        conv = conv + xe_ref[j * nb:j * nb + rows, :] * v512_ref[j:j + 1, :]
    tail = xe_ref[rows:rows + conv_rows, :]
    xe_ref[0:conv_rows, :] = tail
    conv_ref[...] = tail
    act = _silu(conv)
    xs = act[:, 0:GROUP_W]
    b_g = act[:, GROUP_W:GROUP_W + 128]
    c_g = act[:, GROUP_W + 128:GROUP_W + 256]
    misc = proj_ref[:, O_MISC:O_MISC + 128]
    dt = _softplus(misc + v128_ref[0:1, :])
    dt_hd = _dot_split(dt, ehd_ref[...], 3)
    la_ref[...] = dt_hd * v256_ref[7:8, :]
    _store_tiles(qf_ref, 0, jnp.concatenate(
        [c_g[:, 0:64], c_g[:, 0:64], c_g[:, 64:128], c_g[:, 64:128]], axis=1))
    _store_tiles(kp_ref, 0, jnp.concatenate(
        [b_g[:, 0:64], b_g[:, 0:64], b_g[:, 64:128], b_g[:, 64:128]], axis=1))
    _store_tiles(vp_ref, 0, xs * dt_hd)
    _cumsum_time(nb, t_steps, la_ref, cp_ref, 256)
    _gated_scan_tile(256, 64, nb, t_steps, qf_ref, kp_ref, cp_ref, vp_ref, ob_ref, os_ref, st_ssd_ref,
                     s16_ssd_ref, j256_ref, ko256_ref, ko256_ref, head_decay=True)
    project(O_HQ, O_MISC)
    y = _load_tiles(os_ref, 0, rows, 2) + v256_ref[2:3, :] * xs
    y = y * _silu(proj_ref[:, O_Z:O_Z + GROUP_W])
    y = y * _rms_scale(y, -1) * v256_ref[3:4, :]
    mix_ref[:, GROUP_W:2 * GROUP_W] = y.astype(mix_ref.dtype)

    sig_f = _sigmoid_rel(proj_ref[:, O_HF:O_HF + GROUP_W])
    nsig_f = 1.0 - sig_f
    la_ref[...] = jnp.log(sig_f + v256_ref[4:5, :] * nsig_f)
    _store_tiles(qf_ref, 0, _silu(proj_ref[:, O_HQ:O_HQ + GROUP_W]))
    _store_tiles(kp_ref, 0, v256_ref[5:6, :] * nsig_f)
    _store_tiles(vp_ref, 0, proj_ref[:, O_HI:O_HI + GROUP_W])
    _cumsum_time(nb, t_steps, la_ref, cp_ref, 256)
    _gated_scan_tile(256, 64, nb, t_steps, qf_ref, kp_ref, cp_ref, vp_ref, ob_ref, os_ref, st_hg_ref,
                     s16_hg_ref, j256_ref, ko256_ref, ko256_ref)
    o = _load_tiles(ob_ref, 0, rows, 2) + _load_tiles(os_ref, 0, rows, 2)
    o = o * _head_rms_scale(o, jn_ref) * v256_ref[6:7, :]
    o = o * _silu(proj_ref[:, O_HGATE:O_HGATE + GROUP_W])
    mix_ref[:, 2 * GROUP_W:3 * GROUP_W] = o.astype(mix_ref.dtype)

    gk = _dot(misc.astype(BF16), w2p_ref[...]) + v128_ref[2:3, :]
    la_ref[:, 0:128] = _log_sigmoid(gk) * (1.0 / GLA_TAU)
    _store_tiles(qf_ref, 0, proj_ref[:, O_GQ:O_GQ + 128] * (GLA_K ** -0.5))
    _store_tiles(kp_ref, 0, proj_ref[:, O_GK:O_GK + 128])
    _store_tiles(vp_ref, 0, proj_ref[:, O_GV:O_GV + GROUP_W])
    _cumsum_time(nb, t_steps, la_ref, cp_ref, 128)
    _gated_scan_tile(128, 32, nb, t_steps, qf_ref, kp_ref, cp_ref, vp_ref, ob_ref, os_ref, st_gla_ref,
                     s16_gla_ref, j128_ref, ko128_ref, ko256_ref)
    o = _load_tiles(ob_ref, 0, rows, 2) + _load_tiles(os_ref, 0, rows, 2)
    o = o * _head_rms_scale(o, jn_ref) * v256_ref[8:9, :]
    o = o * _silu(proj_ref[:, O_GGATE:O_GGATE + GROUP_W])
    mix_ref[:, 3 * GROUP_W:4 * GROUP_W] = o.astype(mix_ref.dtype)


def _const_spec(shape):
    nd = len(shape)
    return pl.BlockSpec(shape, lambda i, _nd=nd: (0,) * _nd, pipeline_mode=pl.Buffered(1))


def _layer_spec(arr, layer):
    nd = arr.ndim - 1
    return pl.BlockSpec((None,) + arr.shape[1:], lambda i, _nd=nd: (layer,) + (0,) * _nd,
                        pipeline_mode=pl.Buffered(1))


PROMPT_LAYER_PARAMS = ('gmix', 'win', 'bblk', 'lam', 'cblk', 'wglu', 'v256', 'v128', 'v512', 'w2p')
PROMPT_SHARED_PARAMS = ('ehd', 'j256', 'j128', 'jn')


def _mixer_prompt(x, nb, t_steps, p, layer):
    batch_major_in = x.ndim == 3
    rows_total = x.shape[0] * x.shape[1] if batch_major_in else x.shape[0]
    rows = nb * t_steps
    n_tiles = rows_total // rows
    layered = [p[k] for k in PROMPT_LAYER_PARAMS]
    shared = [p[k] for k in PROMPT_SHARED_PARAMS] + [
        jnp.asarray(_block_ones(N_HEADS * t_steps, t_steps, 256, 64), BF16),
        jnp.asarray(_block_ones(N_HEADS * t_steps, t_steps, 128, 32), BF16)]
    mlp_w = [p['wout32'], p['wup32'], p['wdn32']]
    assert all(w.shape[1] % (16 * n_tiles) == 0 for w in mlp_w)
    slab = lambda w: (w.shape[1] // n_tiles, w.shape[2])
    consts = layered + shared + mlp_w
    x_spec = (pl.BlockSpec((nb, t_steps, D_MODEL), lambda i: (0, i, 0)) if batch_major_in
              else pl.BlockSpec((rows, D_MODEL), lambda i: (i, 0)))
    in_specs = ([x_spec] + [_layer_spec(c, layer) for c in layered] + [_const_spec(c.shape) for c in shared]
                + [pl.BlockSpec((None,) + slab(w), lambda i: (layer, i, 0)) for w in mlp_w])
    conv_rows = (SSD_CONV - 1) * nb
    out_shape = (
        jax.ShapeDtypeStruct((rows_total, D_MODEL), BF16),
        jax.ShapeDtypeStruct((2, nb, S5_STATE), F32),
        jax.ShapeDtypeStruct((conv_rows, SSD_XBC), F32),
        jax.ShapeDtypeStruct((nb, 256, 256), F32),
        jax.ShapeDtypeStruct((nb, 256, 256), F32),
        jax.ShapeDtypeStruct((nb, 256, 128), F32),
    ) + tuple(jax.ShapeDtypeStruct(w.shape[1:], BF16) for w in mlp_w)
    out_specs = (
        pl.BlockSpec((rows, D_MODEL), lambda i: (i, 0)),
        pl.BlockSpec((2, nb, S5_STATE), lambda i: (0, 0, 0)),
        pl.BlockSpec((conv_rows, SSD_XBC), lambda i: (0, 0)),
        pl.BlockSpec((nb, 256, 256), lambda i: (0, 0, 0)),
        pl.BlockSpec((nb, 256, 256), lambda i: (0, 0, 0)),
        pl.BlockSpec((nb, 256, 128), lambda i: (0, 0, 0)),
    ) + tuple(pl.BlockSpec(slab(w), lambda i: (i, 0)) for w in mlp_w)
    scratch = [
        pltpu.VMEM((rows, N_PROJ), F32),
        pltpu.VMEM((rows, 2 * S5_STATE), F32),
        pltpu.VMEM((conv_rows + rows, SSD_XBC), F32),
        pltpu.VMEM((2, rows, 128), F32),
        pltpu.VMEM((2, rows, 128), F32),
        pltpu.VMEM((2, rows, 128), F32),
        pltpu.VMEM((2, rows, 128), F32),
        pltpu.VMEM((2, rows, 128), F32),
        pltpu.VMEM((2, rows, 128), F32),
        pltpu.VMEM((rows, 256), F32),
        pltpu.VMEM((nb, 256, 256), BF16),
        pltpu.VMEM((nb, 256, 256), BF16),
        pltpu.VMEM((nb, 256, 128), BF16),
        pltpu.VMEM((rows, D_MODEL), BF16),
    ]
    if batch_major_in:
        out_shape += (jax.ShapeDtypeStruct((rows_total, D_MODEL), F32),)
        out_specs += (pl.BlockSpec((rows, D_MODEL), lambda i: (i, 0)),)
        scratch.append(pltpu.VMEM((D_MODEL // 128, rows, 128), F32))
    return pl.pallas_call(
        functools.partial(_mixer_prompt_kernel, nb, t_steps, batch_major_in),
        grid=(n_tiles,),
        in_specs=in_specs,
        out_specs=out_specs,
        out_shape=out_shape,
        scratch_shapes=scratch,
        compiler_params=pltpu.CompilerParams(dimension_semantics=("arbitrary",),
                                             vmem_limit_bytes=VMEM_LIMIT),
        name="mixer_prompt",
    )(x, *consts)


SAMPLE_LAYER_PARAMS = ('gmix', 'win', 'bblk', 'lam', 'cblk', 'wglu', 'v256', 'v128', 'v512', 'w2p')


def _mixer_sample_kernel(x_ref, gmix_ref, win_ref, bblk_ref, lam_ref, cblk_ref, wglu_ref,
                         v256_ref, v128_ref, v512_ref, w2p_ref, jn_ref,
                         s5_in, conv_in, ssd_in, hg_in, gla_in,
                         mix_ref, s5_out, conv_out, ssd_out, hg_out, gla_out,
                         proj_ref, xs_ref, act_ref, dt_ref, es_ref, qh_ref, kh_ref, eh_ref, vh_ref,
                         qg_ref, kg_ref, eg_ref, vg_ref, os_ref, oh_ref, og_ref):
    h = pl.program_id(0)

    @pl.when(h == 0)
    def _():
        x = x_ref[...]
        hn = (x * _rms_scale(x, -1) * gmix_ref[...]).astype(BF16)
        proj_ref[...] = _dot(hn, win_ref[...])

        u = proj_ref[:, O_U:O_U + GROUP_W]
        bu = _dot(u.astype(BF16), bblk_ref[...])
        lam_r, lam_i = lam_ref[0:1, :], lam_ref[1:2, :]
        h0_r, h0_i = s5_in[0], s5_in[1]
        h_r = lam_r * h0_r - lam_i * h0_i + bu[:, 0:S5_STATE]
        h_i = lam_r * h0_i + lam_i * h0_r + bu[:, S5_STATE:2 * S5_STATE]
        s5_out[0] = h_r
        s5_out[1] = h_i
        y = (_dot(jnp.concatenate([h_r, h_i], axis=1).astype(BF16), cblk_ref[...])
             + v256_ref[0:1, :] * u)
        z = _gelu_tanh(y)
        o_s5 = z * _sigmoid(_dot(z.astype(BF16), wglu_ref[...]) + v256_ref[1:2, :])
        mix_ref[:, 0:GROUP_W] = o_s5.astype(mix_ref.dtype)

        xbc = proj_ref[:, O_XBC:O_XBC + SSD_XBC]
        conv = v512_ref[SSD_CONV:SSD_CONV + 1, :] + v512_ref[SSD_CONV - 1:SSD_CONV, :] * xbc
        for j in range(SSD_CONV - 1):
            conv = conv + v512_ref[j:j + 1, :] * conv_in[:, j * SSD_XBC:(j + 1) * SSD_XBC]
        conv_out[:, 0:2 * SSD_XBC] = conv_in[:, SSD_XBC:3 * SSD_XBC]
        conv_out[:, 2 * SSD_XBC:3 * SSD_XBC] = xbc
        act = _silu(conv)
        xs_ref[...] = act[:, 0:GROUP_W]
        act_ref[...] = act.T
        misc = proj_ref[:, O_MISC:O_MISC + 128]
        dt = _softplus(misc + v128_ref[0:1, :])
        dt_ref[...] = dt.T
        es_ref[...] = jnp.exp(dt * v128_ref[1:2, :]).T

        sig_f = _sigmoid_rel(proj_ref[:, O_HF:O_HF + GROUP_W])
        nsig_f = 1.0 - sig_f
        eh_ref[...] = (sig_f + v256_ref[4:5, :] * nsig_f).T
        kh_ref[...] = (v256_ref[5:6, :] * nsig_f).T
        qh_ref[...] = _silu(proj_ref[:, O_HQ:O_HQ + GROUP_W]).T
        vh_ref[...] = proj_ref[:, O_HI:O_HI + GROUP_W].T

        gk = _dot(misc.astype(BF16), w2p_ref[...]) + v128_ref[2:3, :]
        eg_ref[...] = jnp.exp(_log_sigmoid(gk) * (1.0 / GLA_TAU)).T
        kg_ref[...] = proj_ref[:, O_GK:O_GK + 128].T
        qg_ref[...] = (proj_ref[:, O_GQ:O_GQ + 128] * (GLA_K ** -0.5)).T
        vg_ref[...] = proj_ref[:, O_GV:O_GV + GROUP_W].T

    def state_step(st_in, st_out, e_row, k_row, q_row, v, n_keys):
        def body(i, acc):
            r = pl.multiple_of(i * 64, 64)
            s_new = e_row(i) * st_in[pl.ds(r, 64), :] + k_row(i) * v
            st_out[pl.ds(r, 64), :] = s_new
            return acc + q_row(i) * s_new
        return lax.fori_loop(0, n_keys, body, jnp.zeros_like(v), unroll=4)

    def head_rows():
        return pl.ds(pl.multiple_of(h * 64, 64), 64)

    g = _idiv(h, SSD_HEADS // SSD_NGROUPS)
    e_h = es_ref[pl.ds(h, 1), :]
    os_ref[head_rows(), :] = state_step(
        ssd_in, ssd_out,
        lambda i: e_h,
        lambda i: act_ref[pl.ds(GROUP_W + g * SSD_N + i, 1), :],
        lambda i: act_ref[pl.ds(GROUP_W + 128 + g * SSD_N + i, 1), :],
        act_ref[head_rows(), :] * dt_ref[pl.ds(h, 1), :], SSD_N)

    oh_ref[head_rows(), :] = state_step(
        hg_in, hg_out,
        lambda i: eh_ref[pl.ds(h * HG_K + i, 1), :],
        lambda i: kh_ref[pl.ds(h * HG_K + i, 1), :],
        lambda i: qh_ref[pl.ds(h * HG_K + i, 1), :],
        vh_ref[head_rows(), :], HG_K)

    og_ref[head_rows(), :] = state_step(
        gla_in, gla_out,
        lambda i: eg_ref[pl.ds(h * GLA_K + i, 1), :],
        lambda i: kg_ref[pl.ds(h * GLA_K + i, 1), :],
        lambda i: qg_ref[pl.ds(h * GLA_K + i, 1), :],
        vg_ref[head_rows(), :], GLA_K)

    @pl.when(h == N_HEADS - 1)
    def _():
        y = os_ref[...].T + v256_ref[2:3, :] * xs_ref[...]
        y = y * _silu(proj_ref[:, O_Z:O_Z + GROUP_W])
        mix_ref[:, GROUP_W:2 * GROUP_W] = (y * _rms_scale(y, -1) * v256_ref[3:4, :]).astype(mix_ref.dtype)
        o = oh_ref[...].T
        o = o * _head_rms_scale(o, jn_ref) * v256_ref[6:7, :] * _silu(proj_ref[:, O_HGATE:O_HGATE + GROUP_W])
        mix_ref[:, 2 * GROUP_W:3 * GROUP_W] = o.astype(mix_ref.dtype)
        o = og_ref[...].T
        o = o * _head_rms_scale(o, jn_ref) * v256_ref[8:9, :] * _silu(proj_ref[:, O_GGATE:O_GGATE + GROUP_W])
        mix_ref[:, 3 * GROUP_W:4 * GROUP_W] = o.astype(mix_ref.dtype)


def _mixer_sample(x, p, layer, s5, conv, ssd_t, hg_t, gla_t):
    nbatch = x.shape[0]
    layered = [p[k] for k in SAMPLE_LAYER_PARAMS]
    head_in = lambda a: pl.BlockSpec((None, a.shape[1] // N_HEADS, nbatch), lambda i: (layer, i, 0))
    head_blk = lambda a: pl.BlockSpec((a.shape[1] // N_HEADS, nbatch), lambda i: (i, 0))
    res_blk = lambda shape: pl.BlockSpec(shape, lambda i, _nd=len(shape): (0,) * _nd)
    out_shape = (
        jax.ShapeDtypeStruct((nbatch, D_MODEL), BF16),
        jax.ShapeDtypeStruct(s5.shape[1:], F32),
        jax.ShapeDtypeStruct(conv.shape[1:], F32),
        jax.ShapeDtypeStruct(ssd_t.shape[1:], F32),
        jax.ShapeDtypeStruct(hg_t.shape[1:], F32),
        jax.ShapeDtypeStruct(gla_t.shape[1:], F32),
    )
    fm = lambda n: pltpu.VMEM((n, nbatch), F32)
    scratch = [
        pltpu.VMEM((nbatch, N_PROJ), F32),
        pltpu.VMEM((nbatch, GROUP_W), F32),
        fm(SSD_XBC),
        fm(128), fm(128),
        fm(GROUP_W), fm(GROUP_W), fm(GROUP_W), fm(GROUP_W),
        fm(128), fm(128), fm(128), fm(GROUP_W),
        fm(GROUP_W), fm(GROUP_W), fm(GROUP_W),
    ]
    return pl.pallas_call(
        _mixer_sample_kernel,
        grid=(N_HEADS,),
        in_specs=([_const_spec(x.shape)] + [_layer_spec(c, layer) for c in layered]
                  + [_const_spec(p['jn'].shape), _layer_spec(s5, layer), _layer_spec(conv, layer),
                     head_in(ssd_t), head_in(hg_t), head_in(gla_t)]),
        out_specs=(res_blk((nbatch, D_MODEL)), res_blk(s5.shape[1:]), res_blk(conv.shape[1:]),
                   head_blk(ssd_t), head_blk(hg_t), head_blk(gla_t)),
        out_shape=out_shape,
        scratch_shapes=scratch,
        compiler_params=pltpu.CompilerParams(dimension_semantics=("arbitrary",),
                                             vmem_limit_bytes=VMEM_LIMIT),
        name="mixer_sample",
    )(x, *layered, p['jn'], s5, conv, ssd_t, hg_t, gla_t)


FF_TILE = 512


def _mlp_kernel(final, n_seq, mix_ref, x_ref, wout_ref, g_ref, wup_ref, wdn_ref, gfin_ref, o_ref, *scratch):
    x1 = x_ref[...] + _dot(mix_ref[...], wout_ref[...])
    hn = (x1 * _rms_scale(x1, -1) * g_ref[...]).astype(BF16)
    acc = x1
    for f in range(D_FF // FF_TILE):
        up = jnp.maximum(_dot(hn, wup_ref[:, f * FF_TILE:(f + 1) * FF_TILE]), 0.0)
        acc = acc + _dot((up * up).astype(BF16), wdn_ref[f * FF_TILE:(f + 1) * FF_TILE, :])
    if final:
        acc = acc * _rms_scale(acc, -1) * gfin_ref[...]
    if n_seq is None:
        o_ref[...] = acc
    else:
        tm_ref, = scratch
        _store_tiles(tm_ref, 0, acc)
        steps = acc.shape[0] // n_seq
        for b in range(n_seq):
            for j in range(D_MODEL // 128):
                o_ref[b, :, j * 128:(j + 1) * 128] = tm_ref[j, pl.ds(b, steps, stride=n_seq), :]


def _mlp(mix, x, weights16, p, layer, final, tm, n_seq=None):
    rows_total = x.shape[0]
    tm = min(tm, rows_total)
    wout, wup, wdn = weights16
    consts = [wout, p['gmlp'], wup, wdn, p['gfin']]
    specs = [_const_spec(wout.shape), _layer_spec(p['gmlp'], layer), _const_spec(wup.shape),
             _const_spec(wdn.shape), _const_spec(p['gfin'].shape)]
    if n_seq is None:
        out_spec = pl.BlockSpec((tm, D_MODEL), lambda i: (i, 0))
        out_shape = jax.ShapeDtypeStruct((rows_total, D_MODEL), F32)
        scratch = []
    else:
        out_spec = pl.BlockSpec((n_seq, tm // n_seq, D_MODEL), lambda i: (0, i, 0))
        out_shape = jax.ShapeDtypeStruct((n_seq, rows_total // n_seq, D_MODEL), F32)
        scratch = [pltpu.VMEM((D_MODEL // 128, tm, 128), F32)]
    return pl.pallas_call(
        functools.partial(_mlp_kernel, final, n_seq),
        grid=(rows_total // tm,),
        in_specs=[pl.BlockSpec((tm, D_MODEL), lambda i: (i, 0)),
                  pl.BlockSpec((tm, D_MODEL), lambda i: (i, 0))] + specs,
        out_specs=out_spec,
        out_shape=out_shape,
        scratch_shapes=scratch,
        compiler_params=pltpu.CompilerParams(dimension_semantics=("parallel",),
                                             vmem_limit_bytes=VMEM_LIMIT),
        name="out_mlp",
    )(mix, x, *consts)


N_IN = 2836
IN_MISC_LO, IN_LR_LO = 1024, 2820
PACK_ROWS = 256


def _pack_win_kernel(w_ref, o_ref):
    w = w_ref[...]
    n_dt, n_lr = SSD_HEADS, GLA_RANK
    o_ref[:, 0:IN_MISC_LO] = w[:, 0:IN_MISC_LO].astype(BF16)
    o_ref[:, IN_MISC_LO:O_MISC] = w[:, IN_MISC_LO + n_dt:IN_LR_LO].astype(BF16)
    tail = jnp.concatenate([w[:, IN_MISC_LO:IN_MISC_LO + n_dt], w[:, IN_LR_LO:IN_LR_LO + n_lr],
                            jnp.zeros((w.shape[0], 128 - n_dt - n_lr), w.dtype)], axis=1)
    o_ref[:, O_MISC:N_PROJ] = tail.astype(BF16)


def _pack_win(w_in):
    depth, d_in, n_in = w_in.shape
    assert n_in == N_IN and d_in % PACK_ROWS == 0
    return pl.pallas_call(
        _pack_win_kernel,
        grid=(depth, d_in // PACK_ROWS),
        in_specs=[pl.BlockSpec((None, PACK_ROWS, n_in), lambda l, i: (l, i, 0))],
        out_specs=pl.BlockSpec((None, PACK_ROWS, N_PROJ), lambda l, i: (l, i, 0)),
        out_shape=jax.ShapeDtypeStruct((depth, d_in, N_PROJ), BF16),
        compiler_params=pltpu.CompilerParams(dimension_semantics=("parallel", "parallel")),
        name="pack_win",
    )(w_in.astype(F32))


def _block_ones(n_rows, row_blk, n_cols, col_blk, scale=1.0):
    r = np.arange(n_rows)[:, None] // row_blk
    c = np.arange(n_cols)[None, :] // col_blk
    return (r == c).astype(np.float32) * scale


def _prepare(norm_mix_g, w_in, s5_lam_re, s5_lam_im, s5_log_dt, s5_b_re, s5_b_im, s5_c_re, s5_c_im,
             s5_d, s5_w_glu, s5_b_glu, ssd_conv_w, ssd_conv_b, ssd_dt_bias, ssd_a_log, ssd_d, ssd_norm_g,
             hg_lb_logits, hg_norm_g, gla_w_gk2, gla_b_gk, gla_norm_g):
    depth = w_in.shape[0]
    win = _pack_win(w_in)

    lam_re, lam_im = s5_lam_re.astype(F32), s5_lam_im.astype(F32)
    dt = jnp.exp(s5_log_dt.astype(F32))[..., None]
    ea = jnp.exp(lam_re * dt)
    lb_r, lb_i = ea * jnp.cos(lam_im * dt), ea * jnp.sin(lam_im * dt)
    den = lam_re * lam_re + lam_im * lam_im
    cr = ((lb_r - 1.0) * lam_re + lb_i * lam_im) / den
    ci = (lb_i * lam_re - (lb_r - 1.0) * lam_im) / den
    bb_r = cr[..., None] * s5_b_re - ci[..., None] * s5_b_im
    bb_i = cr[..., None] * s5_b_im + ci[..., None] * s5_b_re
    eye_g = jnp.eye(S5_GROUPS, dtype=F32)
    blk_b = lambda t: jnp.einsum('lgpc,gh->lgchp', t, eye_g).reshape(depth, GROUP_W, S5_STATE)
    blk_c = lambda t: jnp.einsum('lgcp,gh->lgphc', t, eye_g).reshape(depth, S5_STATE, GROUP_W)
    bblk = jnp.concatenate([blk_b(bb_r), blk_b(bb_i)], axis=-1)
    cblk = jnp.concatenate([blk_c(s5_c_re.astype(F32)), -blk_c(s5_c_im.astype(F32))], axis=1)
    lam = jnp.stack([lb_r.reshape(depth, S5_STATE), lb_i.reshape(depth, S5_STATE)], axis=1)

    sm = jax.nn.softmax(hg_lb_logits.astype(F32), axis=0)
    lb = jnp.cumsum(sm, axis=0) - sm[0:1]
    lb_floor = jnp.maximum(lb, LB_FLOOR)
    a = -jnp.exp(ssd_a_log.astype(F32))
    rep64 = lambda t: jnp.repeat(t, 64, axis=-1)
    zeros256 = jnp.zeros((depth, GROUP_W), F32)
    v256 = jnp.stack([s5_d, s5_b_glu, rep64(ssd_d), ssd_norm_g, lb_floor, 1.0 - lb, hg_norm_g, rep64(a),
                      jnp.tile(gla_norm_g, (1, GLA_HEADS))] + [zeros256] * 7, axis=1).astype(F32)
    pad128 = lambda t: jnp.pad(t, ((0, 0), (0, 128 - t.shape[-1])))
    zeros128 = jnp.zeros((depth, 128), F32)
    v128 = jnp.stack([pad128(ssd_dt_bias), pad128(a), gla_b_gk] + [zeros128] * 5, axis=1).astype(F32)
    v512 = jnp.concatenate([ssd_conv_w, ssd_conv_b[:, None, :], jnp.zeros((depth, 3, SSD_XBC), F32)],
                           axis=1).astype(F32)
    w2p = jnp.pad(gla_w_gk2, ((0, 0), (LR_LO, 128 - LR_LO - GLA_RANK), (0, 0))).astype(BF16)

    return dict(
        gmix=norm_mix_g[:, None, :].astype(F32), win=win, bblk=bblk.astype(BF16), lam=lam,
        cblk=cblk.astype(BF16), wglu=s5_w_glu.astype(BF16), v256=v256, v128=v128, v512=v512, w2p=w2p,
        ehd=jnp.asarray(_block_ones(128, 1, 256, 64) * (np.arange(128)[:, None] < SSD_HEADS), BF16),
        j256=jnp.asarray(_block_ones(256, 64, 256, 64), BF16),
        j128=jnp.asarray(_block_ones(128, 32, 256, 64), BF16),
        jn=jnp.asarray(_block_ones(256, 64, 256, 64, 1.0 / 64), BF16),
    )


def _diag_state(st, dk):
    nb = st.shape[0]
    s = st.reshape(nb, N_HEADS, 64, N_HEADS, dk)
    idx = jnp.arange(N_HEADS)
    s = s[:, idx, :, idx, :]
    return s.transpose(1, 0, 3, 2)


def kernel(x_prompt, x_sample, state_s5_re, state_s5_im, state_ssd_conv, state_ssd, state_hgrn, state_gla,
           norm_mix_g, w_in, s5_lam_re, s5_lam_im, s5_log_dt, s5_b_re, s5_b_im, s5_c_re, s5_c_im,
           s5_d, s5_w_glu, s5_b_glu, ssd_conv_w, ssd_conv_b, ssd_dt_bias, ssd_a_log, ssd_d, ssd_norm_g,
           hg_lb_logits, hg_norm_g, gla_w_gk2, gla_b_gk, gla_norm_g, w_out, norm_mlp_g, w_up, w_down,
           norm_final_g):
    nb, seq, _ = x_prompt.shape
    ns = x_sample.shape[0]
    depth = w_in.shape[0]
    t_steps = min(64, seq)

    p = _prepare(norm_mix_g, w_in, s5_lam_re, s5_lam_im, s5_log_dt, s5_b_re, s5_b_im, s5_c_re, s5_c_im,
                 s5_d, s5_w_glu, s5_b_glu, ssd_conv_w, ssd_conv_b, ssd_dt_bias, ssd_a_log, ssd_d,
                 ssd_norm_g, hg_lb_logits, hg_norm_g, gla_w_gk2, gla_b_gk, gla_norm_g)
    p.update(wout32=w_out.astype(F32), wup32=w_up.astype(F32), wdn32=w_down.astype(F32),
             gmlp=norm_mlp_g[:, None, :].astype(F32), gfin=norm_final_g[None, :].astype(F32))

    xp = x_prompt.astype(F32)
    xs = x_sample.astype(F32).reshape(ns, D_MODEL)

    s5s = jnp.stack([state_s5_re.reshape(depth, ns, S5_STATE), state_s5_im.reshape(depth, ns, S5_STATE)],
                    axis=1).astype(F32)
    convs = state_ssd_conv.reshape(depth, ns, (SSD_CONV - 1) * SSD_XBC).astype(F32)
    ssds = state_ssd.transpose(0, 2, 3, 4, 1).reshape(depth, SSD_HEADS * SSD_N * SSD_P, ns).astype(F32)
    hgs = state_hgrn.transpose(0, 2, 3, 4, 1).reshape(depth, HG_HEADS * HG_K * HG_V, ns).astype(F32)
    glas = state_gla.transpose(0, 2, 3, 4, 1).reshape(depth, GLA_HEADS * GLA_K * GLA_V, ns).astype(F32)

    outs_p = [[] for _ in range(6)]
    outs_s = [[] for _ in range(5)]
    for l in range(depth):
        final = l == depth - 1
        mix, s5, conv, st_ssd, st_hg, st_gla, *w16, = _mixer_prompt(xp, nb, t_steps, p, l)
        if len(w16) > 3:
            xp = w16.pop()
        xp = _mlp(mix, xp, w16, p, l, final, 2 * nb * t_steps, n_seq=nb if final else None)
        for dst, val in zip(outs_p, (s5[0], s5[1], conv, st_ssd, st_hg, st_gla)):
            dst.append(val)

        res = _mixer_sample(xs, p, l, s5s, convs, ssds, hgs, glas)
        xs = _mlp(res[0], xs, w16, p, l, final, 512)
        for dst, val in zip(outs_s, res[1:]):
            dst.append(val)

    y_prompt = xp
    y_sample = xs.reshape(ns, 1, D_MODEL)
    ps5r, ps5i, pconv, pssd, phg, pgla = [jnp.stack(v, axis=0) for v in outs_p]
    ss5, sconv, sssd, shg, sgla = [jnp.stack(v, axis=0) for v in outs_s]
    diag = lambda st, dk: _diag_state(st.reshape((depth * nb,) + st.shape[2:]), dk).reshape(
        depth, nb, N_HEADS, dk, 64)
    return (
        y_prompt, y_sample,
        ps5r.reshape(depth, nb, S5_GROUPS, S5_P), ps5i.reshape(depth, nb, S5_GROUPS, S5_P),
        pconv.reshape(depth, SSD_CONV - 1, nb, SSD_XBC).transpose(0, 2, 1, 3),
        diag(pssd, SSD_N), diag(phg, HG_K), diag(pgla, GLA_K),
        ss5[:, 0].reshape(depth, ns, S5_GROUPS, S5_P), ss5[:, 1].reshape(depth, ns, S5_GROUPS, S5_P),
        sconv.reshape(depth, ns, SSD_CONV - 1, SSD_XBC),
        sssd.reshape(depth, SSD_HEADS, SSD_N, SSD_P, ns).transpose(0, 4, 1, 2, 3),
        shg.reshape(depth, HG_HEADS, HG_K, HG_V, ns).transpose(0, 4, 1, 2, 3),
        sgla.reshape(depth, GLA_HEADS, GLA_K, GLA_V, ns).transpose(0, 4, 1, 2, 3),
    )
```

```python
import functools
import math

import numpy as np
import jax
import jax.numpy as jnp
from jax import lax
from jax.experimental import pallas as pl
from jax.experimental.pallas import tpu as pltpu

F32 = jnp.float32
BF16 = jnp.bfloat16

D_MODEL = 1024
GROUP_W = 256
S5_GROUPS, S5_CH, S5_P = 16, 16, 64
S5_STATE = S5_GROUPS * S5_P
SSD_HEADS, SSD_N, SSD_P, SSD_NGROUPS, SSD_CONV, SSD_XBC = 4, 64, 64, 2, 4, 512
HG_HEADS, HG_K, HG_V = 4, 64, 64
GLA_HEADS, GLA_K, GLA_V, GLA_RANK, GLA_TAU = 4, 32, 64, 16, 16.0
N_HEADS = 4
D_FF = 4096
EPS = 1e-6
LB_FLOOR = 1e-30

O_U, O_Z, O_XBC = 0, 256, 512
O_HQ, O_HF, O_HI, O_HGATE = 1024, 1280, 1536, 1792
O_GQ, O_GK, O_GV, O_GGATE = 2048, 2176, 2304, 2560
O_MISC = 2816
N_PROJ = 2944
LR_LO = SSD_HEADS

SUB = 16
NEG = -1e30
LOG2E = 1.4426950408889634
V7X_VMEM_BYTES = 64 * 1024 * 1024
VMEM_LIMIT = V7X_VMEM_BYTES * 7 // 8


def _sigmoid(x):
    return 0.5 * (1.0 + jnp.tanh(0.5 * x))


def _sigmoid_rel(x):
    return 1.0 / (1.0 + jnp.exp(-x))


def _silu(x):
    return x * _sigmoid(x)


def _softplus(x):
    return jnp.maximum(x, 0.0) + jnp.log1p(jnp.exp(-jnp.abs(x)))


def _log_sigmoid(x):
    return -_softplus(-x)


def _gelu_tanh(x):
    c = math.sqrt(2.0 / math.pi)
    return x * (0.5 * (1.0 + jnp.tanh(c * (x + 0.044715 * (x * x * x)))))


def _rms_scale(x, axis):
    return lax.rsqrt(jnp.mean(x * x, axis=axis, keepdims=True) + EPS)


def _dot(a, b):
    return jnp.dot(a, b, preferred_element_type=F32)


def _dot_nt(a, b):
    return lax.dot_general(a, b, (((1,), (1,)), ((), ())), preferred_element_type=F32)


def _dot_tn(a, b):
    return lax.dot_general(a, b, (((0,), (0,)), ((), ())), preferred_element_type=F32)


def _dot_split(a, b16, terms):
    acc, rest = None, a
    for _ in range(terms):
        piece = rest.astype(BF16)
        part = _dot(piece, b16)
        acc = part if acc is None else acc + part
        rest = rest - piece.astype(F32)
    return acc


def _idiv(x, n):
    shift = int(math.log2(n))
    assert 1 << shift == n
    return x >> shift


def _store_tiles(ref, row0, val):
    rows = val.shape[0]
    for j in range(val.shape[1] // 128):
        ref[j, row0:row0 + rows, :] = val[:, j * 128:(j + 1) * 128]


def _load_tiles(ref, row0, rows, n_tiles):
    return jnp.concatenate([ref[j, row0:row0 + rows, :] for j in range(n_tiles)], axis=1)


def _load_seq(ref, start, t_steps, nb, n_tiles):
    tiles = [ref[j, pl.ds(start, t_steps, stride=nb), :] for j in range(n_tiles)]
    return tiles[0] if n_tiles == 1 else jnp.concatenate(tiles, axis=1)


def _load_rows(ref, start, rows, n_tiles):
    tiles = [ref[j, pl.ds(start, rows), :] for j in range(n_tiles)]
    return tiles[0] if n_tiles == 1 else jnp.concatenate(tiles, axis=1)


def _gated_scan_tile(kd_lanes, dk, nb, t_steps, qf_ref, kp_ref, cp_ref, vp_ref, ob_ref, os_ref, st_ref,
                     s16_ref, j_ref, ko_ref, vo_ref, head_decay=False):
    rows = nb * t_steps
    sub_rows = SUB * nb
    n_sub = rows // sub_rows
    n_kt = kd_lanes // 128
    assert not head_decay or dk == t_steps

    def sub_body(sb):
        r0 = pl.multiple_of(sb * sub_rows, sub_rows)
        accs = [[None] * SUB for _ in range(2)]
        for j in range(SUB):
            reps = SUB - j
            rj = pl.multiple_of(r0 + nb * j, nb)
            q = _load_rows(qf_ref, rj, reps * nb, n_kt)
            c = _load_rows(cp_ref, rj, reps * nb, n_kt)
            k_j = jnp.tile(_load_rows(kp_ref, rj, nb, n_kt), (reps, 1))
            c_j = jnp.tile(_load_rows(cp_ref, rj, nb, n_kt), (reps, 1))
            w = _dot((q * k_j * jnp.exp2(c - c_j)).astype(BF16), j_ref[...])
            for n in range(2):
                v_j = vp_ref[n, pl.ds(rj, nb), :]
                for i in range(j, SUB):
                    term = w[(i - j) * nb:(i - j + 1) * nb, n * 128:(n + 1) * 128] * v_j
                    accs[n][i] = term if accs[n][i] is None else accs[n][i] + term
            yield
        for n in range(2):
            ob_ref[n, pl.ds(r0, sub_rows), :] = jnp.concatenate(accs[n], axis=0)

    t_idx = lax.broadcasted_iota(jnp.int32, (t_steps, kd_lanes), 0)
    ri = lax.broadcasted_iota(jnp.int32, (t_steps, N_HEADS * t_steps), 0)
    ci = lax.broadcasted_iota(jnp.int32, (t_steps, N_HEADS * t_steps), 1) & (t_steps - 1)
    lane_tile_w = min(kd_lanes, 128)
    heads_per_tile = lane_tile_w // dk
    tile_head = _idiv(lax.broadcasted_iota(jnp.int32, (64, lane_tile_w), 1), dk)
    levels = []
    c_sz = SUB
    while c_sz < t_steps:
        levels.append(c_sz)
        c_sz *= 2

    def seq_body(b):
        qb = _load_seq(qf_ref, b, t_steps, nb, n_kt)
        kb = _load_seq(kp_ref, b, t_steps, nb, n_kt)
        cb = _load_seq(cp_ref, b, t_steps, nb, n_kt)
        vb16 = _load_seq(vp_ref, b, t_steps, nb, 2).astype(BF16)
        c_last = cb[t_steps - 1:t_steps, :]

        o = _dot_nt((qb * jnp.exp2(cb)).astype(BF16), s16_ref[b])
        yield
        s = None
        if head_decay:
            cum_rows = [cb[:, t * 128:(t + 1) * 128].T for t in range(n_kt)]
            cum_j = jnp.concatenate(
                [jnp.broadcast_to(cum_rows[(h * dk) // 128][(h * dk) % 128:(h * dk) % 128 + 1, :],
                                  (t_steps, t_steps)) for h in range(N_HEADS)], axis=1)
            dmat = jnp.exp2(jnp.where(ri >= ci, cb - cum_j, NEG))
            s = _dot_nt(qb.astype(BF16), jnp.tile(kb.astype(BF16), (N_HEADS, 1)) * ko_ref[...]) * dmat
            yield
        for c_sz in ([] if head_decay else levels):
            pieces = []
            for m in range(t_steps // (2 * c_sz)):
                a_row = m * 2 * c_sz + c_sz - 1
                pieces.append(jnp.broadcast_to(cb[a_row:a_row + 1, :], (2 * c_sz, kd_lanes)))
            anchor = pieces[0] if len(pieces) == 1 else jnp.concatenate(pieces, axis=0)
            upper = (t_idx & c_sz) != 0
            q_l = (qb * jnp.exp2(jnp.where(upper, cb - anchor, NEG))).astype(BF16)
            k_l = (kb * jnp.exp2(jnp.where(upper, NEG, anchor - cb))).astype(BF16)
            s_l = _dot_nt(q_l, jnp.tile(k_l, (N_HEADS, 1)) * ko_ref[...])
            if 2 * c_sz < t_steps:
                shift = int(math.log2(2 * c_sz))
                s_l = jnp.where((ri >> shift) == (ci >> shift), s_l, 0.0)
            s = s_l if s is None else s + s_l
            yield
        if s is not None:
            o = o + _dot(s.astype(BF16), jnp.tile(vb16, (N_HEADS, 1)) * vo_ref[...])
        for j in range(2):
            os_ref[j, pl.ds(b, t_steps, stride=nb), :] = o[:, j * 128:(j + 1) * 128]
        yield

        upd = _dot_tn(vb16, (kb * jnp.exp2(c_last - cb)).astype(BF16))
        decay = jnp.exp2(c_last)
        for h in range(N_HEADS):
            r0, l0 = h * 64, (h // heads_per_tile) * lane_tile_w
            own = tile_head == (h % heads_per_tile)
            blk = (decay[:, l0:l0 + lane_tile_w] * st_ref[b, r0:r0 + 64, l0:l0 + lane_tile_w]
                   + jnp.where(own, upd[r0:r0 + 64, l0:l0 + lane_tile_w], 0.0))
            st_ref[b, r0:r0 + 64, l0:l0 + lane_tile_w] = blk
            s16_ref[b, r0:r0 + 64, l0:l0 + lane_tile_w] = blk.astype(BF16)

    assert nb % n_sub == 0
    seq_per_trip = nb // n_sub

    def trip(i, carry):
        sub = None if head_decay else sub_body(i)
        seqs = [seq_body(i * seq_per_trip + s) for s in range(seq_per_trip)]
        turn = 0
        while sub is not None or seqs:
            if seqs:
                turn %= len(seqs)
                if next(seqs[turn], "done") == "done":
                    seqs.pop(turn)
                else:
                    turn += 1
            if sub is not None and next(sub, "done") == "done":
                sub = None
        return carry

    lax.fori_loop(0, n_sub, trip, 0, unroll=4 if head_decay else 2)


def _cumsum_time(nb, t_steps, src_ref, cp_ref, kd_lanes):
    def body(t, c):
        r = pl.multiple_of(t * nb, nb)
        c = c + src_ref[pl.ds(r, nb), :kd_lanes] * LOG2E
        for j in range(kd_lanes // 128):
            cp_ref[j, pl.ds(r, nb), :] = c[:, j * 128:(j + 1) * 128]
        return c

    lax.fori_loop(0, t_steps, body, jnp.zeros((nb, kd_lanes), F32), unroll=4)


def _head_rms_scale(o, jn_ref):
    return lax.rsqrt(_dot_split(o * o, jn_ref[...], 2) + EPS)


def _mixer_prompt_kernel(nb, t_steps, batch_major_in,
                         x_ref, gmix_ref, win_ref, bblk_ref, lam_ref, cblk_ref, wglu_ref,
                         v256_ref, v128_ref, v512_ref, w2p_ref, ehd_ref, j256_ref, j128_ref, jn_ref,
                         ko256_ref, ko128_ref, wout_ref, wup_ref, wdn_ref,
                         mix_ref, s5_ref, conv_ref, st_ssd_ref, st_hg_ref, st_gla_ref,
                         wout16_ref, wup16_ref, wdn16_ref, *rest):
    wout16_ref[...] = wout_ref[...].astype(BF16)
    wup16_ref[...] = wup_ref[...].astype(BF16)
    wdn16_ref[...] = wdn_ref[...].astype(BF16)
    if batch_major_in:
        xtm_ref, xt_ref = rest[0], rest[-1]
        rest = rest[1:-1]
    (proj_ref, bu_ref, xe_ref, qf_ref, kp_ref, cp_ref, vp_ref, ob_ref, os_ref, la_ref,
     s16_ssd_ref, s16_hg_ref, s16_gla_ref, hn_ref) = rest
    rows = nb * t_steps
    conv_rows = (SSD_CONV - 1) * nb

    @pl.when(pl.program_id(0) == 0)
    def _():
        for ref in (s5_ref, st_ssd_ref, st_hg_ref, st_gla_ref, s16_ssd_ref, s16_hg_ref, s16_gla_ref):
            ref[...] = jnp.zeros_like(ref)
        xe_ref[0:conv_rows, :] = jnp.zeros((conv_rows, SSD_XBC), F32)

    if batch_major_in:
        for b in range(nb):
            for j in range(D_MODEL // 128):
                xt_ref[j, pl.ds(b, t_steps, stride=nb), :] = x_ref[b, :, j * 128:(j + 1) * 128]
        x = _load_tiles(xt_ref, 0, rows, D_MODEL // 128)
        xtm_ref[...] = x
    else:
        x = x_ref[...]
    hn_ref[...] = (x * _rms_scale(x, -1) * gmix_ref[...]).astype(BF16)

    def project(lo, hi):
        proj_ref[:, lo:hi] = _dot(hn_ref[...], win_ref[:, lo:hi])

    project(O_U, O_HQ)
    project(O_MISC, N_PROJ)

    u = proj_ref[:, O_U:O_U + GROUP_W]
    bu_ref[...] = _dot(u.astype(BF16), bblk_ref[...])
    lam_r = jnp.broadcast_to(lam_ref[0:1, :], (nb, S5_STATE))
    lam_i = jnp.broadcast_to(lam_ref[1:2, :], (nb, S5_STATE))

    def s5_step(t, h):
        h_r, h_i = h
        r = pl.multiple_of(t * nb, nb)
        n_r = lam_r * h_r - lam_i * h_i + bu_ref[pl.ds(r, nb), 0:S5_STATE]
        n_i = lam_r * h_i + lam_i * h_r + bu_ref[pl.ds(r, nb), S5_STATE:2 * S5_STATE]
        bu_ref[pl.ds(r, nb), 0:S5_STATE] = n_r
        bu_ref[pl.ds(r, nb), S5_STATE:2 * S5_STATE] = n_i
        return n_r, n_i

    h_r, h_i = lax.fori_loop(0, t_steps, s5_step, (s5_ref[0], s5_ref[1]), unroll=2)
    s5_ref[0] = h_r
    s5_ref[1] = h_i
    y = _dot(bu_ref[...].astype(BF16), cblk_ref[...]) + v256_ref[0:1, :] * u
    z = _gelu_tanh(y)
    o_s5 = z * _sigmoid(_dot(z.astype(BF16), wglu_ref[...]) + v256_ref[1:2, :])
    mix_ref[:, 0:GROUP_W] = o_s5.astype(mix_ref.dtype)

    xe_ref[conv_rows:conv_rows + rows, :] = proj_ref[:, O_XBC:O_XBC + SSD_XBC]
    conv = v512_ref[SSD_CONV:SSD_CONV + 1, :]
    for j in range(SSD_CONV):
        conv = conv + xe_ref[j * nb:j * nb + rows, :] * v512_ref[j:j + 1, :]
    tail = xe_ref[rows:rows + conv_rows, :]
    xe_ref[0:conv_rows, :] = tail
    conv_ref[...] = tail
    act = _silu(conv)
    xs = act[:, 0:GROUP_W]
    b_g = act[:, GROUP_W:GROUP_W + 128]
    c_g = act[:, GROUP_W + 128:GROUP_W + 256]
    misc = proj_ref[:, O_MISC:O_MISC + 128]
    dt = _softplus(misc + v128_ref[0:1, :])
    dt_hd = _dot_split(dt, ehd_ref[...], 3)
    la_ref[...] = dt_hd * v256_ref[7:8, :]
    _store_tiles(qf_ref, 0, jnp.concatenate(
        [c_g[:, 0:64], c_g[:, 0:64], c_g[:, 64:128], c_g[:, 64:128]], axis=1))
    _store_tiles(kp_ref, 0, jnp.concatenate(
        [b_g[:, 0:64], b_g[:, 0:64], b_g[:, 64:128], b_g[:, 64:128]], axis=1))
    _store_tiles(vp_ref, 0, xs * dt_hd)
    _cumsum_time(nb, t_steps, la_ref, cp_ref, 256)
    _gated_scan_tile(256, 64, nb, t_steps, qf_ref, kp_ref, cp_ref, vp_ref, ob_ref, os_ref, st_ssd_ref,
                     s16_ssd_ref, j256_ref, ko256_ref, ko256_ref, head_decay=True)
    project(O_HQ, O_MISC)
    y = _load_tiles(os_ref, 0, rows, 2) + v256_ref[2:3, :] * xs
    y = y * _silu(proj_ref[:, O_Z:O_Z + GROUP_W])
    y = y * _rms_scale(y, -1) * v256_ref[3:4, :]
    mix_ref[:, GROUP_W:2 * GROUP_W] = y.astype(mix_ref.dtype)

    sig_f = _sigmoid_rel(proj_ref[:, O_HF:O_HF + GROUP_W])
    nsig_f = 1.0 - sig_f
    la_ref[...] = jnp.log(sig_f + v256_ref[4:5, :] * nsig_f)
    _store_tiles(qf_ref, 0, _silu(proj_ref[:, O_HQ:O_HQ + GROUP_W]))
    _store_tiles(kp_ref, 0, v256_ref[5:6, :] * nsig_f)
    _store_tiles(vp_ref, 0, proj_ref[:, O_HI:O_HI + GROUP_W])
    _cumsum_time(nb, t_steps, la_ref, cp_ref, 256)
    _gated_scan_tile(256, 64, nb, t_steps, qf_ref, kp_ref, cp_ref, vp_ref, ob_ref, os_ref, st_hg_ref,
                     s16_hg_ref, j256_ref, ko256_ref, ko256_ref)
    o = _load_tiles(ob_ref, 0, rows, 2) + _load_tiles(os_ref, 0, rows, 2)
    o = o * _head_rms_scale(o, jn_ref) * v256_ref[6:7, :]
    o = o * _silu(proj_ref[:, O_HGATE:O_HGATE + GROUP_W])
    mix_ref[:, 2 * GROUP_W:3 * GROUP_W] = o.astype(mix_ref.dtype)

    gk = _dot(misc.astype(BF16), w2p_ref[...]) + v128_ref[2:3, :]
    la_ref[:, 0:128] = _log_sigmoid(gk) * (1.0 / GLA_TAU)
    _store_tiles(qf_ref, 0, proj_ref[:, O_GQ:O_GQ + 128] * (GLA_K ** -0.5))
    _store_tiles(kp_ref, 0, proj_ref[:, O_GK:O_GK + 128])
    _store_tiles(vp_ref, 0, proj_ref[:, O_GV:O_GV + GROUP_W])
    _cumsum_time(nb, t_steps, la_ref, cp_ref, 128)
    _gated_scan_tile(128, 32, nb, t_steps, qf_ref, kp_ref, cp_ref, vp_ref, ob_ref, os_ref, st_gla_ref,
                     s16_gla_ref, j128_ref, ko128_ref, ko256_ref)
    o = _load_tiles(ob_ref, 0, rows, 2) + _load_tiles(os_ref, 0, rows, 2)
    o = o * _head_rms_scale(o, jn_ref) * v256_ref[8:9, :]
    o = o * _silu(proj_ref[:, O_GGATE:O_GGATE + GROUP_W])
    mix_ref[:, 3 * GROUP_W:4 * GROUP_W] = o.astype(mix_ref.dtype)


def _const_spec(shape):
    nd = len(shape)
    return pl.BlockSpec(shape, lambda i, _nd=nd: (0,) * _nd, pipeline_mode=pl.Buffered(1))


def _layer_spec(arr, layer):
    nd = arr.ndim - 1
    return pl.BlockSpec((None,) + arr.shape[1:], lambda i, _nd=nd: (layer,) + (0,) * _nd,
                        pipeline_mode=pl.Buffered(1))


PROMPT_LAYER_PARAMS = ('gmix', 'win', 'bblk', 'lam', 'cblk', 'wglu', 'v256', 'v128', 'v512', 'w2p')
PROMPT_SHARED_PARAMS = ('ehd', 'j256', 'j128', 'jn')


def _mixer_prompt(x, nb, t_steps, p, layer):
    batch_major_in = x.ndim == 3
    rows_total = x.shape[0] * x.shape[1] if batch_major_in else x.shape[0]
    rows = nb * t_steps
    n_tiles = rows_total // rows
    layered = [p[k] for k in PROMPT_LAYER_PARAMS]
    shared = [p[k] for k in PROMPT_SHARED_PARAMS] + [
        jnp.asarray(_block_ones(N_HEADS * t_steps, t_steps, 256, 64), BF16),
        jnp.asarray(_block_ones(N_HEADS * t_steps, t_steps, 128, 32), BF16)]
    mlp_w = [p['wout32'], p['wup32'], p['wdn32']]
    assert all(w.shape[1] % (16 * n_tiles) == 0 for w in mlp_w)
    slab = lambda w: (w.shape[1] // n_tiles, w.shape[2])
    consts = layered + shared + mlp_w
    x_spec = (pl.BlockSpec((nb, t_steps, D_MODEL), lambda i: (0, i, 0)) if batch_major_in
              else pl.BlockSpec((rows, D_MODEL), lambda i: (i, 0)))
    in_specs = ([x_spec] + [_layer_spec(c, layer) for c in layered] + [_const_spec(c.shape) for c in shared]
                + [pl.BlockSpec((None,) + slab(w), lambda i: (layer, i, 0)) for w in mlp_w])
    conv_rows = (SSD_CONV - 1) * nb
    out_shape = (
        jax.ShapeDtypeStruct((rows_total, D_MODEL), BF16),
        jax.ShapeDtypeStruct((2, nb, S5_STATE), F32),
        jax.ShapeDtypeStruct((conv_rows, SSD_XBC), F32),
        jax.ShapeDtypeStruct((nb, 256, 256), F32),
        jax.ShapeDtypeStruct((nb, 256, 256), F32),
        jax.ShapeDtypeStruct((nb, 256, 128), F32),
    ) + tuple(jax.ShapeDtypeStruct(w.shape[1:], BF16) for w in mlp_w)
    out_specs = (
        pl.BlockSpec((rows, D_MODEL), lambda i: (i, 0)),
        pl.BlockSpec((2, nb, S5_STATE), lambda i: (0, 0, 0)),
        pl.BlockSpec((conv_rows, SSD_XBC), lambda i: (0, 0)),
        pl.BlockSpec((nb, 256, 256), lambda i: (0, 0, 0)),
        pl.BlockSpec((nb, 256, 256), lambda i: (0, 0, 0)),
        pl.BlockSpec((nb, 256, 128), lambda i: (0, 0, 0)),
    ) + tuple(pl.BlockSpec(slab(w), lambda i: (i, 0)) for w in mlp_w)
    scratch = [
        pltpu.VMEM((rows, N_PROJ), F32),
        pltpu.VMEM((rows, 2 * S5_STATE), F32),
        pltpu.VMEM((conv_rows + rows, SSD_XBC), F32),
        pltpu.VMEM((2, rows, 128), F32),
        pltpu.VMEM((2, rows, 128), F32),
        pltpu.VMEM((2, rows, 128), F32),
        pltpu.VMEM((2, rows, 128), F32),
        pltpu.VMEM((2, rows, 128), F32),
        pltpu.VMEM((2, rows, 128), F32),
        pltpu.VMEM((rows, 256), F32),
        pltpu.VMEM((nb, 256, 256), BF16),
        pltpu.VMEM((nb, 256, 256), BF16),
        pltpu.VMEM((nb, 256, 128), BF16),
        pltpu.VMEM((rows, D_MODEL), BF16),
    ]
    if batch_major_in:
        out_shape += (jax.ShapeDtypeStruct((rows_total, D_MODEL), F32),)
        out_specs += (pl.BlockSpec((rows, D_MODEL), lambda i: (i, 0)),)
        scratch.append(pltpu.VMEM((D_MODEL // 128, rows, 128), F32))
    return pl.pallas_call(
        functools.partial(_mixer_prompt_kernel, nb, t_steps, batch_major_in),
        grid=(n_tiles,),
        in_specs=in_specs,
        out_specs=out_specs,
        out_shape=out_shape,
        scratch_shapes=scratch,
        compiler_params=pltpu.CompilerParams(dimension_semantics=("arbitrary",),
                                             vmem_limit_bytes=VMEM_LIMIT),
        name="mixer_prompt",
    )(x, *consts)


KEY_SPLIT = 2
SAMPLE_LAYER_PARAMS = ('gmix', 'win', 'bblk', 'lam', 'cblk', 'wglu', 'v256', 'v128', 'v512', 'w2p')


def _mixer_sample_kernel(x_ref, gmix_ref, win_ref, bblk_ref, lam_ref, cblk_ref, wglu_ref,
                         v256_ref, v128_ref, v512_ref, w2p_ref, jn_ref,
                         s5_in, conv_in, ssd_in, hg_in, gla_in,
                         mix_ref, s5_out, conv_out, ssd_out, hg_out, gla_out,
                         proj_ref, xs_ref, act_ref, dt_ref, es_ref, qh_ref, kh_ref, eh_ref, vh_ref,
                         qg_ref, kg_ref, eg_ref, vg_ref, os_ref, oh_ref, og_ref):
    step = pl.program_id(0)
    h = _idiv(step, KEY_SPLIT)
    part = step & (KEY_SPLIT - 1)

    @pl.when(step == 0)
    def _():
        x = x_ref[...]
        hn = (x * _rms_scale(x, -1) * gmix_ref[...]).astype(BF16)
        proj_ref[...] = _dot(hn, win_ref[...])

        u = proj_ref[:, O_U:O_U + GROUP_W]
        bu = _dot(u.astype(BF16), bblk_ref[...])
        lam_r, lam_i = lam_ref[0:1, :], lam_ref[1:2, :]
        h0_r, h0_i = s5_in[0], s5_in[1]
        h_r = lam_r * h0_r - lam_i * h0_i + bu[:, 0:S5_STATE]
        h_i = lam_r * h0_i + lam_i * h0_r + bu[:, S5_STATE:2 * S5_STATE]
        s5_out[0] = h_r
        s5_out[1] = h_i
        y = (_dot(jnp.concatenate([h_r, h_i], axis=1).astype(BF16), cblk_ref[...])
             + v256_ref[0:1, :] * u)
        z = _gelu_tanh(y)
        o_s5 = z * _sigmoid(_dot(z.astype(BF16), wglu_ref[...]) + v256_ref[1:2, :])
        mix_ref[:, 0:GROUP_W] = o_s5.astype(mix_ref.dtype)

        xbc = proj_ref[:, O_XBC:O_XBC + SSD_XBC]
        conv = v512_ref[SSD_CONV:SSD_CONV + 1, :] + v512_ref[SSD_CONV - 1:SSD_CONV, :] * xbc
        for j in range(SSD_CONV - 1):
            conv = conv + v512_ref[j:j + 1, :] * conv_in[:, j * SSD_XBC:(j + 1) * SSD_XBC]
        conv_out[:, 0:2 * SSD_XBC] = conv_in[:, SSD_XBC:3 * SSD_XBC]
        conv_out[:, 2 * SSD_XBC:3 * SSD_XBC] = xbc
        act = _silu(conv)
        xs_ref[...] = act[:, 0:GROUP_W]
        act_ref[...] = act.T
        misc = proj_ref[:, O_MISC:O_MISC + 128]
        dt = _softplus(misc + v128_ref[0:1, :])
        dt_ref[...] = dt.T
        es_ref[...] = jnp.exp(dt * v128_ref[1:2, :]).T

        sig_f = _sigmoid_rel(proj_ref[:, O_HF:O_HF + GROUP_W])
        nsig_f = 1.0 - sig_f
        eh_ref[...] = (sig_f + v256_ref[4:5, :] * nsig_f).T
        kh_ref[...] = (v256_ref[5:6, :] * nsig_f).T
        qh_ref[...] = _silu(proj_ref[:, O_HQ:O_HQ + GROUP_W]).T
        vh_ref[...] = proj_ref[:, O_HI:O_HI + GROUP_W].T

        gk = _dot(misc.astype(BF16), w2p_ref[...]) + v128_ref[2:3, :]
        eg_ref[...] = jnp.exp(_log_sigmoid(gk) * (1.0 / GLA_TAU)).T
        kg_ref[...] = proj_ref[:, O_GK:O_GK + 128].T
        qg_ref[...] = (proj_ref[:, O_GQ:O_GQ + 128] * (GLA_K ** -0.5)).T
        vg_ref[...] = proj_ref[:, O_GV:O_GV + GROUP_W].T

    def head_rows():
        return pl.ds(pl.multiple_of(h * 64, 64), 64)

    def state_step(st_in, st_out, o_ref, e_row, k_row, q_row, v, head_keys):
        n_keys = head_keys // KEY_SPLIT
        key0 = part * n_keys

        def body(i, acc):
            r = pl.multiple_of(i * 64, 64)
            s_new = e_row(key0 + i) * st_in[pl.ds(r, 64), :] + k_row(key0 + i) * v
            st_out[pl.ds(r, 64), :] = s_new
            return acc + q_row(key0 + i) * s_new
        o = lax.fori_loop(0, n_keys, body, jnp.zeros_like(v), unroll=4)

        @pl.when(part == 0)
        def _():
            o_ref[head_rows(), :] = o

        @pl.when(part != 0)
        def _():
            o_ref[head_rows(), :] = o_ref[head_rows(), :] + o

    g = _idiv(h, SSD_HEADS // SSD_NGROUPS)
    e_h = es_ref[pl.ds(h, 1), :]
    state_step(ssd_in, ssd_out, os_ref,
               lambda i: e_h,
               lambda i: act_ref[pl.ds(GROUP_W + g * SSD_N + i, 1), :],
               lambda i: act_ref[pl.ds(GROUP_W + 128 + g * SSD_N + i, 1), :],
               act_ref[head_rows(), :] * dt_ref[pl.ds(h, 1), :], SSD_N)

    state_step(hg_in, hg_out, oh_ref,
               lambda i: eh_ref[pl.ds(h * HG_K + i, 1), :],
               lambda i: kh_ref[pl.ds(h * HG_K + i, 1), :],
               lambda i: qh_ref[pl.ds(h * HG_K + i, 1), :],
               vh_ref[head_rows(), :], HG_K)

    state_step(gla_in, gla_out, og_ref,
               lambda i: eg_ref[pl.ds(h * GLA_K + i, 1), :],
               lambda i: kg_ref[pl.ds(h * GLA_K + i, 1), :],
               lambda i: qg_ref[pl.ds(h * GLA_K + i, 1), :],
               vg_ref[head_rows(), :], GLA_K)

    @pl.when(step == N_HEADS * KEY_SPLIT - 1)
    def _():
        y = os_ref[...].T + v256_ref[2:3, :] * xs_ref[...]
        y = y * _silu(proj_ref[:, O_Z:O_Z + GROUP_W])
        mix_ref[:, GROUP_W:2 * GROUP_W] = (y * _rms_scale(y, -1) * v256_ref[3:4, :]).astype(mix_ref.dtype)
        o = oh_ref[...].T
        o = o * _head_rms_scale(o, jn_ref) * v256_ref[6:7, :] * _silu(proj_ref[:, O_HGATE:O_HGATE + GROUP_W])
        mix_ref[:, 2 * GROUP_W:3 * GROUP_W] = o.astype(mix_ref.dtype)
        o = og_ref[...].T
        o = o * _head_rms_scale(o, jn_ref) * v256_ref[8:9, :] * _silu(proj_ref[:, O_GGATE:O_GGATE + GROUP_W])
        mix_ref[:, 3 * GROUP_W:4 * GROUP_W] = o.astype(mix_ref.dtype)


def _mixer_sample(x, p, layer, s5, conv, ssd_t, hg_t, gla_t):
    nbatch = x.shape[0]
    layered = [p[k] for k in SAMPLE_LAYER_PARAMS]
    n_steps = N_HEADS * KEY_SPLIT
    head_in = lambda a: pl.BlockSpec((None, a.shape[1] // n_steps, nbatch), lambda i: (layer, i, 0))
    head_blk = lambda a: pl.BlockSpec((a.shape[1] // n_steps, nbatch), lambda i: (i, 0))
    res_blk = lambda shape: pl.BlockSpec(shape, lambda i, _nd=len(shape): (0,) * _nd)
    out_shape = (
        jax.ShapeDtypeStruct((nbatch, D_MODEL), BF16),
        jax.ShapeDtypeStruct(s5.shape[1:], F32),
        jax.ShapeDtypeStruct(conv.shape[1:], F32),
        jax.ShapeDtypeStruct(ssd_t.shape[1:], F32),
        jax.ShapeDtypeStruct(hg_t.shape[1:], F32),
        jax.ShapeDtypeStruct(gla_t.shape[1:], F32),
    )
    fm = lambda n: pltpu.VMEM((n, nbatch), F32)
    scratch = [
        pltpu.VMEM((nbatch, N_PROJ), F32),
        pltpu.VMEM((nbatch, GROUP_W), F32),
        fm(SSD_XBC),
        fm(128), fm(128),
        fm(GROUP_W), fm(GROUP_W), fm(GROUP_W), fm(GROUP_W),
        fm(128), fm(128), fm(128), fm(GROUP_W),
        fm(GROUP_W), fm(GROUP_W), fm(GROUP_W),
    ]
    return pl.pallas_call(
        _mixer_sample_kernel,
        grid=(n_steps,),
        in_specs=([_const_spec(x.shape)] + [_layer_spec(c, layer) for c in layered]
                  + [_const_spec(p['jn'].shape), _layer_spec(s5, layer), _layer_spec(conv, layer),
                     head_in(ssd_t), head_in(hg_t), head_in(gla_t)]),
        out_specs=(res_blk((nbatch, D_MODEL)), res_blk(s5.shape[1:]), res_blk(conv.shape[1:]),
                   head_blk(ssd_t), head_blk(hg_t), head_blk(gla_t)),
        out_shape=out_shape,
        scratch_shapes=scratch,
        compiler_params=pltpu.CompilerParams(dimension_semantics=("arbitrary",),
                                             vmem_limit_bytes=VMEM_LIMIT),
        name="mixer_sample",
    )(x, *layered, p['jn'], s5, conv, ssd_t, hg_t, gla_t)


FF_TILE = 512


def _mlp_kernel(final, n_seq, mix_ref, x_ref, wout_ref, g_ref, wup_ref, wdn_ref, gfin_ref, o_ref, *scratch):
    x1 = x_ref[...] + _dot(mix_ref[...], wout_ref[...])
    hn = (x1 * _rms_scale(x1, -1) * g_ref[...]).astype(BF16)
    acc = x1
    for f in range(D_FF // FF_TILE):
        up = jnp.maximum(_dot(hn, wup_ref[:, f * FF_TILE:(f + 1) * FF_TILE]), 0.0)
        acc = acc + _dot((up * up).astype(BF16), wdn_ref[f * FF_TILE:(f + 1) * FF_TILE, :])
    if final:
        acc = acc * _rms_scale(acc, -1) * gfin_ref[...]
    if n_seq is None:
        o_ref[...] = acc
    else:
        tm_ref, = scratch
        _store_tiles(tm_ref, 0, acc)
        steps = acc.shape[0] // n_seq
        for b in range(n_seq):
            for j in range(D_MODEL // 128):
                o_ref[b, :, j * 128:(j + 1) * 128] = tm_ref[j, pl.ds(b, steps, stride=n_seq), :]


def _mlp(mix, x, weights16, p, layer, final, tm, n_seq=None):
    rows_total = x.shape[0]
    tm = min(tm, rows_total)
    wout, wup, wdn = weights16
    consts = [wout, p['gmlp'], wup, wdn, p['gfin']]
    specs = [_const_spec(wout.shape), _layer_spec(p['gmlp'], layer), _const_spec(wup.shape),
             _const_spec(wdn.shape), _const_spec(p['gfin'].shape)]
    if n_seq is None:
        out_spec = pl.BlockSpec((tm, D_MODEL), lambda i: (i, 0))
        out_shape = jax.ShapeDtypeStruct((rows_total, D_MODEL), F32)
        scratch = []
    else:
        out_spec = pl.BlockSpec((n_seq, tm // n_seq, D_MODEL), lambda i: (0, i, 0))
        out_shape = jax.ShapeDtypeStruct((n_seq, rows_total // n_seq, D_MODEL), F32)
        scratch = [pltpu.VMEM((D_MODEL // 128, tm, 128), F32)]
    return pl.pallas_call(
        functools.partial(_mlp_kernel, final, n_seq),
        grid=(rows_total // tm,),
        in_specs=[pl.BlockSpec((tm, D_MODEL), lambda i: (i, 0)),
                  pl.BlockSpec((tm, D_MODEL), lambda i: (i, 0))] + specs,
        out_specs=out_spec,
        out_shape=out_shape,
        scratch_shapes=scratch,
        compiler_params=pltpu.CompilerParams(dimension_semantics=("parallel",),
                                             vmem_limit_bytes=VMEM_LIMIT),
        name="out_mlp",
    )(mix, x, *consts)


N_IN = 2836
IN_MISC_LO, IN_LR_LO = 1024, 2820
PACK_ROWS = 256


def _pack_win_kernel(w_ref, o_ref):
    w = w_ref[...]
    n_dt, n_lr = SSD_HEADS, GLA_RANK
    o_ref[:, 0:IN_MISC_LO] = w[:, 0:IN_MISC_LO].astype(BF16)
    o_ref[:, IN_MISC_LO:O_MISC] = w[:, IN_MISC_LO + n_dt:IN_LR_LO].astype(BF16)
    tail = jnp.concatenate([w[:, IN_MISC_LO:IN_MISC_LO + n_dt], w[:, IN_LR_LO:IN_LR_LO + n_lr],
                            jnp.zeros((w.shape[0], 128 - n_dt - n_lr), w.dtype)], axis=1)
    o_ref[:, O_MISC:N_PROJ] = tail.astype(BF16)


def _pack_win(w_in):
    depth, d_in, n_in = w_in.shape
    assert n_in == N_IN and d_in % PACK_ROWS == 0
    return pl.pallas_call(
        _pack_win_kernel,
        grid=(depth, d_in // PACK_ROWS),
        in_specs=[pl.BlockSpec((None, PACK_ROWS, n_in), lambda l, i: (l, i, 0))],
        out_specs=pl.BlockSpec((None, PACK_ROWS, N_PROJ), lambda l, i: (l, i, 0)),
        out_shape=jax.ShapeDtypeStruct((depth, d_in, N_PROJ), BF16),
        compiler_params=pltpu.CompilerParams(dimension_semantics=("parallel", "parallel")),
        name="pack_win",
    )(w_in.astype(F32))


def _block_ones(n_rows, row_blk, n_cols, col_blk, scale=1.0):
    r = np.arange(n_rows)[:, None] // row_blk
    c = np.arange(n_cols)[None, :] // col_blk
    return (r == c).astype(np.float32) * scale


def _prepare(norm_mix_g, w_in, s5_lam_re, s5_lam_im, s5_log_dt, s5_b_re, s5_b_im, s5_c_re, s5_c_im,
             s5_d, s5_w_glu, s5_b_glu, ssd_conv_w, ssd_conv_b, ssd_dt_bias, ssd_a_log, ssd_d, ssd_norm_g,
             hg_lb_logits, hg_norm_g, gla_w_gk2, gla_b_gk, gla_norm_g):
    depth = w_in.shape[0]
    win = _pack_win(w_in)

    lam_re, lam_im = s5_lam_re.astype(F32), s5_lam_im.astype(F32)
    dt = jnp.exp(s5_log_dt.astype(F32))[..., None]
    ea = jnp.exp(lam_re * dt)
    lb_r, lb_i = ea * jnp.cos(lam_im * dt), ea * jnp.sin(lam_im * dt)
    den = lam_re * lam_re + lam_im * lam_im
    cr = ((lb_r - 1.0) * lam_re + lb_i * lam_im) / den
    ci = (lb_i * lam_re - (lb_r - 1.0) * lam_im) / den
    bb_r = cr[..., None] * s5_b_re - ci[..., None] * s5_b_im
    bb_i = cr[..., None] * s5_b_im + ci[..., None] * s5_b_re
    eye_g = jnp.eye(S5_GROUPS, dtype=F32)
    blk_b = lambda t: jnp.einsum('lgpc,gh->lgchp', t, eye_g).reshape(depth, GROUP_W, S5_STATE)
    blk_c = lambda t: jnp.einsum('lgcp,gh->lgphc', t, eye_g).reshape(depth, S5_STATE, GROUP_W)
    bblk = jnp.concatenate([blk_b(bb_r), blk_b(bb_i)], axis=-1)
    cblk = jnp.concatenate([blk_c(s5_c_re.astype(F32)), -blk_c(s5_c_im.astype(F32))], axis=1)
    lam = jnp.stack([lb_r.reshape(depth, S5_STATE), lb_i.reshape(depth, S5_STATE)], axis=1)

    sm = jax.nn.softmax(hg_lb_logits.astype(F32), axis=0)
    lb = jnp.cumsum(sm, axis=0) - sm[0:1]
    lb_floor = jnp.maximum(lb, LB_FLOOR)
    a = -jnp.exp(ssd_a_log.astype(F32))
    rep64 = lambda t: jnp.repeat(t, 64, axis=-1)
    zeros256 = jnp.zeros((depth, GROUP_W), F32)
    v256 = jnp.stack([s5_d, s5_b_glu, rep64(ssd_d), ssd_norm_g, lb_floor, 1.0 - lb, hg_norm_g, rep64(a),
                      jnp.tile(gla_norm_g, (1, GLA_HEADS))] + [zeros256] * 7, axis=1).astype(F32)
    pad128 = lambda t: jnp.pad(t, ((0, 0), (0, 128 - t.shape[-1])))
    zeros128 = jnp.zeros((depth, 128), F32)
    v128 = jnp.stack([pad128(ssd_dt_bias), pad128(a), gla_b_gk] + [zeros128] * 5, axis=1).astype(F32)
    v512 = jnp.concatenate([ssd_conv_w, ssd_conv_b[:, None, :], jnp.zeros((depth, 3, SSD_XBC), F32)],
                           axis=1).astype(F32)
    w2p = jnp.pad(gla_w_gk2, ((0, 0), (LR_LO, 128 - LR_LO - GLA_RANK), (0, 0))).astype(BF16)

    return dict(
        gmix=norm_mix_g[:, None, :].astype(F32), win=win, bblk=bblk.astype(BF16), lam=lam,
        cblk=cblk.astype(BF16), wglu=s5_w_glu.astype(BF16), v256=v256, v128=v128, v512=v512, w2p=w2p,
        ehd=jnp.asarray(_block_ones(128, 1, 256, 64) * (np.arange(128)[:, None] < SSD_HEADS), BF16),
        j256=jnp.asarray(_block_ones(256, 64, 256, 64), BF16),
        j128=jnp.asarray(_block_ones(128, 32, 256, 64), BF16),
        jn=jnp.asarray(_block_ones(256, 64, 256, 64, 1.0 / 64), BF16),
    )


def _diag_state(st, dk):
    nb = st.shape[0]
    s = st.reshape(nb, N_HEADS, 64, N_HEADS, dk)
    idx = jnp.arange(N_HEADS)
    s = s[:, idx, :, idx, :]
    return s.transpose(1, 0, 3, 2)


def kernel(x_prompt, x_sample, state_s5_re, state_s5_im, state_ssd_conv, state_ssd, state_hgrn, state_gla,
           norm_mix_g, w_in, s5_lam_re, s5_lam_im, s5_log_dt, s5_b_re, s5_b_im, s5_c_re, s5_c_im,
           s5_d, s5_w_glu, s5_b_glu, ssd_conv_w, ssd_conv_b, ssd_dt_bias, ssd_a_log, ssd_d, ssd_norm_g,
           hg_lb_logits, hg_norm_g, gla_w_gk2, gla_b_gk, gla_norm_g, w_out, norm_mlp_g, w_up, w_down,
           norm_final_g):
    nb, seq, _ = x_prompt.shape
    ns = x_sample.shape[0]
    depth = w_in.shape[0]
    t_steps = min(64, seq)

    p = _prepare(norm_mix_g, w_in, s5_lam_re, s5_lam_im, s5_log_dt, s5_b_re, s5_b_im, s5_c_re, s5_c_im,
                 s5_d, s5_w_glu, s5_b_glu, ssd_conv_w, ssd_conv_b, ssd_dt_bias, ssd_a_log, ssd_d,
                 ssd_norm_g, hg_lb_logits, hg_norm_g, gla_w_gk2, gla_b_gk, gla_norm_g)
    p.update(wout32=w_out.astype(F32), wup32=w_up.astype(F32), wdn32=w_down.astype(F32),
             gmlp=norm_mlp_g[:, None, :].astype(F32), gfin=norm_final_g[None, :].astype(F32))

    xp = x_prompt.astype(F32)
    xs = x_sample.astype(F32).reshape(ns, D_MODEL)

    s5s = jnp.stack([state_s5_re.reshape(depth, ns, S5_STATE), state_s5_im.reshape(depth, ns, S5_STATE)],
                    axis=1).astype(F32)
    convs = state_ssd_conv.reshape(depth, ns, (SSD_CONV - 1) * SSD_XBC).astype(F32)
    ssds = state_ssd.transpose(0, 2, 3, 4, 1).reshape(depth, SSD_HEADS * SSD_N * SSD_P, ns).astype(F32)
    hgs = state_hgrn.transpose(0, 2, 3, 4, 1).reshape(depth, HG_HEADS * HG_K * HG_V, ns).astype(F32)
    glas = state_gla.transpose(0, 2, 3, 4, 1).reshape(depth, GLA_HEADS * GLA_K * GLA_V, ns).astype(F32)

    outs_p = [[] for _ in range(6)]
    outs_s = [[] for _ in range(5)]
    for l in range(depth):
        final = l == depth - 1
        mix, s5, conv, st_ssd, st_hg, st_gla, *w16, = _mixer_prompt(xp, nb, t_steps, p, l)
        if len(w16) > 3:
            xp = w16.pop()
        xp = _mlp(mix, xp, w16, p, l, final, 2 * nb * t_steps, n_seq=nb if final else None)
        for dst, val in zip(outs_p, (s5[0], s5[1], conv, st_ssd, st_hg, st_gla)):
            dst.append(val)

        res = _mixer_sample(xs, p, l, s5s, convs, ssds, hgs, glas)
        xs = _mlp(res[0], xs, w16, p, l, final, 512)
        for dst, val in zip(outs_s, res[1:]):
            dst.append(val)

    y_prompt = xp
    y_sample = xs.reshape(ns, 1, D_MODEL)
    ps5r, ps5i, pconv, pssd, phg, pgla = [jnp.stack(v, axis=0) for v in outs_p]
    ss5, sconv, sssd, shg, sgla = [jnp.stack(v, axis=0) for v in outs_s]
    diag = lambda st, dk: _diag_state(st.reshape((depth * nb,) + st.shape[2:]), dk).reshape(
        depth, nb, N_HEADS, dk, 64)
    return (
        y_prompt, y_sample,
        ps5r.reshape(depth, nb, S5_GROUPS, S5_P), ps5i.reshape(depth, nb, S5_GROUPS, S5_P),
        pconv.reshape(depth, SSD_CONV - 1, nb, SSD_XBC).transpose(0, 2, 1, 3),
        diag(pssd, SSD_N), diag(phg, HG_K), diag(pgla, GLA_K),
        ss5[:, 0].reshape(depth, ns, S5_GROUPS, S5_P), ss5[:, 1].reshape(depth, ns, S5_GROUPS, S5_P),
        sconv.reshape(depth, ns, SSD_CONV - 1, SSD_XBC),
        sssd.reshape(depth, SSD_HEADS, SSD_N, SSD_P, ns).transpose(0, 4, 1, 2, 3),
        shg.reshape(depth, HG_HEADS, HG_K, HG_V, ns).transpose(0, 4, 1, 2, 3),
        sgla.reshape(depth, GLA_HEADS, GLA_K, GLA_V, ns).transpose(0, 4, 1, 2, 3),
    )
```
